```python
import jax, jax.numpy as jnp
from jax import lax
import numpy as np

D_MODEL = 1024
BATCH = 2
SEQ = 8192
DEPTH = 1

CHUNK = 64
Q_BLOCK = 128
EPS = 1e-6

SB_HEADS = 8
SB_HEAD_DIM = 64
SB_WIDTH = SB_HEADS * SB_HEAD_DIM

MLA_HEADS = 8
MLA_NOPE_DIM = 64
MLA_ROPE_DIM = 32
MLA_V_DIM = 64
MLA_Q_LORA = 384
MLA_KV_LORA = 256
MLA_QK_DIM = MLA_NOPE_DIM + MLA_ROPE_DIM
MLA_WIDTH = MLA_HEADS * MLA_V_DIM
ROPE_THETA = 10000.0

N_BRANCH = 2
IN_SPLIT_SIZES = (SB_WIDTH, SB_WIDTH, SB_WIDTH, SB_WIDTH,
                  MLA_Q_LORA, MLA_KV_LORA, MLA_ROPE_DIM, MLA_WIDTH,
                  N_BRANCH * D_MODEL)
IN_COLS = sum(IN_SPLIT_SIZES)

kernel_name = "hybrid_stickbreaking_mla_gated_block"


def rmsnorm(x, g):
    x32 = x.astype(jnp.float32)
    inv = lax.rsqrt(jnp.mean(x32 * x32, axis=-1, keepdims=True) + EPS)
    return (x32 * inv).astype(x.dtype) * g


def rope_tables(seq_len):
    half = MLA_ROPE_DIM // 2
    inv_freq = ROPE_THETA ** (-jnp.arange(half, dtype=jnp.float32) / half)
    ang = jnp.arange(seq_len, dtype=jnp.float32)[:, None] * inv_freq[None, :]
    return jnp.cos(ang), jnp.sin(ang)


def apply_rope(x, cos, sin):
    half = x.shape[-1] // 2
    x32 = x.astype(jnp.float32)
    x1, x2 = x32[..., :half], x32[..., half:]
    c, s = cos[None, :, None, :], sin[None, :, None, :]
    return jnp.concatenate([x1 * c - x2 * s, x1 * s + x2 * c], axis=-1).astype(x.dtype)


def to_query_blocks(q):
    b, s, h, d = q.shape
    return q.reshape(b, s // Q_BLOCK, Q_BLOCK, h, d).transpose(1, 0, 3, 2, 4)


def from_query_blocks(o):
    nb, b, h, qb, d = o.shape
    return o.transpose(1, 0, 3, 2, 4).reshape(b, nb * qb, h, d)


def stick_breaking_attention(q, k, v):
    seq_len = q.shape[1]
    scale = q.shape[-1] ** -0.5
    key_pos = jnp.arange(seq_len)
    qb = to_query_blocks(q)

    def block(args):
        q_blk, i = args
        q_pos = i * Q_BLOCK + jnp.arange(Q_BLOCK)
        z = jnp.einsum("bhqd,bshd->bhqs", q_blk, k).astype(jnp.float32) * scale
        strict = key_pos[None, :] < q_pos[:, None]
        log_beta = jax.nn.log_sigmoid(z)
        log_1m = jnp.where(strict, jax.nn.log_sigmoid(-z), 0.0)
        later = lax.cumsum(log_1m, axis=3, reverse=True) - log_1m
        w = jnp.where(strict, jnp.exp(log_beta + later), 0.0)
        return jnp.einsum("bhqs,bshd->bhqd", w.astype(v.dtype), v)

    o = lax.map(block, (qb, jnp.arange(qb.shape[0])))
    return from_query_blocks(o)


def chunk_causal_softmax_attention(q, k, v):
    seq_len = q.shape[1]
    scale = q.shape[-1] ** -0.5
    key_chunk = jnp.arange(seq_len) // CHUNK
    qb = to_query_blocks(q)

    def block(args):
        q_blk, i = args
        q_chunk = (i * Q_BLOCK + jnp.arange(Q_BLOCK)) // CHUNK
        allowed = key_chunk[None, :] <= q_chunk[:, None]
        z = jnp.einsum("bhqd,bshd->bhqs", q_blk, k).astype(jnp.float32) * scale
        p = jax.nn.softmax(jnp.where(allowed, z, -jnp.inf), axis=-1)
        return jnp.einsum("bhqs,bshd->bhqd", p.astype(v.dtype), v)

    o = lax.map(block, (qb, jnp.arange(qb.shape[0])))
    return from_query_blocks(o)


def setup_inputs(seed: int = 0) -> dict:
    key = jax.random.key(seed)
    ks = jax.random.split(key, 16)
    f32 = jnp.float32

    def nrm(k, shape, fan_in):
        return jax.random.normal(k, shape, f32) * (fan_in ** -0.5)

    def gain(k, n):
        return 1.0 + 0.01 * jax.random.normal(k, (DEPTH, n), f32)

    return {
        "x": jax.random.normal(ks[0], (BATCH, SEQ, D_MODEL), f32),
        "norm_in_g": gain(ks[1], D_MODEL),
        "w_in": nrm(ks[2], (DEPTH, D_MODEL, IN_COLS), D_MODEL),
        "b_gate": 0.1 * jax.random.normal(ks[3], (DEPTH, N_BRANCH * D_MODEL), f32),
        "q_norm_g": gain(ks[4], MLA_Q_LORA),
        "w_q_up": nrm(ks[5], (DEPTH, MLA_Q_LORA, MLA_HEADS * MLA_QK_DIM), MLA_Q_LORA),
        "kv_norm_g": gain(ks[6], MLA_KV_LORA),
        "w_kv_up": nrm(ks[7], (DEPTH, MLA_KV_LORA, MLA_HEADS * (MLA_NOPE_DIM + MLA_V_DIM)), MLA_KV_LORA),
        "w_o_sb": nrm(ks[8], (DEPTH, SB_WIDTH, D_MODEL), SB_WIDTH),
        "w_o_mla": nrm(ks[9], (DEPTH, MLA_WIDTH, D_MODEL), MLA_WIDTH),
        "w_out": nrm(ks[10], (DEPTH, D_MODEL, D_MODEL), D_MODEL),
        "norm_f_g": 1.0 + 0.01 * jax.random.normal(ks[11], (D_MODEL,), f32),
    }


def reference(x, norm_in_g, w_in, b_gate, q_norm_g, w_q_up, kv_norm_g, w_kv_up,
              w_o_sb, w_o_mla, w_out, norm_f_g):
    b, s, _ = x.shape
    cos, sin = rope_tables(s)
    offsets = [int(o) for o in np.cumsum(IN_SPLIT_SIZES)[:-1]]

    for l in range(DEPTH):
        h = rmsnorm(x, norm_in_g[l])
        proj = jnp.einsum("bsd,de->bse", h, w_in[l])
        (q_sb, k_sb, v_sb, gate_sb, c_q, c_kv, k_rope,
         gate_mla, gate_logits) = jnp.split(proj, offsets, axis=-1)

        o_sb = stick_breaking_attention(
            q_sb.reshape(b, s, SB_HEADS, SB_HEAD_DIM),
            k_sb.reshape(b, s, SB_HEADS, SB_HEAD_DIM),
            v_sb.reshape(b, s, SB_HEADS, SB_HEAD_DIM)).reshape(b, s, SB_WIDTH)
        y_sb = jnp.einsum("bsc,cd->bsd", o_sb * jax.nn.silu(gate_sb), w_o_sb[l])

        q = jnp.einsum("bsr,re->bse", rmsnorm(c_q, q_norm_g[l]), w_q_up[l])
        q = q.reshape(b, s, MLA_HEADS, MLA_QK_DIM)
        q_nope, q_rope = q[..., :MLA_NOPE_DIM], q[..., MLA_NOPE_DIM:]
        q_rope = apply_rope(q_rope, cos, sin)
        kv = jnp.einsum("bsr,re->bse", rmsnorm(c_kv, kv_norm_g[l]), w_kv_up[l])
        kv = kv.reshape(b, s, MLA_HEADS, MLA_NOPE_DIM + MLA_V_DIM)
        k_nope, v_mla = kv[..., :MLA_NOPE_DIM], kv[..., MLA_NOPE_DIM:]
        k_rope = apply_rope(k_rope[:, :, None, :], cos, sin)
        q_full = jnp.concatenate([q_nope, q_rope], axis=-1)
        k_full = jnp.concatenate(
            [k_nope, jnp.broadcast_to(k_rope, (b, s, MLA_HEADS, MLA_ROPE_DIM))], axis=-1)
        o_mla = chunk_causal_softmax_attention(q_full, k_full, v_mla).reshape(b, s, MLA_WIDTH)
        y_mla = jnp.einsum("bsc,cd->bsd", o_mla * jax.nn.silu(gate_mla), w_o_mla[l])

        g = jax.nn.sigmoid(gate_logits + b_gate[l]).reshape(b, s, N_BRANCH, D_MODEL)
        merged = g[:, :, 0, :] * y_sb + g[:, :, 1, :] * y_mla
        x = x + jnp.einsum("bsd,de->bse", merged, w_out[l])

    return rmsnorm(x, norm_f_g)
```

```python
import functools

import jax
import jax.numpy as jnp
import numpy as np
from jax import lax
from jax.experimental import pallas as pl
from jax.experimental.pallas import tpu as pltpu

EPS = 1e-6
CHUNK = 64

SB_HEADS = 8
SB_HEAD_DIM = 64
SB_WIDTH = SB_HEADS * SB_HEAD_DIM

MLA_HEADS = 8
MLA_NOPE_DIM = 64
MLA_ROPE_DIM = 32
MLA_V_DIM = 64
MLA_Q_LORA = 384
MLA_KV_LORA = 256
MLA_QK_DIM = MLA_NOPE_DIM + MLA_ROPE_DIM
MLA_WIDTH = MLA_HEADS * MLA_V_DIM
ROPE_THETA = 10000.0

LANES = 128
HEAD_PAIRS = SB_HEADS // 2
ATT_BLOCK = 256
TOKEN_TILE = 512
VMEM_LIMIT_BYTES = 56 * 1024 * 1024

F32 = jnp.float32
BF16 = jnp.bfloat16


def _dot(a, b):
    return jnp.dot(a, b, preferred_element_type=F32)


def _rms(x, g):
    inv = lax.rsqrt(jnp.mean(x * x, axis=-1, keepdims=True) + EPS)
    return (x * inv) * g


def _proj_kernel(x_ref, gin_ref, wqkv_ref, wcq_ref, wckv_ref, wkr1_ref, wkr2_ref,
                 qng_ref, kvng_ref, wq1_ref, wq2_ref, wkn_ref, wv_ref,
                 cq_ref, sq_ref, ck_ref, sk_ref,
                 qsb_ref, ksb_ref, vsb_ref, qf_ref, kf_ref, vm_ref):
    h = _rms(x_ref[...], gin_ref[...]).astype(BF16)

    for c, o_ref in enumerate((qsb_ref, ksb_ref, vsb_ref)):
        o_ref[...] = _dot(h, wqkv_ref[:, c * SB_WIDTH:(c + 1) * SB_WIDTH]).astype(BF16)

    cq = _rms(_dot(h, wcq_ref[...]), qng_ref[...]).astype(BF16)
    ckv = _rms(_dot(h, wckv_ref[...]), kvng_ref[...]).astype(BF16)

    k_rope = _dot(h, wkr1_ref[...]) * ck_ref[...] + _dot(h, wkr2_ref[...]) * sk_ref[...]
    cq_t, sq_t = cq_ref[...], sq_ref[...]
    for hd in range(MLA_HEADS):
        sl = slice(hd * LANES, (hd + 1) * LANES)
        q1 = _dot(cq, wq1_ref[:, sl])
        q2 = _dot(cq, wq2_ref[:, sl])
        qf_ref[:, sl] = (q1 * cq_t + q2 * sq_t).astype(BF16)
        kf_ref[:, sl] = (_dot(ckv, wkn_ref[:, sl]) + k_rope).astype(BF16)
    vm_ref[...] = _dot(ckv, wv_ref[...]).astype(BF16)


def _proj_call(x2, gin, wts, tabs):
    n, d = x2.shape
    s = tabs[0].shape[0]
    tm = min(TOKEN_TILE, s)
    row = lambda i: (i, 0)
    fix = lambda i: (0, 0)
    w_specs = [pl.BlockSpec(w.shape, fix) for w in wts]
    t_specs = [pl.BlockSpec((tm, LANES), lambda i: (i % (s // tm), 0)) for _ in tabs]
    outs = [(SB_WIDTH, BF16)] * 3 + [(MLA_HEADS * LANES, BF16)] * 2 + [(MLA_WIDTH, BF16)]
    return pl.pallas_call(
        _proj_kernel,
        grid=(n // tm,),
        in_specs=[pl.BlockSpec((tm, d), row), pl.BlockSpec(gin.shape, fix)] + w_specs + t_specs,
        out_specs=[pl.BlockSpec((tm, w), row) for w, _ in outs],
        out_shape=[jax.ShapeDtypeStruct((n, w), dt) for w, dt in outs],
        compiler_params=pltpu.CompilerParams(
            dimension_semantics=("arbitrary",), vmem_limit_bytes=VMEM_LIMIT_BYTES),
        name="proj",
    )(x2, gin, *wts, *tabs)


def _transpose_v(v_ref, vt_ref, nblk):
    t = ATT_BLOCK

    def body(j, carry):
        blk = v_ref[pl.ds(pl.multiple_of(j * t, t), t), :]
        vt_ref[j] = blk.astype(F32).T.astype(BF16)
        return carry

    lax.fori_loop(0, nblk, body, 0)


def _head_rows():
    return lax.broadcasted_iota(jnp.int32, (LANES, ATT_BLOCK), 0) < SB_HEAD_DIM


def _sb_kernel(q_ref, k_ref, v_ref, u_ref, o_ref, vt_ref, acc_ref, *, nblk):
    t = ATT_BLOCK
    qi = pl.program_id(2)

    @pl.when(qi == 0)
    def _():
        _transpose_v(v_ref, vt_ref, nblk)

    first = _head_rows()
    q_t = q_ref[...].astype(F32).T
    q_heads = (jnp.where(first, q_t, 0.0).astype(BF16), jnp.where(first, 0.0, q_t).astype(BF16))

    def block(j, r, strict):
        k_blk = k_ref[pl.ds(pl.multiple_of(j * t, t), t), :]
        vt_blk = vt_ref[j]
        r_new = []
        for hd in range(2):
            z = _dot(k_blk, q_heads[hd])
            soft = jnp.log(1.0 + jnp.exp(jnp.minimum(z, -z)))
            log_beta = jnp.minimum(z, 0.0) - soft
            log_1m = log_beta - z
            if strict is not None:
                log_1m = jnp.where(strict, log_1m, 0.0)
            log_1m = log_1m.astype(BF16)
            later = _dot(u_ref[...], log_1m)
            w = jnp.exp(log_beta + later + r[hd])
            if strict is not None:
                w = jnp.where(strict, w, 0.0)
            acc_ref[hd] += _dot(vt_blk, w.astype(BF16))
            r_new.append(r[hd] + later[0:1, :] + log_1m[0:1, :].astype(F32))
        return tuple(r_new)

    acc_ref[...] = jnp.zeros_like(acc_ref)
    key_pos = lax.broadcasted_iota(jnp.int32, (t, t), 0)
    qry_pos = lax.broadcasted_iota(jnp.int32, (t, t), 1)
    zero = jnp.zeros((1, t), F32)
    r = block(qi, (zero, zero), key_pos < qry_pos)
    lax.fori_loop(0, qi, lambda jj, r: block(qi - 1 - jj, r, None), r)

    o_ref[...] = jnp.where(first, acc_ref[0], acc_ref[1]).T


def _sb_call(q, k, v, u):
    b, s, _ = q.shape
    t = ATT_BLOCK
    nblk = s // t
    return pl.pallas_call(
        functools.partial(_sb_kernel, nblk=nblk),
        grid=(b, HEAD_PAIRS, nblk),
        in_specs=[
            pl.BlockSpec((None, t, LANES), lambda bi, p, qi: (bi, qi, p)),
            pl.BlockSpec((None, s, LANES), lambda bi, p, qi: (bi, 0, p)),
            pl.BlockSpec((None, s, LANES), lambda bi, p, qi: (bi, 0, p)),
            pl.BlockSpec((t, t), lambda bi, p, qi: (0, 0)),
        ],
        out_specs=pl.BlockSpec((None, t, LANES), lambda bi, p, qi: (bi, qi, p)),
        out_shape=jax.ShapeDtypeStruct((b, s, SB_WIDTH), F32),
        scratch_shapes=[
            pltpu.VMEM((nblk, LANES, t), BF16),
            pltpu.VMEM((2, LANES, t), F32),
        ],
        compiler_params=pltpu.CompilerParams(
            dimension_semantics=("arbitrary", "arbitrary", "arbitrary"),
            vmem_limit_bytes=VMEM_LIMIT_BYTES),
        name="sb_attn",
    )(q, k, v, u)


def _mla_kernel(q_ref, k_ref, v_ref, o_ref, vt_ref, acc_ref, *, nblk):
    t = ATT_BLOCK
    qi = pl.program_id(2)

    @pl.when(qi == 0)
    def _():
        _transpose_v(v_ref, vt_ref, nblk)

    q_heads = tuple(q_ref[:, hd * LANES:(hd + 1) * LANES].astype(F32).T.astype(BF16) for hd in range(2))

    def block(j, carry, allowed):
        k_blk = k_ref[pl.ds(pl.multiple_of(j * t, t), t), :]
        vt_blk = vt_ref[j]
        out = []
        for hd in range(2):
            m, l = carry[hd]
            s = _dot(k_blk[:, hd * LANES:(hd + 1) * LANES], q_heads[hd])
            if allowed is not None:
                s = jnp.where(allowed, s, -jnp.inf)
            m_new = jnp.maximum(m, jnp.max(s, axis=0, keepdims=True))
            alpha = jnp.exp(m - m_new)
            p = jnp.exp(s - m_new)
            l_new = alpha * l + jnp.sum(p, axis=0, keepdims=True)
            acc_ref[hd] = alpha * acc_ref[hd] + _dot(vt_blk, p.astype(BF16))
            out.append((m_new, l_new))
        return tuple(out)

    acc_ref[...] = jnp.zeros_like(acc_ref)
    key_chunk = lax.broadcasted_iota(jnp.int32, (t, t), 0) // CHUNK
    qry_chunk = lax.broadcasted_iota(jnp.int32, (t, t), 1) // CHUNK
    init = (jnp.full((1, t), -jnp.inf, F32), jnp.zeros((1, t), F32))
    carry = block(qi, (init, init), key_chunk <= qry_chunk)
    carry = lax.fori_loop(0, qi, lambda j, c: block(j, c, None), carry)

    first = _head_rows()
    o_t = jnp.where(first, acc_ref[0] / carry[0][1], acc_ref[1] / carry[1][1])
    o_ref[...] = o_t.T


def _mla_call(qf, kf, vm):
    b, s, _ = qf.shape
    t = ATT_BLOCK
    nblk = s // t
    return pl.pallas_call(
        functools.partial(_mla_kernel, nblk=nblk),
        grid=(b, HEAD_PAIRS, nblk),
        in_specs=[
            pl.BlockSpec((None, t, 2 * LANES), lambda bi, p, qi: (bi, qi, p)),
            pl.BlockSpec((None, s, 2 * LANES), lambda bi, p, qi: (bi, 0, p)),
            pl.BlockSpec((None, s, LANES), lambda bi, p, qi: (bi, 0, p)),
        ],
        out_specs=pl.BlockSpec((None, t, LANES), lambda bi, p, qi: (bi, qi, p)),
        out_shape=jax.ShapeDtypeStruct((b, s, MLA_WIDTH), F32),
        scratch_shapes=[
            pltpu.VMEM((nblk, LANES, t), BF16),
            pltpu.VMEM((2, LANES, t), F32),
        ],
        compiler_params=pltpu.CompilerParams(
            dimension_semantics=("arbitrary", "arbitrary", "arbitrary"),
            vmem_limit_bytes=VMEM_LIMIT_BYTES),
        name="mla_attn",
    )(qf, kf, vm)


def _out_kernel(x_ref, osb_ref, omla_ref, gin_ref, wg_ref, bg_ref, wosb_ref, womla_ref, wout_ref,
                gf_ref, o_ref, *, final_norm):
    x = x_ref[...]
    d = x.shape[-1]
    h = _rms(x, gin_ref[...]).astype(BF16)
    a_sb = (osb_ref[...] * jax.nn.silu(_dot(h, wg_ref[:, :SB_WIDTH]))).astype(BF16)
    a_mla = (omla_ref[...] * jax.nn.silu(_dot(h, wg_ref[:, SB_WIDTH:SB_WIDTH + MLA_WIDTH]))).astype(BF16)
    g0 = SB_WIDTH + MLA_WIDTH
    g_sb = jax.nn.sigmoid(_dot(h, wg_ref[:, g0:g0 + d]) + bg_ref[:, :d])
    merged = g_sb * _dot(a_sb, wosb_ref[...])
    g_mla = jax.nn.sigmoid(_dot(h, wg_ref[:, g0 + d:g0 + 2 * d]) + bg_ref[:, d:])
    merged = merged + g_mla * _dot(a_mla, womla_ref[...])
    y = x + _dot(merged.astype(BF16), wout_ref[...])
    o_ref[...] = _rms(y, gf_ref[...]) if final_norm else y


def _out_call(x2, osb, omla, gin, wg, bg, wosb, womla, wout, gf, final_norm):
    n, d = x2.shape
    tm = min(TOKEN_TILE, n)
    row = lambda i: (i, 0)
    fix = lambda i: (0, 0)
    consts = (gin, wg, bg, wosb, womla, wout, gf)
    return pl.pallas_call(
        functools.partial(_out_kernel, final_norm=final_norm),
        grid=(n // tm,),
        in_specs=[pl.BlockSpec((tm, d), row), pl.BlockSpec((tm, SB_WIDTH), row),
                  pl.BlockSpec((tm, MLA_WIDTH), row)] + [pl.BlockSpec(c.shape, fix) for c in consts],
        out_specs=pl.BlockSpec((tm, d), row),
        out_shape=jax.ShapeDtypeStruct((n, d), F32),
        compiler_params=pltpu.CompilerParams(
            dimension_semantics=("arbitrary",), vmem_limit_bytes=VMEM_LIMIT_BYTES),
        name="out",
    )(x2, osb, omla, *consts)


def _rope_tables(s):
    half = MLA_ROPE_DIM // 2
    inv_freq = ROPE_THETA ** (-jnp.arange(half, dtype=F32) / half)
    ang = jnp.arange(s, dtype=F32)[:, None] * inv_freq[None, :]
    cos, sin = jnp.cos(ang), jnp.sin(ang)
    z_nope = jnp.zeros((s, MLA_NOPE_DIM), F32)
    z_pad = jnp.zeros((s, LANES - MLA_QK_DIM), F32)
    scale = MLA_QK_DIM ** -0.5
    cq = jnp.concatenate([jnp.full((s, MLA_NOPE_DIM), scale, F32), cos * scale, cos * scale, z_pad], axis=1)
    sq = jnp.concatenate([z_nope, -sin * scale, sin * scale, z_pad], axis=1)
    ck = jnp.concatenate([z_nope, cos, cos, z_pad], axis=1)
    sk = jnp.concatenate([z_nope, -sin, sin, z_pad], axis=1)
    return cq, sq, ck, sk


def _swap_halves(w):
    half = w.shape[-1] // 2
    return jnp.concatenate([w[..., half:], w[..., :half]], axis=-1)


def _layer_weights(w_in, w_q_up, w_kv_up):
    d = w_in.shape[0]
    offs = np.cumsum([0, SB_WIDTH, SB_WIDTH, SB_WIDTH, SB_WIDTH, MLA_Q_LORA, MLA_KV_LORA, MLA_ROPE_DIM,
                      MLA_WIDTH, 2 * d])
    col = lambda i: w_in[:, offs[i]:offs[i + 1]]
    wqkv = jnp.concatenate([col(0) * (SB_HEAD_DIM ** -0.5), col(1), col(2)], axis=1)
    pad_to_group = lambda w: jnp.concatenate(
        [jnp.zeros((d, MLA_NOPE_DIM), F32), w, jnp.zeros((d, LANES - MLA_QK_DIM), F32)], axis=1)
    wkr1, wkr2 = pad_to_group(col(6)), pad_to_group(_swap_halves(col(6)))
    wg = jnp.concatenate([col(3), col(7), col(8)], axis=1)

    rq = w_q_up.shape[0]
    wq = w_q_up.reshape(rq, MLA_HEADS, MLA_QK_DIM)
    nope, rope = wq[..., :MLA_NOPE_DIM], wq[..., MLA_NOPE_DIM:]
    zq = jnp.zeros((rq, MLA_HEADS, LANES - MLA_QK_DIM), F32)
    wq1 = jnp.concatenate([nope, rope, zq], axis=-1).reshape(rq, MLA_HEADS * LANES)
    wq2 = jnp.concatenate([jnp.zeros_like(nope), _swap_halves(rope), zq], axis=-1).reshape(rq, MLA_HEADS * LANES)

    rkv = w_kv_up.shape[0]
    wkv = w_kv_up.reshape(rkv, MLA_HEADS, MLA_NOPE_DIM + MLA_V_DIM)
    wkn = jnp.concatenate([wkv[..., :MLA_NOPE_DIM], jnp.zeros((rkv, MLA_HEADS, LANES - MLA_NOPE_DIM), F32)],
                          axis=-1).reshape(rkv, MLA_HEADS * LANES)
    wv = wkv[..., MLA_NOPE_DIM:].reshape(rkv, MLA_WIDTH)
    bf = lambda w: w.astype(BF16)
    return dict(wqkv=bf(wqkv), wcq=bf(col(4)), wckv=bf(col(5)), wkr1=bf(wkr1), wkr2=bf(wkr2),
                wq1=bf(wq1), wq2=bf(wq2), wkn=bf(wkn), wv=bf(wv), wg=bf(wg))


def kernel(x, norm_in_g, w_in, b_gate, q_norm_g, w_q_up, kv_norm_g, w_kv_up, w_o_sb, w_o_mla, w_out, norm_f_g):
    b, s, d = x.shape
    depth = w_in.shape[0]
    assert s % ATT_BLOCK == 0 and s % min(TOKEN_TILE, s) == 0
    tabs = _rope_tables(s)
    idx = jnp.arange(ATT_BLOCK)
    later_mat = (idx[None, :] > idx[:, None]).astype(BF16)

    x2 = x.reshape(b * s, d)
    for l in range(depth):
        w = _layer_weights(w_in[l], w_q_up[l], w_kv_up[l])
        gin = norm_in_g[l][None, :]
        qsb, ksb, vsb, qf, kf, vm = _proj_call(
            x2, gin,
            (w["wqkv"], w["wcq"], w["wckv"], w["wkr1"], w["wkr2"], q_norm_g[l][None, :], kv_norm_g[l][None, :],
             w["wq1"], w["wq2"], w["wkn"], w["wv"]),
            tabs)
        r3 = lambda a: a.reshape(b, s, a.shape[-1])
        o_sb = _sb_call(r3(qsb), r3(ksb), r3(vsb), later_mat).reshape(b * s, SB_WIDTH)
        o_mla = _mla_call(r3(qf), r3(kf), r3(vm)).reshape(b * s, MLA_WIDTH)
        x2 = _out_call(x2, o_sb, o_mla, gin, w["wg"], b_gate[l][None, :], w_o_sb[l].astype(BF16),
                       w_o_mla[l].astype(BF16), w_out[l].astype(BF16), norm_f_g[None, :],
                       final_norm=(l == depth - 1))
    return x2.reshape(b, s, d)
```

```python
import functools

import jax
import jax.numpy as jnp
import numpy as np
from jax import lax
from jax.experimental import pallas as pl
from jax.experimental.pallas import tpu as pltpu

EPS = 1e-6
CHUNK = 64

SB_HEADS = 8
SB_HEAD_DIM = 64
SB_WIDTH = SB_HEADS * SB_HEAD_DIM

MLA_HEADS = 8
MLA_NOPE_DIM = 64
MLA_ROPE_DIM = 32
MLA_V_DIM = 64
MLA_Q_LORA = 384
MLA_KV_LORA = 256
MLA_QK_DIM = MLA_NOPE_DIM + MLA_ROPE_DIM
MLA_WIDTH = MLA_HEADS * MLA_V_DIM
ROPE_THETA = 10000.0

LANES = 128
ATT_BLOCK = 512
SB_BLOCK = 256
SB_DEAD_LOG = 110.0
HEADS_PER_STEP = 4
V_ROWS = 80
TOKEN_TILE = 512
VMEM_LIMIT_BYTES = 56 * 1024 * 1024

F32 = jnp.float32
BF16 = jnp.bfloat16


def _dot(a, b):
    return jnp.dot(a, b, preferred_element_type=F32)


def _rms(x, g):
    inv = lax.rsqrt(jnp.mean(x * x, axis=-1, keepdims=True) + EPS)
    return (x * inv) * g


def _proj_kernel(x_ref, gin_ref, wqkv_ref, wcq_ref, wckv_ref, wkr1_ref, wkr2_ref,
                 qng_ref, kvng_ref, wq1_ref, wq2_ref, wkn_ref, wv_ref,
                 cq_ref, sq_ref, ck_ref, sk_ref,
                 qsb_ref, ksb_ref, vsb_ref, qf_ref, kf_ref, vm_ref):
    h = _rms(x_ref[...], gin_ref[...]).astype(BF16)

    for c, o_ref in enumerate((qsb_ref, ksb_ref, vsb_ref)):
        o_ref[...] = _dot(h, wqkv_ref[:, c * SB_WIDTH:(c + 1) * SB_WIDTH]).astype(BF16)

    cq = _rms(_dot(h, wcq_ref[...]), qng_ref[...]).astype(BF16)
    ckv = _rms(_dot(h, wckv_ref[...]), kvng_ref[...]).astype(BF16)

    k_rope = _dot(h, wkr1_ref[...]) * ck_ref[...] + _dot(h, wkr2_ref[...]) * sk_ref[...]
    cq_t, sq_t = cq_ref[...], sq_ref[...]
    for hd in range(MLA_HEADS):
        sl = slice(hd * LANES, (hd + 1) * LANES)
        q1 = _dot(cq, wq1_ref[:, sl])
        q2 = _dot(cq, wq2_ref[:, sl])
        qf_ref[:, sl] = (q1 * cq_t + q2 * sq_t).astype(BF16)
        kf_ref[:, sl] = (_dot(ckv, wkn_ref[:, sl]) + k_rope).astype(BF16)
    vm_ref[...] = _dot(ckv, wv_ref[...]).astype(BF16)


def _proj_call(x2, gin, wts, tabs):
    n, d = x2.shape
    s = tabs[0].shape[0]
    tm = min(TOKEN_TILE, s)
    row = lambda i: (i, 0)
    fix = lambda i: (0, 0)
    w_specs = [pl.BlockSpec(w.shape, fix) for w in wts]
    t_specs = [pl.BlockSpec((tm, LANES), lambda i: (i % (s // tm), 0)) for _ in tabs]
    outs = [(SB_WIDTH, BF16)] * 3 + [(MLA_HEADS * LANES, BF16)] * 2 + [(MLA_WIDTH, BF16)]
    return pl.pallas_call(
        _proj_kernel,
        grid=(n // tm,),
        in_specs=[pl.BlockSpec((tm, d), row), pl.BlockSpec(gin.shape, fix)] + w_specs + t_specs,
        out_specs=[pl.BlockSpec((tm, w), row) for w, _ in outs],
        out_shape=[jax.ShapeDtypeStruct((n, w), dt) for w, dt in outs],
        compiler_params=pltpu.CompilerParams(
            dimension_semantics=("arbitrary",), vmem_limit_bytes=VMEM_LIMIT_BYTES),
        name="proj",
    )(x2, gin, *wts, *tabs)


def _transpose_v(v_ref, vt_ref, nblk, blk):
    def body(j, carry):
        rows = v_ref[pl.ds(pl.multiple_of(j * blk, blk), blk), :]
        vt_ref[j] = rows.astype(F32).T.astype(BF16)
        return carry

    lax.fori_loop(0, nblk, body, 0)


def _sb_kernel(q_ref, k_ref, v_ref, u_ref, o_ref, vt_ref, acc_ref, *, nblk):
    t = SB_BLOCK
    qi = pl.program_id(1)

    @pl.when(qi == 0)
    def _():
        _transpose_v(v_ref, vt_ref, nblk, t)

    first = lax.broadcasted_iota(jnp.int32, (LANES, t), 0) < SB_HEAD_DIM
    q_heads = []
    for p in range(SB_HEADS // 2):
        q_t = q_ref[:, p * LANES:(p + 1) * LANES].astype(F32).T
        q_heads += [jnp.where(first, q_t, 0.0).astype(BF16), jnp.where(first, 0.0, q_t).astype(BF16)]

    def block(j, r, strict):
        k_blk = k_ref[pl.ds(pl.multiple_of(j * t, t), t), :]
        heads = range(SB_HEADS)
        z = [_dot(k_blk[:, (hd // 2) * LANES:(hd // 2 + 1) * LANES], q_heads[hd]) for hd in heads]
        incl = []
        for hd in heads:
            neg_log_1m = jnp.maximum(z[hd], 0.0) + jnp.log(1.0 + jnp.exp(-jnp.abs(z[hd])))
            if strict is not None:
                neg_log_1m = jnp.where(strict, neg_log_1m, 0.0)
            incl.append(_dot(u_ref[...], neg_log_1m.astype(BF16)))
        for hd in heads:
            w = jnp.exp(z[hd] - incl[hd] + r[hd])
            if strict is not None:
                w = jnp.where(strict, w, 0.0)
            vt_blk = vt_ref[j, hd * SB_HEAD_DIM:(hd + 1) * SB_HEAD_DIM, :]
            acc_ref[hd] += _dot(vt_blk, w.astype(BF16))
        return tuple(r[hd] - incl[hd][0:1, :] for hd in heads)

    def r_max(r):
        return functools.reduce(jnp.maximum, [jnp.max(x) for x in r])

    acc_ref[...] = jnp.zeros_like(acc_ref)
    zero = jnp.zeros((1, t), F32)
    key_pos = lax.broadcasted_iota(jnp.int32, (t, t), 0)
    qry_pos = lax.broadcasted_iota(jnp.int32, (t, t), 1)
    r = block(qi, (zero,) * SB_HEADS, key_pos < qry_pos)

    def cond(c):
        j, rmax, _ = c
        return jnp.logical_and(j >= 0, rmax > -SB_DEAD_LOG)

    def body(c):
        j, _, r = c
        r = block(j, r, None)
        return j - 1, r_max(r), r

    lax.while_loop(cond, body, (qi - 1, r_max(r), r))

    o_ref[...] = jnp.concatenate([acc_ref[hd] for hd in range(SB_HEADS)], axis=0).T


def _sb_call(q, k, v, u):
    b, s, w = q.shape
    t = SB_BLOCK
    nblk = s // t
    resident = lambda: pl.BlockSpec((None, s, w), lambda bi, qi: (bi, 0, 0), pipeline_mode=pl.Buffered(1))
    return pl.pallas_call(
        functools.partial(_sb_kernel, nblk=nblk),
        grid=(b, nblk),
        in_specs=[
            pl.BlockSpec((None, t, w), lambda bi, qi: (bi, qi, 0)),
            resident(),
            resident(),
            pl.BlockSpec(u.shape, lambda bi, qi: (0, 0)),
        ],
        out_specs=pl.BlockSpec((None, t, w), lambda bi, qi: (bi, qi, 0)),
        out_shape=jax.ShapeDtypeStruct((b, s, w), F32),
        scratch_shapes=[
            pltpu.VMEM((nblk, w, t), BF16),
            pltpu.VMEM((SB_HEADS, SB_HEAD_DIM, t), F32),
        ],
        compiler_params=pltpu.CompilerParams(
            dimension_semantics=("arbitrary", "arbitrary"), vmem_limit_bytes=VMEM_LIMIT_BYTES),
        name="sb_attn",
    )(q, k, v, u)


def _mla_kernel(q_ref, k_ref, v_ref, o_ref, vt_ref, acc_ref, m_ref, s0_ref, s1_ref, smax0_ref, smax1_ref, *,
                nblk):
    t = ATT_BLOCK
    qi = pl.program_id(2)

    @pl.when(qi == 0)
    def _():
        ones_rows = (lax.broadcasted_iota(jnp.int32, (V_ROWS - MLA_V_DIM, t), 0) == 0).astype(BF16)

        def body(j, carry):
            rows = v_ref[pl.ds(pl.multiple_of(j * t, t), t), :].astype(F32).T.astype(BF16)
            for hd in range(HEADS_PER_STEP):
                vt_ref[j, hd, :MLA_V_DIM, :] = rows[hd * MLA_V_DIM:(hd + 1) * MLA_V_DIM, :]
                vt_ref[j, hd, MLA_V_DIM:, :] = ones_rows
            return carry

        lax.fori_loop(0, nblk, body, 0)

    q_heads = [q_ref[:, hd * LANES:(hd + 1) * LANES].astype(F32).T.astype(BF16) for hd in range(HEADS_PER_STEP)]

    def scores(hd, j, buf, allowed):
        s_ref, smax_ref = buf
        k_blk = k_ref[pl.ds(pl.multiple_of(j * t, t), t), hd * LANES:(hd + 1) * LANES]
        s = _dot(k_blk, q_heads[hd])
        if allowed is not None:
            s = jnp.where(allowed, s, -jnp.inf)
        s_ref[hd] = s
        smax_ref[hd] = jnp.max(s, axis=0, keepdims=True)

    def update(hd, j, buf):
        s_ref, smax_ref = buf
        m = m_ref[hd]
        m_new = jnp.maximum(m, smax_ref[hd])
        m_ref[hd] = m_new
        p = jnp.exp2(s_ref[hd] - m_new).astype(BF16)
        acc_ref[hd] = jnp.exp2(m - m_new) * acc_ref[hd] + _dot(vt_ref[j, hd], p)

    def half_step(j_next, buf_next, j_cur, buf_cur):
        for hd in range(HEADS_PER_STEP):
            scores(hd, j_next, buf_next, None)
            update(hd, j_cur, buf_cur)

    acc_ref[...] = jnp.zeros_like(acc_ref)
    m_ref[...] = jnp.full_like(m_ref, -jnp.inf)
    key_chunk = lax.broadcasted_iota(jnp.int32, (t, t), 0) // CHUNK
    qry_chunk = lax.broadcasted_iota(jnp.int32, (t, t), 1) // CHUNK
    buf0, buf1 = (s0_ref, smax0_ref), (s1_ref, smax1_ref)

    n_visits = qi + 1
    block_of = lambda k: jnp.where(k == 0, qi, jnp.minimum(k, n_visits - 1) - 1)
    allowed = key_chunk <= qry_chunk
    for hd in range(HEADS_PER_STEP):
        scores(hd, qi, buf0, allowed)

    def body(pp, carry):
        k = 2 * pp
        half_step(block_of(k + 1), buf1, block_of(k), buf0)
        half_step(block_of(k + 2), buf0, block_of(k + 1), buf1)
        return carry

    lax.fori_loop(0, n_visits // 2, body, 0)

    @pl.when(n_visits % 2 == 1)
    def _():
        for hd in range(HEADS_PER_STEP):
            update(hd, block_of(n_visits - 1), buf0)

    o_ref[...] = jnp.concatenate(
        [acc_ref[hd, :MLA_V_DIM, :] / acc_ref[hd, MLA_V_DIM:MLA_V_DIM + 1, :] for hd in range(HEADS_PER_STEP)],
        axis=0).T


def _mla_call(qf, kf, vm):
    b, s, _ = qf.shape
    t = ATT_BLOCK
    nblk = s // t
    groups = MLA_HEADS // HEADS_PER_STEP
    return pl.pallas_call(
        functools.partial(_mla_kernel, nblk=nblk),
        grid=(b, groups, nblk),
        in_specs=[
            pl.BlockSpec((None, t, HEADS_PER_STEP * LANES), lambda bi, g, qi: (bi, qi, g)),
            pl.BlockSpec((None, s, HEADS_PER_STEP * LANES), lambda bi, g, qi: (bi, 0, g)),
            pl.BlockSpec((None, s, HEADS_PER_STEP * MLA_V_DIM), lambda bi, g, qi: (bi, 0, g)),
        ],
        out_specs=pl.BlockSpec((None, t, HEADS_PER_STEP * MLA_V_DIM), lambda bi, g, qi: (bi, qi, g)),
        out_shape=jax.ShapeDtypeStruct((b, s, MLA_WIDTH), F32),
        scratch_shapes=[
            pltpu.VMEM((nblk, HEADS_PER_STEP, V_ROWS, t), BF16),
            pltpu.VMEM((HEADS_PER_STEP, V_ROWS, t), F32),
            pltpu.VMEM((HEADS_PER_STEP, 1, t), F32),
            pltpu.VMEM((HEADS_PER_STEP, t, t), F32),
            pltpu.VMEM((HEADS_PER_STEP, t, t), F32),
            pltpu.VMEM((HEADS_PER_STEP, 1, t), F32),
            pltpu.VMEM((HEADS_PER_STEP, 1, t), F32),
        ],
        compiler_params=pltpu.CompilerParams(
            dimension_semantics=("arbitrary", "arbitrary", "arbitrary"), vmem_limit_bytes=VMEM_LIMIT_BYTES),
        name="mla_attn",
    )(qf, kf, vm)


def _out_kernel(x_ref, osb_ref, omla_ref, gin_ref, wg_ref, bg_ref, wosb_ref, womla_ref, wout_ref,
                gf_ref, o_ref, *, final_norm):
    x = x_ref[...]
    d = x.shape[-1]
    h = _rms(x, gin_ref[...]).astype(BF16)
    a_sb = (osb_ref[...] * jax.nn.silu(_dot(h, wg_ref[:, :SB_WIDTH]))).astype(BF16)
    a_mla = (omla_ref[...] * jax.nn.silu(_dot(h, wg_ref[:, SB_WIDTH:SB_WIDTH + MLA_WIDTH]))).astype(BF16)
    g0 = SB_WIDTH + MLA_WIDTH
    g_sb = jax.nn.sigmoid(_dot(h, wg_ref[:, g0:g0 + d]) + bg_ref[:, :d])
    merged = g_sb * _dot(a_sb, wosb_ref[...])
    g_mla = jax.nn.sigmoid(_dot(h, wg_ref[:, g0 + d:g0 + 2 * d]) + bg_ref[:, d:])
    merged = merged + g_mla * _dot(a_mla, womla_ref[...])
    y = x + _dot(merged.astype(BF16), wout_ref[...])
    o_ref[...] = _rms(y, gf_ref[...]) if final_norm else y


def _out_call(x2, osb, omla, gin, wg, bg, wosb, womla, wout, gf, final_norm):
    n, d = x2.shape
    tm = min(TOKEN_TILE, n)
    row = lambda i: (i, 0)
    fix = lambda i: (0, 0)
    consts = (gin, wg, bg, wosb, womla, wout, gf)
    return pl.pallas_call(
        functools.partial(_out_kernel, final_norm=final_norm),
        grid=(n // tm,),
        in_specs=[pl.BlockSpec((tm, d), row), pl.BlockSpec((tm, SB_WIDTH), row),
                  pl.BlockSpec((tm, MLA_WIDTH), row)] + [pl.BlockSpec(c.shape, fix) for c in consts],
        out_specs=pl.BlockSpec((tm, d), row),
        out_shape=jax.ShapeDtypeStruct((n, d), F32),
        compiler_params=pltpu.CompilerParams(
            dimension_semantics=("arbitrary",), vmem_limit_bytes=VMEM_LIMIT_BYTES),
        name="out",
    )(x2, osb, omla, *consts)


def _rope_tables(s):
    half = MLA_ROPE_DIM // 2
    inv_freq = ROPE_THETA ** (-jnp.arange(half, dtype=F32) / half)
    ang = jnp.arange(s, dtype=F32)[:, None] * inv_freq[None, :]
    cos, sin = jnp.cos(ang), jnp.sin(ang)
    z_nope = jnp.zeros((s, MLA_NOPE_DIM), F32)
    z_pad = jnp.zeros((s, LANES - MLA_QK_DIM), F32)
    scale = MLA_QK_DIM ** -0.5 * np.log2(np.e)
    cq = jnp.concatenate([jnp.full((s, MLA_NOPE_DIM), scale, F32), cos * scale, cos * scale, z_pad], axis=1)
    sq = jnp.concatenate([z_nope, -sin * scale, sin * scale, z_pad], axis=1)
    ck = jnp.concatenate([z_nope, cos, cos, z_pad], axis=1)
    sk = jnp.concatenate([z_nope, -sin, sin, z_pad], axis=1)
    return cq, sq, ck, sk


def _swap_halves(w):
    half = w.shape[-1] // 2
    return jnp.concatenate([w[..., half:], w[..., :half]], axis=-1)


def _layer_weights(w_in, w_q_up, w_kv_up):
    d = w_in.shape[0]
    offs = np.cumsum([0, SB_WIDTH, SB_WIDTH, SB_WIDTH, SB_WIDTH, MLA_Q_LORA, MLA_KV_LORA, MLA_ROPE_DIM,
                      MLA_WIDTH, 2 * d])
    col = lambda i: w_in[:, offs[i]:offs[i + 1]]
    wqkv = jnp.concatenate([col(0) * (SB_HEAD_DIM ** -0.5), col(1), col(2)], axis=1)
    pad_to_group = lambda w: jnp.concatenate(
        [jnp.zeros((d, MLA_NOPE_DIM), F32), w, jnp.zeros((d, LANES - MLA_QK_DIM), F32)], axis=1)
    wkr1, wkr2 = pad_to_group(col(6)), pad_to_group(_swap_halves(col(6)))
    wg = jnp.concatenate([col(3), col(7), col(8)], axis=1)

    rq = w_q_up.shape[0]
    wq = w_q_up.reshape(rq, MLA_HEADS, MLA_QK_DIM)
    nope, rope = wq[..., :MLA_NOPE_DIM], wq[..., MLA_NOPE_DIM:]
    zq = jnp.zeros((rq, MLA_HEADS, LANES - MLA_QK_DIM), F32)
    wq1 = jnp.concatenate([nope, rope, zq], axis=-1).reshape(rq, MLA_HEADS * LANES)
    wq2 = jnp.concatenate([jnp.zeros_like(nope), _swap_halves(rope), zq], axis=-1).reshape(rq, MLA_HEADS * LANES)

    rkv = w_kv_up.shape[0]
    wkv = w_kv_up.reshape(rkv, MLA_HEADS, MLA_NOPE_DIM + MLA_V_DIM)
    wkn = jnp.concatenate([wkv[..., :MLA_NOPE_DIM], jnp.zeros((rkv, MLA_HEADS, LANES - MLA_NOPE_DIM), F32)],
                          axis=-1).reshape(rkv, MLA_HEADS * LANES)
    wv = wkv[..., MLA_NOPE_DIM:].reshape(rkv, MLA_WIDTH)
    bf = lambda w: w.astype(BF16)
    return dict(wqkv=bf(wqkv), wcq=bf(col(4)), wckv=bf(col(5)), wkr1=bf(wkr1), wkr2=bf(wkr2),
                wq1=bf(wq1), wq2=bf(wq2), wkn=bf(wkn), wv=bf(wv), wg=bf(wg))


def kernel(x, norm_in_g, w_in, b_gate, q_norm_g, w_q_up, kv_norm_g, w_kv_up, w_o_sb, w_o_mla, w_out, norm_f_g):
    b, s, d = x.shape
    depth = w_in.shape[0]
    assert s % ATT_BLOCK == 0 and s % SB_BLOCK == 0 and s % min(TOKEN_TILE, s) == 0
    tabs = _rope_tables(s)
    idx = jnp.arange(SB_BLOCK)
    from_mat = (idx[None, :] >= idx[:, None]).astype(BF16)

    x2 = x.reshape(b * s, d)
    for l in range(depth):
        w = _layer_weights(w_in[l], w_q_up[l], w_kv_up[l])
        gin = norm_in_g[l][None, :]
        qsb, ksb, vsb, qf, kf, vm = _proj_call(
            x2, gin,
            (w["wqkv"], w["wcq"], w["wckv"], w["wkr1"], w["wkr2"], q_norm_g[l][None, :], kv_norm_g[l][None, :],
             w["wq1"], w["wq2"], w["wkn"], w["wv"]),
            tabs)
        r3 = lambda a: a.reshape(b, s, a.shape[-1])
        o_sb = _sb_call(r3(qsb), r3(ksb), r3(vsb), from_mat).reshape(b * s, SB_WIDTH)
        o_mla = _mla_call(r3(qf), r3(kf), r3(vm)).reshape(b * s, MLA_WIDTH)
        x2 = _out_call(x2, o_sb, o_mla, gin, w["wg"], b_gate[l][None, :], w_o_sb[l].astype(BF16),
                       w_o_mla[l].astype(BF16), w_out[l].astype(BF16), norm_f_g[None, :],
                       final_norm=(l == depth - 1))
    return x2.reshape(b, s, d)
```

```python
import functools

import jax
import jax.numpy as jnp
import numpy as np
from jax import lax
from jax.experimental import pallas as pl
from jax.experimental.pallas import tpu as pltpu

EPS = 1e-6
CHUNK = 64

SB_HEADS = 8
SB_HEAD_DIM = 64
SB_WIDTH = SB_HEADS * SB_HEAD_DIM

MLA_HEADS = 8
MLA_NOPE_DIM = 64
MLA_ROPE_DIM = 32
MLA_V_DIM = 64
MLA_Q_LORA = 384
MLA_KV_LORA = 256
MLA_QK_DIM = MLA_NOPE_DIM + MLA_ROPE_DIM
MLA_WIDTH = MLA_HEADS * MLA_V_DIM
ROPE_THETA = 10000.0

LANES = 128
ATT_BLOCK = 512
SB_BLOCK = 256
SB_DEAD_LOG = 110.0
HEADS_PER_STEP = 4
V_ROWS = 80
TOKEN_TILE = 512
VMEM_LIMIT_BYTES = 56 * 1024 * 1024

F32 = jnp.float32
BF16 = jnp.bfloat16


def _dot(a, b):
    return jnp.dot(a, b, preferred_element_type=F32)


def _rms(x, g):
    inv = lax.rsqrt(jnp.mean(x * x, axis=-1, keepdims=True) + EPS)
    return (x * inv) * g


def _proj_kernel(x_ref, gin_ref, wqkv_ref, wcq_ref, wckv_ref, wkr1_ref, wkr2_ref,
                 qng_ref, kvng_ref, wq1_ref, wq2_ref, wkn_ref, wv_ref,
                 cq_ref, sq_ref, ck_ref, sk_ref,
                 qsb_ref, ksb_ref, vsb_ref, qf_ref, kf_ref, vm_ref):
    h = _rms(x_ref[...], gin_ref[...]).astype(BF16)

    for c, o_ref in enumerate((qsb_ref, ksb_ref, vsb_ref)):
        o_ref[...] = _dot(h, wqkv_ref[:, c * SB_WIDTH:(c + 1) * SB_WIDTH]).astype(BF16)

    cq = _rms(_dot(h, wcq_ref[...]), qng_ref[...]).astype(BF16)
    ckv = _rms(_dot(h, wckv_ref[...]), kvng_ref[...]).astype(BF16)

    k_rope = _dot(h, wkr1_ref[...]) * ck_ref[...] + _dot(h, wkr2_ref[...]) * sk_ref[...]
    cq_t, sq_t = cq_ref[...], sq_ref[...]
    for hd in range(MLA_HEADS):
        sl = slice(hd * LANES, (hd + 1) * LANES)
        q1 = _dot(cq, wq1_ref[:, sl])
        q2 = _dot(cq, wq2_ref[:, sl])
        qf_ref[:, sl] = (q1 * cq_t + q2 * sq_t).astype(BF16)
        kf_ref[:, sl] = (_dot(ckv, wkn_ref[:, sl]) + k_rope).astype(BF16)
    vm_ref[...] = _dot(ckv, wv_ref[...]).astype(BF16)


def _proj_call(x2, gin, wts, tabs):
    n, d = x2.shape
    s = tabs[0].shape[0]
    tm = min(TOKEN_TILE, s)
    row = lambda i: (i, 0)
    fix = lambda i: (0, 0)
    w_specs = [pl.BlockSpec(w.shape, fix) for w in wts]
    t_specs = [pl.BlockSpec((tm, LANES), lambda i: (i % (s // tm), 0)) for _ in tabs]
    outs = [(SB_WIDTH, BF16)] * 3 + [(MLA_HEADS * LANES, BF16)] * 2 + [(MLA_WIDTH, BF16)]
    return pl.pallas_call(
        _proj_kernel,
        grid=(n // tm,),
        in_specs=[pl.BlockSpec((tm, d), row), pl.BlockSpec(gin.shape, fix)] + w_specs + t_specs,
        out_specs=[pl.BlockSpec((tm, w), row) for w, _ in outs],
        out_shape=[jax.ShapeDtypeStruct((n, w), dt) for w, dt in outs],
        compiler_params=pltpu.CompilerParams(
            dimension_semantics=("arbitrary",), vmem_limit_bytes=VMEM_LIMIT_BYTES),
        name="proj",
    )(x2, gin, *wts, *tabs)


def _transpose_v(v_ref, vt_ref, nblk, blk):
    def body(j, carry):
        rows = v_ref[pl.ds(pl.multiple_of(j * blk, blk), blk), :]
        vt_ref[j] = rows.astype(F32).T.astype(BF16)
        return carry

    lax.fori_loop(0, nblk, body, 0)


def _sb_kernel(q_ref, k_ref, v_ref, u_ref, o_ref, vt_ref, acc_ref, *, nblk):
    t = SB_BLOCK
    qi = pl.program_id(1)

    @pl.when(qi == 0)
    def _():
        _transpose_v(v_ref, vt_ref, nblk, t)

    first = lax.broadcasted_iota(jnp.int32, (LANES, t), 0) < SB_HEAD_DIM
    q_heads = []
    for p in range(SB_HEADS // 2):
        q_t = q_ref[:, p * LANES:(p + 1) * LANES].astype(F32).T
        q_heads += [jnp.where(first, q_t, 0.0).astype(BF16), jnp.where(first, 0.0, q_t).astype(BF16)]

    def block(j, r, strict):
        k_blk = k_ref[pl.ds(pl.multiple_of(j * t, t), t), :]
        heads = range(SB_HEADS)
        z = [_dot(k_blk[:, (hd // 2) * LANES:(hd // 2 + 1) * LANES], q_heads[hd]) for hd in heads]
        incl = []
        for hd in heads:
            neg_log_1m = jnp.maximum(z[hd], 0.0) + jnp.log(1.0 + jnp.exp(-jnp.abs(z[hd])))
            if strict is not None:
                neg_log_1m = jnp.where(strict, neg_log_1m, 0.0)
            incl.append(_dot(u_ref[...], neg_log_1m.astype(BF16)))
        for hd in heads:
            w = jnp.exp(z[hd] - incl[hd] + r[hd])
            if strict is not None:
                w = jnp.where(strict, w, 0.0)
            vt_blk = vt_ref[j, hd * SB_HEAD_DIM:(hd + 1) * SB_HEAD_DIM, :]
            acc_ref[hd] += _dot(vt_blk, w.astype(BF16))
        return tuple(r[hd] - incl[hd][0:1, :] for hd in heads)

    def r_max(r):
        return functools.reduce(jnp.maximum, [jnp.max(x) for x in r])

    acc_ref[...] = jnp.zeros_like(acc_ref)
    zero = jnp.zeros((1, t), F32)
    key_pos = lax.broadcasted_iota(jnp.int32, (t, t), 0)
    qry_pos = lax.broadcasted_iota(jnp.int32, (t, t), 1)
    r = block(qi, (zero,) * SB_HEADS, key_pos < qry_pos)

    def cond(c):
        j, rmax, _ = c
        return jnp.logical_and(j >= 0, rmax > -SB_DEAD_LOG)

    def body(c):
        j, _, r = c
        r = block(j, r, None)
        return j - 1, r_max(r), r

    lax.while_loop(cond, body, (qi - 1, r_max(r), r))

    o_ref[...] = jnp.concatenate([acc_ref[hd] for hd in range(SB_HEADS)], axis=0).T


def _sb_call(q, k, v, u):
    b, s, w = q.shape
    t = SB_BLOCK
    nblk = s // t
    resident = lambda: pl.BlockSpec((None, s, w), lambda bi, qi: (bi, 0, 0), pipeline_mode=pl.Buffered(1))
    return pl.pallas_call(
        functools.partial(_sb_kernel, nblk=nblk),
        grid=(b, nblk),
        in_specs=[
            pl.BlockSpec((None, t, w), lambda bi, qi: (bi, qi, 0)),
            resident(),
            resident(),
            pl.BlockSpec(u.shape, lambda bi, qi: (0, 0)),
        ],
        out_specs=pl.BlockSpec((None, t, w), lambda bi, qi: (bi, qi, 0)),
        out_shape=jax.ShapeDtypeStruct((b, s, w), F32),
        scratch_shapes=[
            pltpu.VMEM((nblk, w, t), BF16),
            pltpu.VMEM((SB_HEADS, SB_HEAD_DIM, t), F32),
        ],
        compiler_params=pltpu.CompilerParams(
            dimension_semantics=("arbitrary", "arbitrary"), vmem_limit_bytes=VMEM_LIMIT_BYTES),
        name="sb_attn",
    )(q, k, v, u)


def _mla_kernel(q_ref, k_ref, v_ref, o_ref, vt_ref, acc_ref, m_ref, s0_ref, s1_ref, smax0_ref, smax1_ref, *,
                nblk):
    t = ATT_BLOCK
    qi = pl.program_id(2)

    @pl.when(qi == 0)
    def _():
        ones_rows = (lax.broadcasted_iota(jnp.int32, (V_ROWS - MLA_V_DIM, t), 0) == 0).astype(BF16)

        def body(j, carry):
            rows = v_ref[pl.ds(pl.multiple_of(j * t, t), t), :].astype(F32).T.astype(BF16)
            for hd in range(HEADS_PER_STEP):
                vt_ref[j, hd, :MLA_V_DIM, :] = rows[hd * MLA_V_DIM:(hd + 1) * MLA_V_DIM, :]
                vt_ref[j, hd, MLA_V_DIM:, :] = ones_rows
            return carry

        lax.fori_loop(0, nblk, body, 0)

    q_heads = [q_ref[:, hd * LANES:(hd + 1) * LANES].astype(F32).T.astype(BF16) for hd in range(HEADS_PER_STEP)]

    def scores(hd, j, buf, allowed):
        s_ref, smax_ref = buf
        k_blk = k_ref[pl.ds(pl.multiple_of(j * t, t), t), hd * LANES:(hd + 1) * LANES]
        s = _dot(k_blk, q_heads[hd])
        if allowed is not None:
            s = jnp.where(allowed, s, -jnp.inf)
        s_ref[hd] = s
        smax_ref[hd] = jnp.max(s, axis=0, keepdims=True)

    def update(hd, j, buf):
        s_ref, smax_ref = buf
        m = m_ref[hd]
        m_new = jnp.maximum(m, smax_ref[hd])
        m_ref[hd] = m_new
        p = jnp.exp2(s_ref[hd] - m_new).astype(BF16)
        acc_ref[hd] = jnp.exp2(m - m_new) * acc_ref[hd] + _dot(vt_ref[j, hd], p)

    acc_ref[...] = jnp.zeros_like(acc_ref)
    m_ref[...] = jnp.full_like(m_ref, -jnp.inf)
    key_chunk = lax.broadcasted_iota(jnp.int32, (t, t), 0) // CHUNK
    qry_chunk = lax.broadcasted_iota(jnp.int32, (t, t), 1) // CHUNK
    bufs = ((s0_ref, smax0_ref), (s1_ref, smax1_ref))

    block_of = lambda k: jnp.where(k == 0, qi, k - 1)
    allowed = key_chunk <= qry_chunk
    for hd in range(HEADS_PER_STEP):
        scores(hd, qi, bufs[0], allowed)

    def body(k, carry):
        for parity in range(2):
            @pl.when(k % 2 == parity)
            def _():
                for hd in range(HEADS_PER_STEP):
                    scores(hd, block_of(k + 1), bufs[1 - parity], None)
                    update(hd, block_of(k), bufs[parity])
        return carry

    lax.fori_loop(0, qi, body, 0)

    for parity in range(2):
        @pl.when(qi % 2 == parity)
        def _():
            for hd in range(HEADS_PER_STEP):
                update(hd, block_of(qi), bufs[parity])

    o_ref[...] = jnp.concatenate(
        [acc_ref[hd, :MLA_V_DIM, :] / acc_ref[hd, MLA_V_DIM:MLA_V_DIM + 1, :] for hd in range(HEADS_PER_STEP)],
        axis=0).T


def _mla_call(qf, kf, vm):
    b, s, _ = qf.shape
    t = ATT_BLOCK
    nblk = s // t
    groups = MLA_HEADS // HEADS_PER_STEP
    return pl.pallas_call(
        functools.partial(_mla_kernel, nblk=nblk),
        grid=(b, groups, nblk),
        in_specs=[
            pl.BlockSpec((None, t, HEADS_PER_STEP * LANES), lambda bi, g, qi: (bi, qi, g)),
            pl.BlockSpec((None, s, HEADS_PER_STEP * LANES), lambda bi, g, qi: (bi, 0, g)),
            pl.BlockSpec((None, s, HEADS_PER_STEP * MLA_V_DIM), lambda bi, g, qi: (bi, 0, g)),
        ],
        out_specs=pl.BlockSpec((None, t, HEADS_PER_STEP * MLA_V_DIM), lambda bi, g, qi: (bi, qi, g)),
        out_shape=jax.ShapeDtypeStruct((b, s, MLA_WIDTH), F32),
        scratch_shapes=[
            pltpu.VMEM((nblk, HEADS_PER_STEP, V_ROWS, t), BF16),
            pltpu.VMEM((HEADS_PER_STEP, V_ROWS, t), F32),
            pltpu.VMEM((HEADS_PER_STEP, 1, t), F32),
            pltpu.VMEM((HEADS_PER_STEP, t, t), F32),
            pltpu.VMEM((HEADS_PER_STEP, t, t), F32),
            pltpu.VMEM((HEADS_PER_STEP, 1, t), F32),
            pltpu.VMEM((HEADS_PER_STEP, 1, t), F32),
        ],
        compiler_params=pltpu.CompilerParams(
            dimension_semantics=("arbitrary", "arbitrary", "arbitrary"), vmem_limit_bytes=VMEM_LIMIT_BYTES),
        name="mla_attn",
    )(qf, kf, vm)


def _out_kernel(x_ref, osb_ref, omla_ref, gin_ref, wg_ref, bg_ref, wosb_ref, womla_ref, wout_ref,
                gf_ref, o_ref, *, final_norm):
    x = x_ref[...]
    d = x.shape[-1]
    h = _rms(x, gin_ref[...]).astype(BF16)
    a_sb = (osb_ref[...] * jax.nn.silu(_dot(h, wg_ref[:, :SB_WIDTH]))).astype(BF16)
    a_mla = (omla_ref[...] * jax.nn.silu(_dot(h, wg_ref[:, SB_WIDTH:SB_WIDTH + MLA_WIDTH]))).astype(BF16)
    g0 = SB_WIDTH + MLA_WIDTH
    g_sb = jax.nn.sigmoid(_dot(h, wg_ref[:, g0:g0 + d]) + bg_ref[:, :d])
    merged = g_sb * _dot(a_sb, wosb_ref[...])
    g_mla = jax.nn.sigmoid(_dot(h, wg_ref[:, g0 + d:g0 + 2 * d]) + bg_ref[:, d:])
    merged = merged + g_mla * _dot(a_mla, womla_ref[...])
    y = x + _dot(merged.astype(BF16), wout_ref[...])
    o_ref[...] = _rms(y, gf_ref[...]) if final_norm else y


def _out_call(x2, osb, omla, gin, wg, bg, wosb, womla, wout, gf, final_norm):
    n, d = x2.shape
    tm = min(TOKEN_TILE, n)
    row = lambda i: (i, 0)
    fix = lambda i: (0, 0)
    consts = (gin, wg, bg, wosb, womla, wout, gf)
    return pl.pallas_call(
        functools.partial(_out_kernel, final_norm=final_norm),
        grid=(n // tm,),
        in_specs=[pl.BlockSpec((tm, d), row), pl.BlockSpec((tm, SB_WIDTH), row),
                  pl.BlockSpec((tm, MLA_WIDTH), row)] + [pl.BlockSpec(c.shape, fix) for c in consts],
        out_specs=pl.BlockSpec((tm, d), row),
        out_shape=jax.ShapeDtypeStruct((n, d), F32),
        compiler_params=pltpu.CompilerParams(
            dimension_semantics=("arbitrary",), vmem_limit_bytes=VMEM_LIMIT_BYTES),
        name="out",
    )(x2, osb, omla, *consts)


def _rope_tables(s):
    half = MLA_ROPE_DIM // 2
    inv_freq = ROPE_THETA ** (-jnp.arange(half, dtype=F32) / half)
    ang = jnp.arange(s, dtype=F32)[:, None] * inv_freq[None, :]
    cos, sin = jnp.cos(ang), jnp.sin(ang)
    z_nope = jnp.zeros((s, MLA_NOPE_DIM), F32)
    z_pad = jnp.zeros((s, LANES - MLA_QK_DIM), F32)
    scale = MLA_QK_DIM ** -0.5 * np.log2(np.e)
    cq = jnp.concatenate([jnp.full((s, MLA_NOPE_DIM), scale, F32), cos * scale, cos * scale, z_pad], axis=1)
    sq = jnp.concatenate([z_nope, -sin * scale, sin * scale, z_pad], axis=1)
    ck = jnp.concatenate([z_nope, cos, cos, z_pad], axis=1)
    sk = jnp.concatenate([z_nope, -sin, sin, z_pad], axis=1)
    return cq, sq, ck, sk


def _swap_halves(w):
    half = w.shape[-1] // 2
    return jnp.concatenate([w[..., half:], w[..., :half]], axis=-1)


def _layer_weights(w_in, w_q_up, w_kv_up):
    d = w_in.shape[0]
    offs = np.cumsum([0, SB_WIDTH, SB_WIDTH, SB_WIDTH, SB_WIDTH, MLA_Q_LORA, MLA_KV_LORA, MLA_ROPE_DIM,
                      MLA_WIDTH, 2 * d])
    col = lambda i: w_in[:, offs[i]:offs[i + 1]]
    wqkv = jnp.concatenate([col(0) * (SB_HEAD_DIM ** -0.5), col(1), col(2)], axis=1)
    pad_to_group = lambda w: jnp.concatenate(
        [jnp.zeros((d, MLA_NOPE_DIM), F32), w, jnp.zeros((d, LANES - MLA_QK_DIM), F32)], axis=1)
    wkr1, wkr2 = pad_to_group(col(6)), pad_to_group(_swap_halves(col(6)))
    wg = jnp.concatenate([col(3), col(7), col(8)], axis=1)

    rq = w_q_up.shape[0]
    wq = w_q_up.reshape(rq, MLA_HEADS, MLA_QK_DIM)
    nope, rope = wq[..., :MLA_NOPE_DIM], wq[..., MLA_NOPE_DIM:]
    zq = jnp.zeros((rq, MLA_HEADS, LANES - MLA_QK_DIM), F32)
    wq1 = jnp.concatenate([nope, rope, zq], axis=-1).reshape(rq, MLA_HEADS * LANES)
    wq2 = jnp.concatenate([jnp.zeros_like(nope), _swap_halves(rope), zq], axis=-1).reshape(rq, MLA_HEADS * LANES)

    rkv = w_kv_up.shape[0]
    wkv = w_kv_up.reshape(rkv, MLA_HEADS, MLA_NOPE_DIM + MLA_V_DIM)
    wkn = jnp.concatenate([wkv[..., :MLA_NOPE_DIM], jnp.zeros((rkv, MLA_HEADS, LANES - MLA_NOPE_DIM), F32)],
                          axis=-1).reshape(rkv, MLA_HEADS * LANES)
    wv = wkv[..., MLA_NOPE_DIM:].reshape(rkv, MLA_WIDTH)
    bf = lambda w: w.astype(BF16)
    return dict(wqkv=bf(wqkv), wcq=bf(col(4)), wckv=bf(col(5)), wkr1=bf(wkr1), wkr2=bf(wkr2),
                wq1=bf(wq1), wq2=bf(wq2), wkn=bf(wkn), wv=bf(wv), wg=bf(wg))


def kernel(x, norm_in_g, w_in, b_gate, q_norm_g, w_q_up, kv_norm_g, w_kv_up, w_o_sb, w_o_mla, w_out, norm_f_g):
    b, s, d = x.shape
    depth = w_in.shape[0]
    assert s % ATT_BLOCK == 0 and s % SB_BLOCK == 0 and s % min(TOKEN_TILE, s) == 0
    tabs = _rope_tables(s)
    idx = jnp.arange(SB_BLOCK)
    from_mat = (idx[None, :] >= idx[:, None]).astype(BF16)

    x2 = x.reshape(b * s, d)
    for l in range(depth):
        w = _layer_weights(w_in[l], w_q_up[l], w_kv_up[l])
        gin = norm_in_g[l][None, :]
        qsb, ksb, vsb, qf, kf, vm = _proj_call(
            x2, gin,
            (w["wqkv"], w["wcq"], w["wckv"], w["wkr1"], w["wkr2"], q_norm_g[l][None, :], kv_norm_g[l][None, :],
             w["wq1"], w["wq2"], w["wkn"], w["wv"]),
            tabs)
        r3 = lambda a: a.reshape(b, s, a.shape[-1])
        o_sb = _sb_call(r3(qsb), r3(ksb), r3(vsb), from_mat).reshape(b * s, SB_WIDTH)
        o_mla = _mla_call(r3(qf), r3(kf), r3(vm)).reshape(b * s, MLA_WIDTH)
        x2 = _out_call(x2, o_sb, o_mla, gin, w["wg"], b_gate[l][None, :], w_o_sb[l].astype(BF16),
                       w_o_mla[l].astype(BF16), w_out[l].astype(BF16), norm_f_g[None, :],
                       final_norm=(l == depth - 1))
    return x2.reshape(b, s, d)
```

```python
import functools

import jax
import jax.numpy as jnp
import numpy as np
from jax import lax
from jax.experimental import pallas as pl
from jax.experimental.pallas import tpu as pltpu

EPS = 1e-6
CHUNK = 64

SB_HEADS = 8
SB_HEAD_DIM = 64
SB_WIDTH = SB_HEADS * SB_HEAD_DIM

MLA_HEADS = 8
MLA_NOPE_DIM = 64
MLA_ROPE_DIM = 32
MLA_V_DIM = 64
MLA_Q_LORA = 384
MLA_KV_LORA = 256
MLA_QK_DIM = MLA_NOPE_DIM + MLA_ROPE_DIM
MLA_WIDTH = MLA_HEADS * MLA_V_DIM
ROPE_THETA = 10000.0

LANES = 128
ATT_BLOCK = 512
SB_BLOCK = 256
SB_DEAD_LOG = 110.0
HEADS_PER_STEP = 4
V_ROWS = 80
TOKEN_TILE = 512
VMEM_LIMIT_BYTES = 56 * 1024 * 1024

F32 = jnp.float32
BF16 = jnp.bfloat16


def _dot(a, b):
    return jnp.dot(a, b, preferred_element_type=F32)


def _rms(x, g):
    inv = lax.rsqrt(jnp.mean(x * x, axis=-1, keepdims=True) + EPS)
    return (x * inv) * g


def _proj_kernel(x_ref, gin_ref, wqkv_ref, wc_ref, wckv_ref, qng_ref, kvng_ref, wq_ref, wkn_ref, wv_ref,
                 cq_ref, sq_lo_ref, sq_hi_ref, ck_ref, sk_lo_ref, sk_hi_ref,
                 qsb_ref, ksb_ref, vsb_ref, qf_ref, kf_ref, vm_ref):
    h = _rms(x_ref[...], gin_ref[...]).astype(BF16)

    c = _dot(h, wc_ref[...])
    ckv = _dot(h, wckv_ref[...])
    qsb_ref[...] = (_dot(h, wqkv_ref[:, :SB_WIDTH]) * SB_HEAD_DIM ** -0.5).astype(BF16)
    cq = _rms(c[:, :MLA_Q_LORA], qng_ref[...]).astype(BF16)
    ckv = _rms(ckv, kvng_ref[...]).astype(BF16)

    def rope(v, cos, sin_lo, sin_hi):
        return (v * cos + pltpu.roll(v, LANES - MLA_ROPE_DIM // 2, 1) * sin_lo
                + pltpu.roll(v, MLA_ROPE_DIM // 2, 1) * sin_hi)

    k_rope = rope(c[:, MLA_Q_LORA:], ck_ref[...], sk_lo_ref[...], sk_hi_ref[...])
    q = _dot(cq, wq_ref[...])
    kn = _dot(ckv, wkn_ref[...])
    vm_ref[...] = _dot(ckv, wv_ref[...]).astype(BF16)
    ksb_ref[...] = _dot(h, wqkv_ref[:, SB_WIDTH:2 * SB_WIDTH]).astype(BF16)
    vsb_ref[...] = _dot(h, wqkv_ref[:, 2 * SB_WIDTH:]).astype(BF16)
    cq_t, sq_lo, sq_hi = cq_ref[...], sq_lo_ref[...], sq_hi_ref[...]
    for hd in range(MLA_HEADS):
        sl = slice(hd * LANES, (hd + 1) * LANES)
        qf_ref[:, sl] = rope(q[:, sl], cq_t, sq_lo, sq_hi).astype(BF16)
        kf_ref[:, sl] = (kn[:, sl] + k_rope).astype(BF16)


def _proj_call(x2, gin, w_in_bf, wts, tabs):
    n, d = x2.shape
    s = tabs[0].shape[0]
    tm = min(TOKEN_TILE, s)
    row = lambda i: (i, 0)
    fix = lambda i: (0, 0)
    w_specs = [pl.BlockSpec(w.shape, fix) for w in wts]
    t_specs = [pl.BlockSpec((tm, LANES), lambda i: (i % (s // tm), 0)) for _ in tabs]
    outs = [(SB_WIDTH, BF16)] * 3 + [(MLA_HEADS * LANES, BF16)] * 2 + [(MLA_WIDTH, BF16)]
    return pl.pallas_call(
        _proj_kernel,
        grid=(n // tm,),
        in_specs=[pl.BlockSpec((tm, d), row), pl.BlockSpec(gin.shape, fix),
                  pl.BlockSpec((d, 3 * SB_WIDTH), fix)] + w_specs + t_specs,
        out_specs=[pl.BlockSpec((tm, w), row) for w, _ in outs],
        out_shape=[jax.ShapeDtypeStruct((n, w), dt) for w, dt in outs],
        compiler_params=pltpu.CompilerParams(
            dimension_semantics=("arbitrary",), vmem_limit_bytes=VMEM_LIMIT_BYTES),
        name="proj",
    )(x2, gin, w_in_bf, *wts, *tabs)


def _transpose_v(v_ref, vt_ref, nblk, blk):
    def body(j, carry):
        rows = v_ref[pl.ds(pl.multiple_of(j * blk, blk), blk), :]
        vt_ref[j] = rows.astype(F32).T.astype(BF16)
        return carry

    lax.fori_loop(0, nblk, body, 0)


def _sb_kernel(q_ref, k_ref, v_ref, u_ref, o_ref, vt_ref, acc_ref, *, nblk):
    t = SB_BLOCK
    qi = pl.program_id(1)

    @pl.when(qi == 0)
    def _():
        _transpose_v(v_ref, vt_ref, nblk, t)

    first = lax.broadcasted_iota(jnp.int32, (LANES, t), 0) < SB_HEAD_DIM
    q_heads = []
    for p in range(SB_HEADS // 2):
        q_t = q_ref[:, p * LANES:(p + 1) * LANES].astype(F32).T
        q_heads += [jnp.where(first, q_t, 0.0).astype(BF16), jnp.where(first, 0.0, q_t).astype(BF16)]

    def block(j, r, strict):
        k_blk = k_ref[pl.ds(pl.multiple_of(j * t, t), t), :]
        heads = range(SB_HEADS)
        z = [_dot(k_blk[:, (hd // 2) * LANES:(hd // 2 + 1) * LANES], q_heads[hd]) for hd in heads]
        incl = []
        for hd in heads:
            neg_log_1m = jnp.maximum(z[hd], 0.0) + jnp.log(1.0 + jnp.exp(-jnp.abs(z[hd])))
            if strict is not None:
                neg_log_1m = jnp.where(strict, neg_log_1m, 0.0)
            incl.append(_dot(u_ref[...], neg_log_1m.astype(BF16)))
        for hd in heads:
            w = jnp.exp(z[hd] - incl[hd] + r[hd])
            if strict is not None:
                w = jnp.where(strict, w, 0.0)
            vt_blk = vt_ref[j, hd * SB_HEAD_DIM:(hd + 1) * SB_HEAD_DIM, :]
            acc_ref[hd] += _dot(vt_blk, w.astype(BF16))
        return tuple(r[hd] - incl[hd][0:1, :] for hd in heads)

    def r_max(r):
        return functools.reduce(jnp.maximum, [jnp.max(x) for x in r])

    acc_ref[...] = jnp.zeros_like(acc_ref)
    zero = jnp.zeros((1, t), F32)
    key_pos = lax.broadcasted_iota(jnp.int32, (t, t), 0)
    qry_pos = lax.broadcasted_iota(jnp.int32, (t, t), 1)
    r = block(qi, (zero,) * SB_HEADS, key_pos < qry_pos)

    def cond(c):
        j, rmax, _ = c
        return jnp.logical_and(j >= 0, rmax > -SB_DEAD_LOG)

    def body(c):
        j, _, r = c
        r = block(j, r, None)
        return j - 1, r_max(r), r

    lax.while_loop(cond, body, (qi - 1, r_max(r), r))

    o_ref[...] = jnp.concatenate([acc_ref[hd] for hd in range(SB_HEADS)], axis=0).T


def _sb_call(q, k, v, u):
    b, s, w = q.shape
    t = SB_BLOCK
    nblk = s // t
    resident = lambda: pl.BlockSpec((None, s, w), lambda bi, qi: (bi, 0, 0), pipeline_mode=pl.Buffered(1))
    return pl.pallas_call(
        functools.partial(_sb_kernel, nblk=nblk),
        grid=(b, nblk),
        in_specs=[
            pl.BlockSpec((None, t, w), lambda bi, qi: (bi, qi, 0)),
            resident(),
            resident(),
            pl.BlockSpec(u.shape, lambda bi, qi: (0, 0)),
        ],
        out_specs=pl.BlockSpec((None, t, w), lambda bi, qi: (bi, qi, 0)),
        out_shape=jax.ShapeDtypeStruct((b, s, w), F32),
        scratch_shapes=[
            pltpu.VMEM((nblk, w, t), BF16),
            pltpu.VMEM((SB_HEADS, SB_HEAD_DIM, t), F32),
        ],
        compiler_params=pltpu.CompilerParams(
            dimension_semantics=("arbitrary", "arbitrary"), vmem_limit_bytes=VMEM_LIMIT_BYTES),
        name="sb_attn",
    )(q, k, v, u)


def _mla_kernel(q_ref, k_ref, v_ref, o_ref, vt_ref, acc_ref, m_ref, s0_ref, s1_ref, smax0_ref, smax1_ref, *,
                nblk):
    t = ATT_BLOCK
    qi = pl.program_id(2)

    @pl.when(qi == 0)
    def _():
        ones_rows = (lax.broadcasted_iota(jnp.int32, (V_ROWS - MLA_V_DIM, t), 0) == 0).astype(BF16)

        def body(j, carry):
            rows = v_ref[pl.ds(pl.multiple_of(j * t, t), t), :].astype(F32).T.astype(BF16)
            for hd in range(HEADS_PER_STEP):
                vt_ref[j, hd, :MLA_V_DIM, :] = rows[hd * MLA_V_DIM:(hd + 1) * MLA_V_DIM, :]
                vt_ref[j, hd, MLA_V_DIM:, :] = ones_rows
            return carry

        lax.fori_loop(0, nblk, body, 0)

    q_heads = [q_ref[:, hd * LANES:(hd + 1) * LANES].astype(F32).T.astype(BF16) for hd in range(HEADS_PER_STEP)]

    def scores(hd, j, buf, allowed):
        s_ref, smax_ref = buf
        k_blk = k_ref[pl.ds(pl.multiple_of(j * t, t), t), hd * LANES:(hd + 1) * LANES]
        s = _dot(k_blk, q_heads[hd])
        if allowed is not None:
            s = jnp.where(allowed, s, -jnp.inf)
        s_ref[hd] = s
        smax_ref[hd] = jnp.max(s, axis=0, keepdims=True)

    def update(hd, j, buf):
        s_ref, smax_ref = buf
        m = m_ref[hd]
        m_new = jnp.maximum(m, smax_ref[hd])
        m_ref[hd] = m_new
        p = jnp.exp2(s_ref[hd] - m_new).astype(BF16)
        acc_ref[hd] = jnp.exp2(m - m_new) * acc_ref[hd] + _dot(vt_ref[j, hd], p)

    acc_ref[...] = jnp.zeros_like(acc_ref)
    m_ref[...] = jnp.full_like(m_ref, -jnp.inf)
    key_chunk = lax.broadcasted_iota(jnp.int32, (t, t), 0) // CHUNK
    qry_chunk = lax.broadcasted_iota(jnp.int32, (t, t), 1) // CHUNK
    bufs = ((s0_ref, smax0_ref), (s1_ref, smax1_ref))

    block_of = lambda k: jnp.where(k == 0, qi, k - 1)
    allowed = key_chunk <= qry_chunk
    for hd in range(HEADS_PER_STEP):
        scores(hd, qi, bufs[0], allowed)

    def body(k, carry):
        for parity in range(2):
            @pl.when(k % 2 == parity)
            def _():
                for hd in range(HEADS_PER_STEP):
                    scores(hd, block_of(k + 1), bufs[1 - parity], None)
                    update(hd, block_of(k), bufs[parity])
        return carry

    lax.fori_loop(0, qi, body, 0)

    for parity in range(2):
        @pl.when(qi % 2 == parity)
        def _():
            for hd in range(HEADS_PER_STEP):
                update(hd, block_of(qi), bufs[parity])

    o_ref[...] = jnp.concatenate(
        [acc_ref[hd, :MLA_V_DIM, :] / acc_ref[hd, MLA_V_DIM:MLA_V_DIM + 1, :] for hd in range(HEADS_PER_STEP)],
        axis=0).T


def _mla_call(qf, kf, vm):
    b, s, _ = qf.shape
    t = ATT_BLOCK
    nblk = s // t
    groups = MLA_HEADS // HEADS_PER_STEP
    return pl.pallas_call(
        functools.partial(_mla_kernel, nblk=nblk),
        grid=(b, groups, nblk),
        in_specs=[
            pl.BlockSpec((None, t, HEADS_PER_STEP * LANES), lambda bi, g, qi: (bi, qi, g)),
            pl.BlockSpec((None, s, HEADS_PER_STEP * LANES), lambda bi, g, qi: (bi, 0, g)),
            pl.BlockSpec((None, s, HEADS_PER_STEP * MLA_V_DIM), lambda bi, g, qi: (bi, 0, g)),
        ],
        out_specs=pl.BlockSpec((None, t, HEADS_PER_STEP * MLA_V_DIM), lambda bi, g, qi: (bi, qi, g)),
        out_shape=jax.ShapeDtypeStruct((b, s, MLA_WIDTH), F32),
        scratch_shapes=[
            pltpu.VMEM((nblk, HEADS_PER_STEP, V_ROWS, t), BF16),
            pltpu.VMEM((HEADS_PER_STEP, V_ROWS, t), F32),
            pltpu.VMEM((HEADS_PER_STEP, 1, t), F32),
            pltpu.VMEM((HEADS_PER_STEP, t, t), F32),
            pltpu.VMEM((HEADS_PER_STEP, t, t), F32),
            pltpu.VMEM((HEADS_PER_STEP, 1, t), F32),
            pltpu.VMEM((HEADS_PER_STEP, 1, t), F32),
        ],
        compiler_params=pltpu.CompilerParams(
            dimension_semantics=("arbitrary", "arbitrary", "arbitrary"), vmem_limit_bytes=VMEM_LIMIT_BYTES),
        name="mla_attn",
    )(qf, kf, vm)


def _out_kernel(x_ref, osb_ref, omla_ref, gin_ref, wgsb_ref, wg_ref, bg_ref, wosb_ref, womla_ref, wout_ref,
                gf_ref, o_ref, *, final_norm):
    x = x_ref[...]
    d = x.shape[-1]
    h = _rms(x, gin_ref[...]).astype(BF16)
    a_sb = (osb_ref[...] * jax.nn.silu(_dot(h, wgsb_ref[...]))).astype(BF16)
    a_mla = (omla_ref[...] * jax.nn.silu(_dot(h, wg_ref[:, :MLA_WIDTH]))).astype(BF16)
    g0 = MLA_WIDTH
    g_sb = jax.nn.sigmoid(_dot(h, wg_ref[:, g0:g0 + d]) + bg_ref[:, :d])
    merged = g_sb * _dot(a_sb, wosb_ref[...])
    g_mla = jax.nn.sigmoid(_dot(h, wg_ref[:, g0 + d:g0 + 2 * d]) + bg_ref[:, d:])
    merged = merged + g_mla * _dot(a_mla, womla_ref[...])
    y = x + _dot(merged.astype(BF16), wout_ref[...])
    o_ref[...] = _rms(y, gf_ref[...]) if final_norm else y


def _out_call(x2, osb, omla, gin, w_in_bf, wg, bg, wosb, womla, wout, gf, final_norm):
    n, d = x2.shape
    tm = min(TOKEN_TILE, n)
    row = lambda i: (i, 0)
    fix = lambda i: (0, 0)
    consts = (wg, bg, wosb, womla, wout, gf)
    return pl.pallas_call(
        functools.partial(_out_kernel, final_norm=final_norm),
        grid=(n // tm,),
        in_specs=[pl.BlockSpec((tm, d), row), pl.BlockSpec((tm, SB_WIDTH), row), pl.BlockSpec((tm, MLA_WIDTH), row),
                  pl.BlockSpec(gin.shape, fix),
                  pl.BlockSpec((d, SB_WIDTH), lambda i: (0, 3))]
                 + [pl.BlockSpec(c.shape, fix) for c in consts],
        out_specs=pl.BlockSpec((tm, d), row),
        out_shape=jax.ShapeDtypeStruct((n, d), F32),
        compiler_params=pltpu.CompilerParams(
            dimension_semantics=("arbitrary",), vmem_limit_bytes=VMEM_LIMIT_BYTES),
        name="out",
    )(x2, osb, omla, gin, w_in_bf, *consts)


def _rope_tables(s):
    half = MLA_ROPE_DIM // 2
    inv_freq = ROPE_THETA ** (-jnp.arange(half, dtype=F32) / half)
    ang = jnp.arange(s, dtype=F32)[:, None] * inv_freq[None, :]
    cos, sin = jnp.cos(ang), jnp.sin(ang)
    zeros = lambda w: jnp.zeros((s, w), F32)
    lanes = lambda lo, hi: jnp.concatenate(
        [zeros(MLA_NOPE_DIM), lo, hi, zeros(LANES - MLA_QK_DIM)], axis=1)
    scale = MLA_QK_DIM ** -0.5 * np.log2(np.e)
    cq = jnp.concatenate([jnp.full((s, MLA_NOPE_DIM), scale, F32), cos * scale, cos * scale,
                          zeros(LANES - MLA_QK_DIM)], axis=1)
    return (cq, lanes(-sin * scale, zeros(half)), lanes(zeros(half), sin * scale),
            lanes(cos, cos), lanes(-sin, zeros(half)), lanes(zeros(half), sin))


def _layer_weights(w_in_bf, w_q_up, w_kv_up):
    d = w_in_bf.shape[0]
    o_cq = 4 * SB_WIDTH
    o_ckv = o_cq + MLA_Q_LORA
    o_kr = o_ckv + MLA_KV_LORA
    o_gm = o_kr + MLA_ROPE_DIM
    zeros = lambda r, w: jnp.zeros((r, w), BF16)
    wc = jnp.concatenate([w_in_bf[:, o_cq:o_ckv], zeros(d, MLA_NOPE_DIM), w_in_bf[:, o_kr:o_gm],
                          zeros(d, LANES - MLA_QK_DIM)], axis=1)
    wckv = w_in_bf[:, o_ckv:o_kr]
    wg = w_in_bf[:, o_gm:]

    rq = w_q_up.shape[0]
    wq = w_q_up.astype(BF16).reshape(rq, MLA_HEADS, MLA_QK_DIM)
    wq = jnp.concatenate([wq, jnp.zeros((rq, MLA_HEADS, LANES - MLA_QK_DIM), BF16)], axis=-1)
    rkv = w_kv_up.shape[0]
    wkv = w_kv_up.astype(BF16).reshape(rkv, MLA_HEADS, MLA_NOPE_DIM + MLA_V_DIM)
    wkn = jnp.concatenate([wkv[..., :MLA_NOPE_DIM], jnp.zeros((rkv, MLA_HEADS, LANES - MLA_NOPE_DIM), BF16)], axis=-1)
    wv = wkv[..., MLA_NOPE_DIM:]
    return dict(wc=wc, wckv=wckv, wg=wg, wq=wq.reshape(rq, MLA_HEADS * LANES),
                wkn=wkn.reshape(rkv, MLA_HEADS * LANES), wv=wv.reshape(rkv, MLA_WIDTH))


def kernel(x, norm_in_g, w_in, b_gate, q_norm_g, w_q_up, kv_norm_g, w_kv_up, w_o_sb, w_o_mla, w_out, norm_f_g):
    b, s, d = x.shape
    depth = w_in.shape[0]
    assert s % ATT_BLOCK == 0 and s % SB_BLOCK == 0 and s % min(TOKEN_TILE, s) == 0
    tabs = _rope_tables(s)
    idx = jnp.arange(SB_BLOCK)
    from_mat = (idx[None, :] >= idx[:, None]).astype(BF16)

    x2 = x.reshape(b * s, d)
    for l in range(depth):
        w_in_bf = w_in[l].astype(BF16)
        w = _layer_weights(w_in_bf, w_q_up[l], w_kv_up[l])
        gin = norm_in_g[l][None, :]
        qsb, ksb, vsb, qf, kf, vm = _proj_call(
            x2, gin, w_in_bf,
            (w["wc"], w["wckv"], q_norm_g[l][None, :], kv_norm_g[l][None, :], w["wq"], w["wkn"], w["wv"]),
            tabs)
        r3 = lambda a: a.reshape(b, s, a.shape[-1])
        o_sb = _sb_call(r3(qsb), r3(ksb), r3(vsb), from_mat).reshape(b * s, SB_WIDTH)
        o_mla = _mla_call(r3(qf), r3(kf), r3(vm)).reshape(b * s, MLA_WIDTH)
        x2 = _out_call(x2, o_sb, o_mla, gin, w_in_bf, w["wg"], b_gate[l][None, :], w_o_sb[l].astype(BF16),
                       w_o_mla[l].astype(BF16), w_out[l].astype(BF16), norm_f_g[None, :],
                       final_norm=(l == depth - 1))
    return x2.reshape(b, s, d)
```

```python
import functools

import jax
import jax.numpy as jnp
import numpy as np
from jax import lax
from jax.experimental import pallas as pl
from jax.experimental.pallas import tpu as pltpu

EPS = 1e-6
CHUNK = 64

SB_HEADS = 8
SB_HEAD_DIM = 64
SB_WIDTH = SB_HEADS * SB_HEAD_DIM

MLA_HEADS = 8
MLA_NOPE_DIM = 64
MLA_ROPE_DIM = 32
MLA_V_DIM = 64
MLA_Q_LORA = 384
MLA_KV_LORA = 256
MLA_QK_DIM = MLA_NOPE_DIM + MLA_ROPE_DIM
MLA_WIDTH = MLA_HEADS * MLA_V_DIM
ROPE_THETA = 10000.0

LANES = 128
ATT_BLOCK = 512
SB_BLOCK = 256
SB_DEAD_LOG2 = 160.0
LOG2_E = float(np.log2(np.e))
HEADS_PER_STEP = 4
V_ROWS = 80
TOKEN_TILE = 512
VMEM_LIMIT_BYTES = 56 * 1024 * 1024

F32 = jnp.float32
BF16 = jnp.bfloat16


def _dot(a, b):
    return jnp.dot(a, b, preferred_element_type=F32)


def _rms(x, g):
    inv = lax.rsqrt(jnp.mean(x * x, axis=-1, keepdims=True) + EPS)
    return (x * inv) * g


def _proj_kernel(x_ref, gin_ref, wqkv_ref, wc_ref, wckv_ref, qng_ref, kvng_ref, wq_ref, wkn_ref, wv_ref,
                 cos_ref, sin_ref, qsb_ref, ksb_ref, vsb_ref, qf_ref, kf_ref, vm_ref):
    h = _rms(x_ref[...], gin_ref[...]).astype(BF16)

    c = _dot(h, wc_ref[...])
    ckv = _dot(h, wckv_ref[...])
    qsb_ref[...] = (_dot(h, wqkv_ref[:, :SB_WIDTH]) * (SB_HEAD_DIM ** -0.5 * LOG2_E)).astype(BF16)
    cq = _rms(c[:, :MLA_Q_LORA], qng_ref[...]).astype(BF16)
    ckv = _rms(ckv, kvng_ref[...]).astype(BF16)

    def rope(v, cos, sin_lo, sin_hi):
        return (v * cos + pltpu.roll(v, LANES - MLA_ROPE_DIM // 2, 1) * sin_lo
                + pltpu.roll(v, MLA_ROPE_DIM // 2, 1) * sin_hi)

    lane = lax.broadcasted_iota(jnp.int32, cos_ref.shape, 1)
    ck_t, sin_t = cos_ref[...], sin_ref[...]
    sk_lo = jnp.where(lane < MLA_NOPE_DIM + MLA_ROPE_DIM // 2, sin_t, 0.0)
    sk_hi = sin_t - sk_lo
    scale = MLA_QK_DIM ** -0.5 * np.log2(np.e)
    cq_t = jnp.where(lane < MLA_NOPE_DIM, scale, ck_t * scale)
    sq_lo, sq_hi = sk_lo * scale, sk_hi * scale

    k_rope = rope(c[:, MLA_Q_LORA:], ck_t, sk_lo, sk_hi)
    q = _dot(cq, wq_ref[...])
    kn = _dot(ckv, wkn_ref[...])
    vm_ref[...] = _dot(ckv, wv_ref[...]).astype(BF16)
    ksb_ref[...] = _dot(h, wqkv_ref[:, SB_WIDTH:2 * SB_WIDTH]).astype(BF16)
    vsb_ref[...] = _dot(h, wqkv_ref[:, 2 * SB_WIDTH:]).astype(BF16)
    for hd in range(MLA_HEADS):
        sl = slice(hd * LANES, (hd + 1) * LANES)
        qf_ref[:, sl] = rope(q[:, sl], cq_t, sq_lo, sq_hi).astype(BF16)
        kf_ref[:, sl] = (kn[:, sl] + k_rope).astype(BF16)


def _proj_call(x2, gin, w_in_bf, wts, tabs):
    n, d = x2.shape
    s = tabs[0].shape[0]
    tm = min(TOKEN_TILE, s)
    row = lambda i: (i, 0)
    fix = lambda i: (0, 0)
    w_specs = [pl.BlockSpec(w.shape, fix) for w in wts]
    t_specs = [pl.BlockSpec((tm, LANES), lambda i: (i % (s // tm), 0)) for _ in tabs]
    outs = [(SB_WIDTH, BF16)] * 3 + [(MLA_HEADS * LANES, BF16)] * 2 + [(MLA_WIDTH, BF16)]
    return pl.pallas_call(
        _proj_kernel,
        grid=(n // tm,),
        in_specs=[pl.BlockSpec((tm, d), row), pl.BlockSpec(gin.shape, fix),
                  pl.BlockSpec((d, 3 * SB_WIDTH), fix)] + w_specs + t_specs,
        out_specs=[pl.BlockSpec((tm, w), row) for w, _ in outs],
        out_shape=[jax.ShapeDtypeStruct((n, w), dt) for w, dt in outs],
        compiler_params=pltpu.CompilerParams(
            dimension_semantics=("arbitrary",), vmem_limit_bytes=VMEM_LIMIT_BYTES),
        name="proj",
    )(x2, gin, w_in_bf, *wts, *tabs)


def _transpose_v(v_ref, vt_ref, nblk, blk):
    def body(j, carry):
        rows = v_ref[pl.ds(pl.multiple_of(j * blk, blk), blk), :]
        vt_ref[j] = rows.astype(F32).T.astype(BF16)
        return carry

    lax.fori_loop(0, nblk, body, 0)


def _sb_kernel(q_ref, k_ref, v_ref, u_ref, o_ref, vt_ref, acc_ref, *, nblk):
    t = SB_BLOCK
    qi = pl.program_id(1)

    @pl.when(qi == 0)
    def _():
        _transpose_v(v_ref, vt_ref, nblk, t)

    first = lax.broadcasted_iota(jnp.int32, (LANES, t), 0) < SB_HEAD_DIM
    q_heads = []
    for p in range(SB_HEADS // 2):
        q_t = q_ref[:, p * LANES:(p + 1) * LANES].astype(F32).T
        q_heads += [jnp.where(first, q_t, 0.0).astype(BF16), jnp.where(first, 0.0, q_t).astype(BF16)]

    def block(j, r, strict):
        k_blk = k_ref[pl.ds(pl.multiple_of(j * t, t), t), :]
        heads = range(SB_HEADS)
        z = [_dot(k_blk[:, (hd // 2) * LANES:(hd // 2 + 1) * LANES], q_heads[hd]) for hd in heads]
        incl = []
        for hd in heads:
            neg_log_1m = jnp.maximum(z[hd], 0.0) + jnp.log(1.0 + jnp.exp2(-jnp.abs(z[hd]))) * LOG2_E
            if strict is not None:
                neg_log_1m = jnp.where(strict, neg_log_1m, 0.0)
            incl.append(_dot(u_ref[...], neg_log_1m.astype(BF16)))
        for hd in heads:
            w = jnp.exp2(z[hd] - incl[hd] + r[hd])
            if strict is not None:
                w = jnp.where(strict, w, 0.0)
            vt_blk = vt_ref[j, hd * SB_HEAD_DIM:(hd + 1) * SB_HEAD_DIM, :]
            acc_ref[hd] += _dot(vt_blk, w.astype(BF16))
        return tuple(r[hd] - incl[hd][0:1, :] for hd in heads)

    def r_max(r):
        return functools.reduce(jnp.maximum, [jnp.max(x) for x in r])

    acc_ref[...] = jnp.zeros_like(acc_ref)
    zero = jnp.zeros((1, t), F32)
    key_pos = lax.broadcasted_iota(jnp.int32, (t, t), 0)
    qry_pos = lax.broadcasted_iota(jnp.int32, (t, t), 1)
    r = block(qi, (zero,) * SB_HEADS, key_pos < qry_pos)

    def cond(c):
        j, rmax, _ = c
        return jnp.logical_and(j >= 0, rmax > -SB_DEAD_LOG2)

    def body(c):
        j, _, r = c
        r = block(j, r, None)
        return j - 1, r_max(r), r

    lax.while_loop(cond, body, (qi - 1, r_max(r), r))

    o_ref[...] = jnp.concatenate([acc_ref[hd] for hd in range(SB_HEADS)], axis=0).T


def _sb_call(q, k, v, u):
    b, s, w = q.shape
    t = SB_BLOCK
    nblk = s // t
    resident = lambda: pl.BlockSpec((None, s, w), lambda bi, qi: (bi, 0, 0), pipeline_mode=pl.Buffered(1))
    return pl.pallas_call(
        functools.partial(_sb_kernel, nblk=nblk),
        grid=(b, nblk),
        in_specs=[
            pl.BlockSpec((None, t, w), lambda bi, qi: (bi, qi, 0)),
            resident(),
            resident(),
            pl.BlockSpec(u.shape, lambda bi, qi: (0, 0)),
        ],
        out_specs=pl.BlockSpec((None, t, w), lambda bi, qi: (bi, qi, 0)),
        out_shape=jax.ShapeDtypeStruct((b, s, w), F32),
        scratch_shapes=[
            pltpu.VMEM((nblk, w, t), BF16),
            pltpu.VMEM((SB_HEADS, SB_HEAD_DIM, t), F32),
        ],
        compiler_params=pltpu.CompilerParams(
            dimension_semantics=("arbitrary", "arbitrary"), vmem_limit_bytes=VMEM_LIMIT_BYTES),
        name="sb_attn",
    )(q, k, v, u)


def _mla_kernel(q_ref, k_ref, v_ref, o_ref, vt_ref, acc_ref, m_ref, s0_ref, s1_ref, smax0_ref, smax1_ref, *,
                nblk):
    t = ATT_BLOCK
    qi = pl.program_id(2)

    @pl.when(qi == 0)
    def _():
        ones_rows = (lax.broadcasted_iota(jnp.int32, (V_ROWS - MLA_V_DIM, t), 0) == 0).astype(BF16)

        def body(j, carry):
            rows = v_ref[pl.ds(pl.multiple_of(j * t, t), t), :].astype(F32).T.astype(BF16)
            for hd in range(HEADS_PER_STEP):
                vt_ref[j, hd, :MLA_V_DIM, :] = rows[hd * MLA_V_DIM:(hd + 1) * MLA_V_DIM, :]
                vt_ref[j, hd, MLA_V_DIM:, :] = ones_rows
            return carry

        lax.fori_loop(0, nblk, body, 0)

    q_heads = [q_ref[:, hd * LANES:(hd + 1) * LANES].astype(F32).T.astype(BF16) for hd in range(HEADS_PER_STEP)]

    def scores(hd, j, buf, allowed):
        s_ref, smax_ref = buf
        k_blk = k_ref[pl.ds(pl.multiple_of(j * t, t), t), hd * LANES:(hd + 1) * LANES]
        s = _dot(k_blk, q_heads[hd])
        if allowed is not None:
            s = jnp.where(allowed, s, -jnp.inf)
        s_ref[hd] = s
        smax_ref[hd] = jnp.max(s, axis=0, keepdims=True)

    def update(hd, j, buf):
        s_ref, smax_ref = buf
        m = m_ref[hd]
        m_new = jnp.maximum(m, smax_ref[hd])
        m_ref[hd] = m_new
        p = jnp.exp2(s_ref[hd] - m_new).astype(BF16)
        acc_ref[hd] = jnp.exp2(m - m_new) * acc_ref[hd] + _dot(vt_ref[j, hd], p)

    acc_ref[...] = jnp.zeros_like(acc_ref)
    m_ref[...] = jnp.full_like(m_ref, -jnp.inf)
    key_chunk = lax.broadcasted_iota(jnp.int32, (t, t), 0) // CHUNK
    qry_chunk = lax.broadcasted_iota(jnp.int32, (t, t), 1) // CHUNK
    bufs = ((s0_ref, smax0_ref), (s1_ref, smax1_ref))

    block_of = lambda k: jnp.where(k == 0, qi, k - 1)
    allowed = key_chunk <= qry_chunk
    for hd in range(HEADS_PER_STEP):
        scores(hd, qi, bufs[0], allowed)

    def body(k, carry):
        for parity in range(2):
            @pl.when(k % 2 == parity)
            def _():
                for hd in range(HEADS_PER_STEP):
                    scores(hd, block_of(k + 1), bufs[1 - parity], None)
                    update(hd, block_of(k), bufs[parity])
        return carry

    lax.fori_loop(0, qi, body, 0)

    for parity in range(2):
        @pl.when(qi % 2 == parity)
        def _():
            for hd in range(HEADS_PER_STEP):
                update(hd, block_of(qi), bufs[parity])

    o_ref[...] = jnp.concatenate(
        [acc_ref[hd, :MLA_V_DIM, :] / acc_ref[hd, MLA_V_DIM:MLA_V_DIM + 1, :] for hd in range(HEADS_PER_STEP)],
        axis=0).T


def _mla_call(qf, kf, vm):
    b, s, _ = qf.shape
    t = ATT_BLOCK
    nblk = s // t
    groups = MLA_HEADS // HEADS_PER_STEP
    return pl.pallas_call(
        functools.partial(_mla_kernel, nblk=nblk),
        grid=(b, groups, nblk),
        in_specs=[
            pl.BlockSpec((None, t, HEADS_PER_STEP * LANES), lambda bi, g, qi: (bi, qi, g)),
            pl.BlockSpec((None, s, HEADS_PER_STEP * LANES), lambda bi, g, qi: (bi, 0, g)),
            pl.BlockSpec((None, s, HEADS_PER_STEP * MLA_V_DIM), lambda bi, g, qi: (bi, 0, g)),
        ],
        out_specs=pl.BlockSpec((None, t, HEADS_PER_STEP * MLA_V_DIM), lambda bi, g, qi: (bi, qi, g)),
        out_shape=jax.ShapeDtypeStruct((b, s, MLA_WIDTH), F32),
        scratch_shapes=[
            pltpu.VMEM((nblk, HEADS_PER_STEP, V_ROWS, t), BF16),
            pltpu.VMEM((HEADS_PER_STEP, V_ROWS, t), F32),
            pltpu.VMEM((HEADS_PER_STEP, 1, t), F32),
            pltpu.VMEM((HEADS_PER_STEP, t, t), F32),
            pltpu.VMEM((HEADS_PER_STEP, t, t), F32),
            pltpu.VMEM((HEADS_PER_STEP, 1, t), F32),
            pltpu.VMEM((HEADS_PER_STEP, 1, t), F32),
        ],
        compiler_params=pltpu.CompilerParams(
            dimension_semantics=("arbitrary", "arbitrary", "arbitrary"), vmem_limit_bytes=VMEM_LIMIT_BYTES),
        name="mla_attn",
    )(qf, kf, vm)


def _out_kernel(x_ref, osb_ref, omla_ref, gin_ref, wgsb_ref, wg_ref, bg_ref, wosb_ref, womla_ref, wout_ref,
                gf_ref, o_ref, *, final_norm):
    x = x_ref[...]
    d = x.shape[-1]
    h = _rms(x, gin_ref[...]).astype(BF16)
    a_sb = (osb_ref[...] * jax.nn.silu(_dot(h, wgsb_ref[...]))).astype(BF16)
    a_mla = (omla_ref[...] * jax.nn.silu(_dot(h, wg_ref[:, :MLA_WIDTH]))).astype(BF16)
    g0 = MLA_WIDTH
    g_sb = jax.nn.sigmoid(_dot(h, wg_ref[:, g0:g0 + d]) + bg_ref[:, :d])
    merged = g_sb * _dot(a_sb, wosb_ref[...])
    g_mla = jax.nn.sigmoid(_dot(h, wg_ref[:, g0 + d:g0 + 2 * d]) + bg_ref[:, d:])
    merged = merged + g_mla * _dot(a_mla, womla_ref[...])
    y = x + _dot(merged.astype(BF16), wout_ref[...])
    o_ref[...] = _rms(y, gf_ref[...]) if final_norm else y


def _out_call(x2, osb, omla, gin, w_in_bf, wg, bg, wosb, womla, wout, gf, final_norm):
    n, d = x2.shape
    tm = min(TOKEN_TILE, n)
    row = lambda i: (i, 0)
    fix = lambda i: (0, 0)
    consts = (wg, bg, wosb, womla, wout, gf)
    return pl.pallas_call(
        functools.partial(_out_kernel, final_norm=final_norm),
        grid=(n // tm,),
        in_specs=[pl.BlockSpec((tm, d), row), pl.BlockSpec((tm, SB_WIDTH), row), pl.BlockSpec((tm, MLA_WIDTH), row),
                  pl.BlockSpec(gin.shape, fix),
                  pl.BlockSpec((d, SB_WIDTH), lambda i: (0, 3))]
                 + [pl.BlockSpec(c.shape, fix) for c in consts],
        out_specs=pl.BlockSpec((tm, d), row),
        out_shape=jax.ShapeDtypeStruct((n, d), F32),
        compiler_params=pltpu.CompilerParams(
            dimension_semantics=("arbitrary",), vmem_limit_bytes=VMEM_LIMIT_BYTES),
        name="out",
    )(x2, osb, omla, gin, w_in_bf, *consts)


def _rope_tables(s):
    half = MLA_ROPE_DIM // 2
    inv_freq = ROPE_THETA ** (-np.arange(half, dtype=np.float64) / half)
    ang = np.arange(s, dtype=np.float64)[:, None] * inv_freq[None, :]
    cos, sin = np.cos(ang), np.sin(ang)
    lanes = lambda lo, hi: np.concatenate(
        [np.zeros((s, MLA_NOPE_DIM)), lo, hi, np.zeros((s, LANES - MLA_QK_DIM))], axis=1).astype(np.float32)
    return jnp.asarray(lanes(cos, cos)), jnp.asarray(lanes(-sin, sin))


def _layer_weights(w_in_bf, w_q_up, w_kv_up):
    d = w_in_bf.shape[0]
    o_cq = 4 * SB_WIDTH
    o_ckv = o_cq + MLA_Q_LORA
    o_kr = o_ckv + MLA_KV_LORA
    o_gm = o_kr + MLA_ROPE_DIM
    zeros = lambda r, w: jnp.zeros((r, w), BF16)
    wc = jnp.concatenate([w_in_bf[:, o_cq:o_ckv], zeros(d, MLA_NOPE_DIM), w_in_bf[:, o_kr:o_gm],
                          zeros(d, LANES - MLA_QK_DIM)], axis=1)
    wckv = w_in_bf[:, o_ckv:o_kr]
    wg = w_in_bf[:, o_gm:]

    rq = w_q_up.shape[0]
    wq = w_q_up.astype(BF16).reshape(rq, MLA_HEADS, MLA_QK_DIM)
    wq = jnp.concatenate([wq, jnp.zeros((rq, MLA_HEADS, LANES - MLA_QK_DIM), BF16)], axis=-1)
    rkv = w_kv_up.shape[0]
    wkv = w_kv_up.astype(BF16).reshape(rkv, MLA_HEADS, MLA_NOPE_DIM + MLA_V_DIM)
    wkn = jnp.concatenate([wkv[..., :MLA_NOPE_DIM], jnp.zeros((rkv, MLA_HEADS, LANES - MLA_NOPE_DIM), BF16)], axis=-1)
    wv = wkv[..., MLA_NOPE_DIM:]
    return dict(wc=wc, wckv=wckv, wg=wg, wq=wq.reshape(rq, MLA_HEADS * LANES),
                wkn=wkn.reshape(rkv, MLA_HEADS * LANES), wv=wv.reshape(rkv, MLA_WIDTH))


def kernel(x, norm_in_g, w_in, b_gate, q_norm_g, w_q_up, kv_norm_g, w_kv_up, w_o_sb, w_o_mla, w_out, norm_f_g):
    b, s, d = x.shape
    depth = w_in.shape[0]
    assert s % ATT_BLOCK == 0 and s % SB_BLOCK == 0 and s % min(TOKEN_TILE, s) == 0
    tabs = _rope_tables(s)
    idx = jnp.arange(SB_BLOCK)
    from_mat = (idx[None, :] >= idx[:, None]).astype(BF16)

    x2 = x.reshape(b * s, d)
    for l in range(depth):
        w_in_bf = w_in[l].astype(BF16)
        w = _layer_weights(w_in_bf, w_q_up[l], w_kv_up[l])
        gin = norm_in_g[l][None, :]
        qsb, ksb, vsb, qf, kf, vm = _proj_call(
            x2, gin, w_in_bf,
            (w["wc"], w["wckv"], q_norm_g[l][None, :], kv_norm_g[l][None, :], w["wq"], w["wkn"], w["wv"]),
            tabs)
        r3 = lambda a: a.reshape(b, s, a.shape[-1])
        o_sb = _sb_call(r3(qsb), r3(ksb), r3(vsb), from_mat).reshape(b * s, SB_WIDTH)
        o_mla = _mla_call(r3(qf), r3(kf), r3(vm)).reshape(b * s, MLA_WIDTH)
        x2 = _out_call(x2, o_sb, o_mla, gin, w_in_bf, w["wg"], b_gate[l][None, :], w_o_sb[l].astype(BF16),
                       w_o_mla[l].astype(BF16), w_out[l].astype(BF16), norm_f_g[None, :],
                       final_norm=(l == depth - 1))
    return x2.reshape(b, s, d)
```

```python
import functools

import jax
import jax.numpy as jnp
import numpy as np
from jax import lax
from jax.experimental import pallas as pl
from jax.experimental.pallas import tpu as pltpu

EPS = 1e-6
CHUNK = 64

SB_HEADS = 8
SB_HEAD_DIM = 64
SB_WIDTH = SB_HEADS * SB_HEAD_DIM

MLA_HEADS = 8
MLA_NOPE_DIM = 64
MLA_ROPE_DIM = 32
MLA_V_DIM = 64
MLA_Q_LORA = 384
MLA_KV_LORA = 256
MLA_QK_DIM = MLA_NOPE_DIM + MLA_ROPE_DIM
MLA_WIDTH = MLA_HEADS * MLA_V_DIM
ROPE_THETA = 10000.0

LANES = 128
ATT_BLOCK = 512
SB_BLOCK = 256
SB_DEAD_LOG2 = 160.0
LOG2_E = float(np.log2(np.e))
HEADS_PER_STEP = 4
V_ROWS = 80
TOKEN_TILE = 512
VMEM_LIMIT_BYTES = 56 * 1024 * 1024

F32 = jnp.float32
BF16 = jnp.bfloat16


def _dot(a, b):
    return jnp.dot(a, b, preferred_element_type=F32)


def _rms(x, g):
    inv = lax.rsqrt(jnp.mean(x * x, axis=-1, keepdims=True) + EPS)
    return (x * inv) * g


def _proj_kernel(x_ref, gin_ref, wqkv_ref, wc_ref, wckv_ref, qng_ref, kvng_ref, wq_ref, wkn_ref, wv_ref,
                 cos_ref, sin_ref, qsb_ref, ksb_ref, vsb_ref, qf_ref, kf_ref, vm_ref):
    h = _rms(x_ref[...], gin_ref[...]).astype(BF16)

    c = _dot(h, wc_ref[...])
    ckv = _dot(h, wckv_ref[...])
    qsb_ref[...] = (_dot(h, wqkv_ref[:, :SB_WIDTH]) * (SB_HEAD_DIM ** -0.5 * LOG2_E)).astype(BF16)
    cq = _rms(c[:, :MLA_Q_LORA], qng_ref[...]).astype(BF16)
    ckv = _rms(ckv, kvng_ref[...]).astype(BF16)

    def rope(v, cos, sin_lo, sin_hi):
        return (v * cos + pltpu.roll(v, LANES - MLA_ROPE_DIM // 2, 1) * sin_lo
                + pltpu.roll(v, MLA_ROPE_DIM // 2, 1) * sin_hi)

    lane = lax.broadcasted_iota(jnp.int32, cos_ref.shape, 1)
    ck_t, sin_t = cos_ref[...], sin_ref[...]
    sk_lo = jnp.where(lane < MLA_NOPE_DIM + MLA_ROPE_DIM // 2, sin_t, 0.0)
    sk_hi = sin_t - sk_lo
    scale = MLA_QK_DIM ** -0.5 * np.log2(np.e)
    cq_t = jnp.where(lane < MLA_NOPE_DIM, scale, ck_t * scale)
    sq_lo, sq_hi = sk_lo * scale, sk_hi * scale

    k_rope = rope(c[:, MLA_Q_LORA:], ck_t, sk_lo, sk_hi)
    q = _dot(cq, wq_ref[...])
    kn = _dot(ckv, wkn_ref[...])
    vm_ref[...] = _dot(ckv, wv_ref[...]).astype(BF16)
    ksb_ref[...] = _dot(h, wqkv_ref[:, SB_WIDTH:2 * SB_WIDTH]).astype(BF16)
    vsb_ref[...] = _dot(h, wqkv_ref[:, 2 * SB_WIDTH:]).astype(BF16)
    for hd in range(MLA_HEADS):
        sl = slice(hd * LANES, (hd + 1) * LANES)
        qf_ref[:, sl] = rope(q[:, sl], cq_t, sq_lo, sq_hi).astype(BF16)
        kf_ref[:, sl] = (kn[:, sl] + k_rope).astype(BF16)


def _proj_call(x2, gin, w_in_bf, wts, tabs):
    n, d = x2.shape
    s = tabs[0].shape[0]
    tm = min(TOKEN_TILE, s)
    row = lambda i: (i, 0)
    fix = lambda i: (0, 0)
    w_specs = [pl.BlockSpec(w.shape, fix) for w in wts]
    t_specs = [pl.BlockSpec((tm, LANES), lambda i: (i % (s // tm), 0)) for _ in tabs]
    outs = [(SB_WIDTH, BF16)] * 3 + [(MLA_HEADS * LANES, BF16)] * 2 + [(MLA_WIDTH, BF16)]
    return pl.pallas_call(
        _proj_kernel,
        grid=(n // tm,),
        in_specs=[pl.BlockSpec((tm, d), row), pl.BlockSpec(gin.shape, fix),
                  pl.BlockSpec((d, 3 * SB_WIDTH), fix)] + w_specs + t_specs,
        out_specs=[pl.BlockSpec((tm, w), row) for w, _ in outs],
        out_shape=[jax.ShapeDtypeStruct((n, w), dt) for w, dt in outs],
        compiler_params=pltpu.CompilerParams(
            dimension_semantics=("arbitrary",), vmem_limit_bytes=VMEM_LIMIT_BYTES),
        name="proj",
    )(x2, gin, w_in_bf, *wts, *tabs)


def _transpose_v(v_ref, vt_ref, nblk, blk):
    def body(j, carry):
        rows = v_ref[pl.ds(pl.multiple_of(j * blk, blk), blk), :]
        vt_ref[j] = rows.astype(F32).T.astype(BF16)
        return carry

    lax.fori_loop(0, nblk, body, 0)


def _sb_kernel(q_ref, k_ref, v_ref, u_ref, o_ref, vt_ref, acc_ref, r_ref, *, nblk):
    t = SB_BLOCK
    qi = pl.program_id(1)

    @pl.when(qi == 0)
    def _():
        _transpose_v(v_ref, vt_ref, nblk, t)

    first = lax.broadcasted_iota(jnp.int32, (LANES, t), 0) < SB_HEAD_DIM
    q_heads = []
    for p in range(SB_HEADS // 2):
        q_t = q_ref[:, p * LANES:(p + 1) * LANES].astype(F32).T
        q_heads += [jnp.where(first, q_t, 0.0).astype(BF16), jnp.where(first, 0.0, q_t).astype(BF16)]

    heads = range(SB_HEADS)

    def scores(j):
        k_blk = k_ref[pl.ds(pl.multiple_of(j * t, t), t), :]
        return [_dot(k_blk[:, (hd // 2) * LANES:(hd // 2 + 1) * LANES], q_heads[hd]) for hd in heads]

    def sums(z, strict):
        incl = []
        for hd in heads:
            neg_log_1m = jnp.maximum(z[hd], 0.0) + jnp.log(1.0 + jnp.exp2(-jnp.abs(z[hd]))) * LOG2_E
            if strict is not None:
                neg_log_1m = jnp.where(strict, neg_log_1m, 0.0)
            incl.append(_dot(u_ref[...], neg_log_1m.astype(BF16)))
        return incl

    def accumulate(j, z, incl, r, strict):
        for hd in heads:
            w = jnp.exp2(z[hd] - incl[hd])
            if strict is not None:
                w = jnp.where(strict, w, 0.0)
            vt_blk = vt_ref[j, hd * SB_HEAD_DIM:(hd + 1) * SB_HEAD_DIM, :]
            pv = _dot(vt_blk, w.astype(BF16))
            acc_ref[hd] += pv if r is None else jnp.exp2(r[hd]) * pv
        return tuple(-incl[hd][0:1, :] if r is None else r[hd] - incl[hd][0:1, :] for hd in heads)

    def r_max(r):
        return functools.reduce(jnp.maximum, [jnp.max(x) for x in r])

    acc_ref[...] = jnp.zeros_like(acc_ref)
    strict = lax.broadcasted_iota(jnp.int32, (t, t), 0) < lax.broadcasted_iota(jnp.int32, (t, t), 1)

    @pl.when(qi == 0)
    def _():
        z = scores(qi)
        r = accumulate(qi, z, sums(z, strict), None, strict)
        for hd in heads:
            r_ref[hd] = r[hd]

    @pl.when(qi > 0)
    def _():
        z0, z1 = scores(qi), scores(qi - 1)
        s0, s1 = sums(z0, strict), sums(z1, None)
        r = accumulate(qi, z0, s0, None, strict)
        r = accumulate(qi - 1, z1, s1, r, None)
        for hd in heads:
            r_ref[hd] = r[hd]

    def cond(c):
        j, rmax, _ = c
        return jnp.logical_and(j >= 0, rmax > -SB_DEAD_LOG2)

    def body(c):
        j, _, r = c
        z = scores(j)
        r = accumulate(j, z, sums(z, None), r, None)
        return j - 1, r_max(r), r

    r = tuple(r_ref[hd] for hd in heads)
    lax.while_loop(cond, body, (qi - 2, r_max(r), r))


    o_ref[...] = jnp.concatenate([acc_ref[hd] for hd in range(SB_HEADS)], axis=0).T


def _sb_call(q, k, v, u):
    b, s, w = q.shape
    t = SB_BLOCK
    nblk = s // t
    resident = lambda: pl.BlockSpec((None, s, w), lambda bi, qi: (bi, 0, 0), pipeline_mode=pl.Buffered(1))
    return pl.pallas_call(
        functools.partial(_sb_kernel, nblk=nblk),
        grid=(b, nblk),
        in_specs=[
            pl.BlockSpec((None, t, w), lambda bi, qi: (bi, qi, 0)),
            resident(),
            resident(),
            pl.BlockSpec(u.shape, lambda bi, qi: (0, 0)),
        ],
        out_specs=pl.BlockSpec((None, t, w), lambda bi, qi: (bi, qi, 0)),
        out_shape=jax.ShapeDtypeStruct((b, s, w), F32),
        scratch_shapes=[
            pltpu.VMEM((nblk, w, t), BF16),
            pltpu.VMEM((SB_HEADS, SB_HEAD_DIM, t), F32),
            pltpu.VMEM((SB_HEADS, 1, t), F32),
        ],
        compiler_params=pltpu.CompilerParams(
            dimension_semantics=("arbitrary", "arbitrary"), vmem_limit_bytes=VMEM_LIMIT_BYTES),
        name="sb_attn",
    )(q, k, v, u)


def _mla_kernel(q_ref, k_ref, v_ref, o_ref, vt_ref, acc_ref, m_ref, s0_ref, s1_ref, smax0_ref, smax1_ref, *,
                nblk):
    t = ATT_BLOCK
    qi = pl.program_id(2)

    @pl.when(qi == 0)
    def _():
        ones_rows = (lax.broadcasted_iota(jnp.int32, (V_ROWS - MLA_V_DIM, t), 0) == 0).astype(BF16)

        def body(j, carry):
            rows = v_ref[pl.ds(pl.multiple_of(j * t, t), t), :].astype(F32).T.astype(BF16)
            for hd in range(HEADS_PER_STEP):
                vt_ref[j, hd, :MLA_V_DIM, :] = rows[hd * MLA_V_DIM:(hd + 1) * MLA_V_DIM, :]
                vt_ref[j, hd, MLA_V_DIM:, :] = ones_rows
            return carry

        lax.fori_loop(0, nblk, body, 0)

    q_heads = [q_ref[:, hd * LANES:(hd + 1) * LANES].astype(F32).T.astype(BF16) for hd in range(HEADS_PER_STEP)]

    def scores(hd, j, buf, allowed):
        s_ref, smax_ref = buf
        k_blk = k_ref[pl.ds(pl.multiple_of(j * t, t), t), hd * LANES:(hd + 1) * LANES]
        s = _dot(k_blk, q_heads[hd])
        if allowed is not None:
            s = jnp.where(allowed, s, -jnp.inf)
        s_ref[hd] = s
        smax_ref[hd] = jnp.max(s, axis=0, keepdims=True)

    def update(hd, j, buf):
        s_ref, smax_ref = buf
        m = m_ref[hd]
        m_new = jnp.maximum(m, smax_ref[hd])
        m_ref[hd] = m_new
        p = jnp.exp2(s_ref[hd] - m_new).astype(BF16)
        acc_ref[hd] = jnp.exp2(m - m_new) * acc_ref[hd] + _dot(vt_ref[j, hd], p)

    acc_ref[...] = jnp.zeros_like(acc_ref)
    m_ref[...] = jnp.full_like(m_ref, -jnp.inf)
    key_chunk = lax.broadcasted_iota(jnp.int32, (t, t), 0) // CHUNK
    qry_chunk = lax.broadcasted_iota(jnp.int32, (t, t), 1) // CHUNK
    bufs = ((s0_ref, smax0_ref), (s1_ref, smax1_ref))

    block_of = lambda k: jnp.where(k == 0, qi, k - 1)
    allowed = key_chunk <= qry_chunk
    for hd in range(HEADS_PER_STEP):
        scores(hd, qi, bufs[0], allowed)

    def body(k, carry):
        for parity in range(2):
            @pl.when(k % 2 == parity)
            def _():
                for hd in range(HEADS_PER_STEP):
                    scores(hd, block_of(k + 1), bufs[1 - parity], None)
                    update(hd, block_of(k), bufs[parity])
        return carry

    lax.fori_loop(0, qi, body, 0)

    for parity in range(2):
        @pl.when(qi % 2 == parity)
        def _():
            for hd in range(HEADS_PER_STEP):
                update(hd, block_of(qi), bufs[parity])

    o_ref[...] = jnp.concatenate(
        [acc_ref[hd, :MLA_V_DIM, :] / acc_ref[hd, MLA_V_DIM:MLA_V_DIM + 1, :] for hd in range(HEADS_PER_STEP)],
        axis=0).T


def _mla_call(qf, kf, vm):
    b, s, _ = qf.shape
    t = ATT_BLOCK
    nblk = s // t
    groups = MLA_HEADS // HEADS_PER_STEP
    return pl.pallas_call(
        functools.partial(_mla_kernel, nblk=nblk),
        grid=(b, groups, nblk),
        in_specs=[
            pl.BlockSpec((None, t, HEADS_PER_STEP * LANES), lambda bi, g, qi: (bi, qi, g)),
            pl.BlockSpec((None, s, HEADS_PER_STEP * LANES), lambda bi, g, qi: (bi, 0, g)),
            pl.BlockSpec((None, s, HEADS_PER_STEP * MLA_V_DIM), lambda bi, g, qi: (bi, 0, g)),
        ],
        out_specs=pl.BlockSpec((None, t, HEADS_PER_STEP * MLA_V_DIM), lambda bi, g, qi: (bi, qi, g)),
        out_shape=jax.ShapeDtypeStruct((b, s, MLA_WIDTH), F32),
        scratch_shapes=[
            pltpu.VMEM((nblk, HEADS_PER_STEP, V_ROWS, t), BF16),
            pltpu.VMEM((HEADS_PER_STEP, V_ROWS, t), F32),
            pltpu.VMEM((HEADS_PER_STEP, 1, t), F32),
            pltpu.VMEM((HEADS_PER_STEP, t, t), F32),
            pltpu.VMEM((HEADS_PER_STEP, t, t), F32),
            pltpu.VMEM((HEADS_PER_STEP, 1, t), F32),
            pltpu.VMEM((HEADS_PER_STEP, 1, t), F32),
        ],
        compiler_params=pltpu.CompilerParams(
            dimension_semantics=("arbitrary", "arbitrary", "arbitrary"), vmem_limit_bytes=VMEM_LIMIT_BYTES),
        name="mla_attn",
    )(qf, kf, vm)


def _out_kernel(x_ref, osb_ref, omla_ref, gin_ref, wgsb_ref, wg_ref, bg_ref, wosb_ref, womla_ref, wout_ref,
                gf_ref, o_ref, *, final_norm):
    x = x_ref[...]
    d = x.shape[-1]
    h = _rms(x, gin_ref[...]).astype(BF16)
    a_sb = (osb_ref[...] * jax.nn.silu(_dot(h, wgsb_ref[...]))).astype(BF16)
    a_mla = (omla_ref[...] * jax.nn.silu(_dot(h, wg_ref[:, :MLA_WIDTH]))).astype(BF16)
    g0 = MLA_WIDTH
    g_sb = jax.nn.sigmoid(_dot(h, wg_ref[:, g0:g0 + d]) + bg_ref[:, :d])
    merged = g_sb * _dot(a_sb, wosb_ref[...])
    g_mla = jax.nn.sigmoid(_dot(h, wg_ref[:, g0 + d:g0 + 2 * d]) + bg_ref[:, d:])
    merged = merged + g_mla * _dot(a_mla, womla_ref[...])
    y = x + _dot(merged.astype(BF16), wout_ref[...])
    o_ref[...] = _rms(y, gf_ref[...]) if final_norm else y


def _out_call(x2, osb, omla, gin, w_in_bf, wg, bg, wosb, womla, wout, gf, final_norm):
    n, d = x2.shape
    tm = min(TOKEN_TILE, n)
    row = lambda i: (i, 0)
    fix = lambda i: (0, 0)
    consts = (wg, bg, wosb, womla, wout, gf)
    return pl.pallas_call(
        functools.partial(_out_kernel, final_norm=final_norm),
        grid=(n // tm,),
        in_specs=[pl.BlockSpec((tm, d), row), pl.BlockSpec((tm, SB_WIDTH), row), pl.BlockSpec((tm, MLA_WIDTH), row),
                  pl.BlockSpec(gin.shape, fix),
                  pl.BlockSpec((d, SB_WIDTH), lambda i: (0, 3))]
                 + [pl.BlockSpec(c.shape, fix) for c in consts],
        out_specs=pl.BlockSpec((tm, d), row),
        out_shape=jax.ShapeDtypeStruct((n, d), F32),
        compiler_params=pltpu.CompilerParams(
            dimension_semantics=("arbitrary",), vmem_limit_bytes=VMEM_LIMIT_BYTES),
        name="out",
    )(x2, osb, omla, gin, w_in_bf, *consts)


def _rope_tables(s):
    half = MLA_ROPE_DIM // 2
    inv_freq = ROPE_THETA ** (-np.arange(half, dtype=np.float64) / half)
    ang = np.arange(s, dtype=np.float64)[:, None] * inv_freq[None, :]
    cos, sin = np.cos(ang), np.sin(ang)
    lanes = lambda lo, hi: np.concatenate(
        [np.zeros((s, MLA_NOPE_DIM)), lo, hi, np.zeros((s, LANES - MLA_QK_DIM))], axis=1).astype(np.float32)
    return jnp.asarray(lanes(cos, cos)), jnp.asarray(lanes(-sin, sin))


def _layer_weights(w_in_bf, w_q_up, w_kv_up):
    d = w_in_bf.shape[0]
    o_cq = 4 * SB_WIDTH
    o_ckv = o_cq + MLA_Q_LORA
    o_kr = o_ckv + MLA_KV_LORA
    o_gm = o_kr + MLA_ROPE_DIM
    zeros = lambda r, w: jnp.zeros((r, w), BF16)
    wc = jnp.concatenate([w_in_bf[:, o_cq:o_ckv], zeros(d, MLA_NOPE_DIM), w_in_bf[:, o_kr:o_gm],
                          zeros(d, LANES - MLA_QK_DIM)], axis=1)
    wckv = w_in_bf[:, o_ckv:o_kr]
    wg = w_in_bf[:, o_gm:]

    rq = w_q_up.shape[0]
    wq = w_q_up.astype(BF16).reshape(rq, MLA_HEADS, MLA_QK_DIM)
    wq = jnp.concatenate([wq, jnp.zeros((rq, MLA_HEADS, LANES - MLA_QK_DIM), BF16)], axis=-1)
    rkv = w_kv_up.shape[0]
    wkv = w_kv_up.astype(BF16).reshape(rkv, MLA_HEADS, MLA_NOPE_DIM + MLA_V_DIM)
    wkn = jnp.concatenate([wkv[..., :MLA_NOPE_DIM], jnp.zeros((rkv, MLA_HEADS, LANES - MLA_NOPE_DIM), BF16)], axis=-1)
    wv = wkv[..., MLA_NOPE_DIM:]
    return dict(wc=wc, wckv=wckv, wg=wg, wq=wq.reshape(rq, MLA_HEADS * LANES),
                wkn=wkn.reshape(rkv, MLA_HEADS * LANES), wv=wv.reshape(rkv, MLA_WIDTH))


def kernel(x, norm_in_g, w_in, b_gate, q_norm_g, w_q_up, kv_norm_g, w_kv_up, w_o_sb, w_o_mla, w_out, norm_f_g):
    b, s, d = x.shape
    depth = w_in.shape[0]
    assert s % ATT_BLOCK == 0 and s % SB_BLOCK == 0 and s % min(TOKEN_TILE, s) == 0
    tabs = _rope_tables(s)
    idx = jnp.arange(SB_BLOCK)
    from_mat = (idx[None, :] >= idx[:, None]).astype(BF16)

    x2 = x.reshape(b * s, d)
    for l in range(depth):
        w_in_bf = w_in[l].astype(BF16)
        w = _layer_weights(w_in_bf, w_q_up[l], w_kv_up[l])
        gin = norm_in_g[l][None, :]
        qsb, ksb, vsb, qf, kf, vm = _proj_call(
            x2, gin, w_in_bf,
            (w["wc"], w["wckv"], q_norm_g[l][None, :], kv_norm_g[l][None, :], w["wq"], w["wkn"], w["wv"]),
            tabs)
        r3 = lambda a: a.reshape(b, s, a.shape[-1])
        o_sb = _sb_call(r3(qsb), r3(ksb), r3(vsb), from_mat).reshape(b * s, SB_WIDTH)
        o_mla = _mla_call(r3(qf), r3(kf), r3(vm)).reshape(b * s, MLA_WIDTH)
        x2 = _out_call(x2, o_sb, o_mla, gin, w_in_bf, w["wg"], b_gate[l][None, :], w_o_sb[l].astype(BF16),
                       w_o_mla[l].astype(BF16), w_out[l].astype(BF16), norm_f_g[None, :],
                       final_norm=(l == depth - 1))
    return x2.reshape(b, s, d)
```

```python
import functools

import jax
import jax.numpy as jnp
import numpy as np
from jax import lax
from jax.experimental import pallas as pl
from jax.experimental.pallas import tpu as pltpu

EPS = 1e-6
CHUNK = 64

SB_HEADS = 8
SB_HEAD_DIM = 64
SB_WIDTH = SB_HEADS * SB_HEAD_DIM

MLA_HEADS = 8
MLA_NOPE_DIM = 64
MLA_ROPE_DIM = 32
MLA_V_DIM = 64
MLA_Q_LORA = 384
MLA_KV_LORA = 256
MLA_QK_DIM = MLA_NOPE_DIM + MLA_ROPE_DIM
MLA_WIDTH = MLA_HEADS * MLA_V_DIM
ROPE_THETA = 10000.0

LANES = 128
ATT_BLOCK = 512
SB_BLOCK = 256
SB_DEAD_LOG2 = 160.0
LOG2_E = float(np.log2(np.e))
HEADS_PER_STEP = 8
V_ROWS = 80
TOKEN_TILE = 512
VMEM_LIMIT_BYTES = 56 * 1024 * 1024

F32 = jnp.float32
BF16 = jnp.bfloat16


def _dot(a, b):
    return jnp.dot(a, b, preferred_element_type=F32)


def _rms(x, g):
    inv = lax.rsqrt(jnp.mean(x * x, axis=-1, keepdims=True) + EPS)
    return (x * inv) * g


def _proj_kernel(x_ref, gin_ref, wqkv_ref, wc_ref, wckv_ref, qng_ref, kvng_ref, wq_ref, wkn_ref, wv_ref,
                 cos_ref, sin_ref, qsb_ref, ksb_ref, vsb_ref, qf_ref, kf_ref, vt_ref):
    h = _rms(x_ref[...], gin_ref[...]).astype(BF16)

    c = _dot(h, wc_ref[...])
    ckv = _dot(h, wckv_ref[...])
    qsb_ref[...] = (_dot(h, wqkv_ref[:, :SB_WIDTH]) * (SB_HEAD_DIM ** -0.5 * LOG2_E)).astype(BF16)
    cq = _rms(c[:, :MLA_Q_LORA], qng_ref[...]).astype(BF16)
    ckv = _rms(ckv, kvng_ref[...]).astype(BF16)

    def rope(v, cos, sin_lo, sin_hi):
        return (v * cos + pltpu.roll(v, LANES - MLA_ROPE_DIM // 2, 1) * sin_lo
                + pltpu.roll(v, MLA_ROPE_DIM // 2, 1) * sin_hi)

    lane = lax.broadcasted_iota(jnp.int32, cos_ref.shape, 1)
    ck_t, sin_t = cos_ref[...], sin_ref[...]
    sk_lo = jnp.where(lane < MLA_NOPE_DIM + MLA_ROPE_DIM // 2, sin_t, 0.0)
    sk_hi = sin_t - sk_lo
    scale = MLA_QK_DIM ** -0.5 * np.log2(np.e)
    cq_t = jnp.where(lane < MLA_NOPE_DIM, scale, ck_t * scale)
    sq_lo, sq_hi = sk_lo * scale, sk_hi * scale

    k_rope = rope(c[:, MLA_Q_LORA:], ck_t, sk_lo, sk_hi)
    q = _dot(cq, wq_ref[...])
    kn = _dot(ckv, wkn_ref[...])
    v_t = lax.dot_general(wv_ref[...], ckv, (((1,), (1,)), ((), ())), preferred_element_type=F32)
    pad_rows = (lax.broadcasted_iota(jnp.int32, (V_ROWS - MLA_V_DIM, v_t.shape[1]), 0) == 0).astype(BF16)
    for hd in range(MLA_HEADS):
        vt_ref[hd, :MLA_V_DIM, :] = v_t[hd * MLA_V_DIM:(hd + 1) * MLA_V_DIM, :].astype(BF16)
        vt_ref[hd, MLA_V_DIM:, :] = pad_rows
    ksb_ref[...] = _dot(h, wqkv_ref[:, SB_WIDTH:2 * SB_WIDTH]).astype(BF16)
    vsb_ref[...] = _dot(h, wqkv_ref[:, 2 * SB_WIDTH:]).astype(BF16)
    for hd in range(MLA_HEADS):
        sl = slice(hd * LANES, (hd + 1) * LANES)
        qf_ref[:, sl] = rope(q[:, sl], cq_t, sq_lo, sq_hi).astype(BF16)
        kf_ref[:, sl] = (kn[:, sl] + k_rope).astype(BF16)


def _proj_call(x2, gin, w_in_bf, wts, tabs):
    n, d = x2.shape
    s = tabs[0].shape[0]
    tm = min(TOKEN_TILE, s)
    row = lambda i: (i, 0)
    fix = lambda i: (0, 0)
    w_specs = [pl.BlockSpec(w.shape, fix) for w in wts]
    t_specs = [pl.BlockSpec((tm, LANES), lambda i: (i % (s // tm), 0)) for _ in tabs]
    outs = [(SB_WIDTH, BF16)] * 3 + [(MLA_HEADS * LANES, BF16)] * 2
    vt_spec = pl.BlockSpec((None, MLA_HEADS, V_ROWS, tm), lambda i: (i, 0, 0, 0))
    vt_shape = jax.ShapeDtypeStruct((n // tm, MLA_HEADS, V_ROWS, tm), BF16)
    return pl.pallas_call(
        _proj_kernel,
        grid=(n // tm,),
        in_specs=[pl.BlockSpec((tm, d), row), pl.BlockSpec(gin.shape, fix),
                  pl.BlockSpec((d, 3 * SB_WIDTH), fix)] + w_specs + t_specs,
        out_specs=[pl.BlockSpec((tm, w), row) for w, _ in outs] + [vt_spec],
        out_shape=[jax.ShapeDtypeStruct((n, w), dt) for w, dt in outs] + [vt_shape],
        compiler_params=pltpu.CompilerParams(
            dimension_semantics=("arbitrary",), vmem_limit_bytes=VMEM_LIMIT_BYTES),
        name="proj",
    )(x2, gin, w_in_bf, *wts, *tabs)


def _transpose_v(v_ref, vt_ref, nblk, blk):
    def body(j, carry):
        rows = v_ref[pl.ds(pl.multiple_of(j * blk, blk), blk), :]
        vt_ref[j] = rows.astype(F32).T.astype(BF16)
        return carry

    lax.fori_loop(0, nblk, body, 0)


def _sb_kernel(q_ref, k_ref, v_ref, u_ref, o_ref, vt_ref, acc_ref, *, nblk):
    t = SB_BLOCK
    qi = pl.program_id(1)

    @pl.when(qi == 0)
    def _():
        _transpose_v(v_ref, vt_ref, nblk, t)

    first = lax.broadcasted_iota(jnp.int32, (LANES, t), 0) < SB_HEAD_DIM
    q_heads = []
    for p in range(SB_HEADS // 2):
        q_t = q_ref[:, p * LANES:(p + 1) * LANES].astype(F32).T
        q_heads += [jnp.where(first, q_t, 0.0).astype(BF16), jnp.where(first, 0.0, q_t).astype(BF16)]

    def block(j, r, strict):
        k_blk = k_ref[pl.ds(pl.multiple_of(j * t, t), t), :]
        heads = range(SB_HEADS)
        z = [_dot(k_blk[:, (hd // 2) * LANES:(hd // 2 + 1) * LANES], q_heads[hd]) for hd in heads]
        incl = []
        for hd in heads:
            neg_log_1m = jnp.maximum(z[hd], 0.0) + jnp.log(1.0 + jnp.exp2(-jnp.abs(z[hd]))) * LOG2_E
            if strict is not None:
                neg_log_1m = jnp.where(strict, neg_log_1m, 0.0)
            incl.append(_dot(u_ref[...], neg_log_1m.astype(BF16)))
        for hd in heads:
            w = jnp.exp2(z[hd] - incl[hd] + r[hd])
            if strict is not None:
                w = jnp.where(strict, w, 0.0)
            vt_blk = vt_ref[j, hd * SB_HEAD_DIM:(hd + 1) * SB_HEAD_DIM, :]
            acc_ref[hd] += _dot(vt_blk, w.astype(BF16))
        return tuple(r[hd] - incl[hd][0:1, :] for hd in heads)

    def r_max(r):
        return functools.reduce(jnp.maximum, [jnp.max(x) for x in r])

    acc_ref[...] = jnp.zeros_like(acc_ref)
    zero = jnp.zeros((1, t), F32)
    key_pos = lax.broadcasted_iota(jnp.int32, (t, t), 0)
    qry_pos = lax.broadcasted_iota(jnp.int32, (t, t), 1)
    r = block(qi, (zero,) * SB_HEADS, key_pos < qry_pos)

    def cond(c):
        j, rmax, _ = c
        return jnp.logical_and(j >= 0, rmax > -SB_DEAD_LOG2)

    def body(c):
        j, _, r = c
        r = block(j, r, None)
        return j - 1, r_max(r), r

    lax.while_loop(cond, body, (qi - 1, r_max(r), r))

    o_ref[...] = jnp.concatenate([acc_ref[hd] for hd in range(SB_HEADS)], axis=0).T


def _sb_call(q, k, v, u):
    b, s, w = q.shape
    t = SB_BLOCK
    nblk = s // t
    resident = lambda: pl.BlockSpec((None, s, w), lambda bi, qi: (bi, 0, 0), pipeline_mode=pl.Buffered(1))
    return pl.pallas_call(
        functools.partial(_sb_kernel, nblk=nblk),
        grid=(b, nblk),
        in_specs=[
            pl.BlockSpec((None, t, w), lambda bi, qi: (bi, qi, 0)),
            resident(),
            resident(),
            pl.BlockSpec(u.shape, lambda bi, qi: (0, 0)),
        ],
        out_specs=pl.BlockSpec((None, t, w), lambda bi, qi: (bi, qi, 0)),
        out_shape=jax.ShapeDtypeStruct((b, s, w), F32),
        scratch_shapes=[
            pltpu.VMEM((nblk, w, t), BF16),
            pltpu.VMEM((SB_HEADS, SB_HEAD_DIM, t), F32),
        ],
        compiler_params=pltpu.CompilerParams(
            dimension_semantics=("arbitrary", "arbitrary"), vmem_limit_bytes=VMEM_LIMIT_BYTES),
        name="sb_attn",
    )(q, k, v, u)


def _mla_kernel(q_ref, k_ref, vt_ref, o_ref, acc_ref, m_ref, s0_ref, s1_ref, smax0_ref, smax1_ref):
    t = ATT_BLOCK
    qi = pl.program_id(1)

    q_heads = [q_ref[:, hd * LANES:(hd + 1) * LANES].astype(F32).T.astype(BF16) for hd in range(HEADS_PER_STEP)]

    def scores(hd, j, buf, allowed):
        s_ref, smax_ref = buf
        k_blk = k_ref[pl.ds(pl.multiple_of(j * t, t), t), hd * LANES:(hd + 1) * LANES]
        s = _dot(k_blk, q_heads[hd])
        if allowed is not None:
            s = jnp.where(allowed, s, -jnp.inf)
        s_ref[hd] = s
        smax_ref[hd] = jnp.max(s, axis=0, keepdims=True)

    def update(hd, j, buf):
        s_ref, smax_ref = buf
        m = m_ref[hd]
        m_new = jnp.maximum(m, smax_ref[hd])
        m_ref[hd] = m_new
        p = jnp.exp2(s_ref[hd] - m_new).astype(BF16)
        acc_ref[hd] = jnp.exp2(m - m_new) * acc_ref[hd] + _dot(vt_ref[j, hd], p)

    acc_ref[...] = jnp.zeros_like(acc_ref)
    m_ref[...] = jnp.full_like(m_ref, -jnp.inf)
    key_chunk = lax.broadcasted_iota(jnp.int32, (t, t), 0) // CHUNK
    qry_chunk = lax.broadcasted_iota(jnp.int32, (t, t), 1) // CHUNK
    bufs = ((s0_ref, smax0_ref), (s1_ref, smax1_ref))

    block_of = lambda k: jnp.where(k == 0, qi, k - 1)
    allowed = key_chunk <= qry_chunk
    for hd in range(HEADS_PER_STEP):
        scores(hd, qi, bufs[0], allowed)

    def body(k, carry):
        for parity in range(2):
            @pl.when(k % 2 == parity)
            def _():
                for hd in range(HEADS_PER_STEP):
                    scores(hd, block_of(k + 1), bufs[1 - parity], None)
                    update(hd, block_of(k), bufs[parity])
        return carry

    lax.fori_loop(0, qi, body, 0)

    for parity in range(2):
        @pl.when(qi % 2 == parity)
        def _():
            for hd in range(HEADS_PER_STEP):
                update(hd, block_of(qi), bufs[parity])

    o_ref[...] = jnp.concatenate(
        [acc_ref[hd, :MLA_V_DIM, :] / acc_ref[hd, MLA_V_DIM:MLA_V_DIM + 1, :] for hd in range(HEADS_PER_STEP)],
        axis=0).T


def _mla_call(qf, kf, vt):
    b, s, w = qf.shape
    t = ATT_BLOCK
    nblk = s // t
    assert vt.shape == (b, nblk, MLA_HEADS, V_ROWS, t) and HEADS_PER_STEP == MLA_HEADS
    resident = lambda shape: pl.BlockSpec((None,) + shape, lambda bi, qi: (bi,) + (0,) * len(shape),
                                          pipeline_mode=pl.Buffered(1))
    return pl.pallas_call(
        _mla_kernel,
        grid=(b, nblk),
        in_specs=[
            pl.BlockSpec((None, t, w), lambda bi, qi: (bi, qi, 0)),
            resident((s, w)),
            resident((nblk, MLA_HEADS, V_ROWS, t)),
        ],
        out_specs=pl.BlockSpec((None, t, MLA_WIDTH), lambda bi, qi: (bi, qi, 0)),
        out_shape=jax.ShapeDtypeStruct((b, s, MLA_WIDTH), F32),
        scratch_shapes=[
            pltpu.VMEM((HEADS_PER_STEP, V_ROWS, t), F32),
            pltpu.VMEM((HEADS_PER_STEP, 1, t), F32),
            pltpu.VMEM((HEADS_PER_STEP, t, t), F32),
            pltpu.VMEM((HEADS_PER_STEP, t, t), F32),
            pltpu.VMEM((HEADS_PER_STEP, 1, t), F32),
            pltpu.VMEM((HEADS_PER_STEP, 1, t), F32),
        ],
        compiler_params=pltpu.CompilerParams(
            dimension_semantics=("arbitrary", "arbitrary"), vmem_limit_bytes=VMEM_LIMIT_BYTES),
        name="mla_attn",
    )(qf, kf, vt)


def _out_kernel(x_ref, osb_ref, omla_ref, gin_ref, wgsb_ref, wg_ref, bg_ref, wosb_ref, womla_ref, wout_ref,
                gf_ref, o_ref, *, final_norm):
    x = x_ref[...]
    d = x.shape[-1]
    h = _rms(x, gin_ref[...]).astype(BF16)
    a_sb = (osb_ref[...] * jax.nn.silu(_dot(h, wgsb_ref[...]))).astype(BF16)
    a_mla = (omla_ref[...] * jax.nn.silu(_dot(h, wg_ref[:, :MLA_WIDTH]))).astype(BF16)
    g0 = MLA_WIDTH
    g_sb = jax.nn.sigmoid(_dot(h, wg_ref[:, g0:g0 + d]) + bg_ref[:, :d])
    merged = g_sb * _dot(a_sb, wosb_ref[...])
    g_mla = jax.nn.sigmoid(_dot(h, wg_ref[:, g0 + d:g0 + 2 * d]) + bg_ref[:, d:])
    merged = merged + g_mla * _dot(a_mla, womla_ref[...])
    y = x + _dot(merged.astype(BF16), wout_ref[...])
    o_ref[...] = _rms(y, gf_ref[...]) if final_norm else y


def _out_call(x2, osb, omla, gin, w_in_bf, wg, bg, wosb, womla, wout, gf, final_norm):
    n, d = x2.shape
    tm = min(TOKEN_TILE, n)
    row = lambda i: (i, 0)
    fix = lambda i: (0, 0)
    consts = (wg, bg, wosb, womla, wout, gf)
    return pl.pallas_call(
        functools.partial(_out_kernel, final_norm=final_norm),
        grid=(n // tm,),
        in_specs=[pl.BlockSpec((tm, d), row), pl.BlockSpec((tm, SB_WIDTH), row), pl.BlockSpec((tm, MLA_WIDTH), row),
                  pl.BlockSpec(gin.shape, fix),
                  pl.BlockSpec((d, SB_WIDTH), lambda i: (0, 3))]
                 + [pl.BlockSpec(c.shape, fix) for c in consts],
        out_specs=pl.BlockSpec((tm, d), row),
        out_shape=jax.ShapeDtypeStruct((n, d), F32),
        compiler_params=pltpu.CompilerParams(
            dimension_semantics=("arbitrary",), vmem_limit_bytes=VMEM_LIMIT_BYTES),
        name="out",
    )(x2, osb, omla, gin, w_in_bf, *consts)


def _rope_tables(s):
    half = MLA_ROPE_DIM // 2
    inv_freq = ROPE_THETA ** (-np.arange(half, dtype=np.float64) / half)
    ang = np.arange(s, dtype=np.float64)[:, None] * inv_freq[None, :]
    cos, sin = np.cos(ang), np.sin(ang)
    lanes = lambda lo, hi: np.concatenate(
        [np.zeros((s, MLA_NOPE_DIM)), lo, hi, np.zeros((s, LANES - MLA_QK_DIM))], axis=1).astype(np.float32)
    return jnp.asarray(lanes(cos, cos)), jnp.asarray(lanes(-sin, sin))


def _layer_weights(w_in_bf, w_q_up, w_kv_up):
    d = w_in_bf.shape[0]
    o_cq = 4 * SB_WIDTH
    o_ckv = o_cq + MLA_Q_LORA
    o_kr = o_ckv + MLA_KV_LORA
    o_gm = o_kr + MLA_ROPE_DIM
    zeros = lambda r, w: jnp.zeros((r, w), BF16)
    wc = jnp.concatenate([w_in_bf[:, o_cq:o_ckv], zeros(d, MLA_NOPE_DIM), w_in_bf[:, o_kr:o_gm],
                          zeros(d, LANES - MLA_QK_DIM)], axis=1)
    wckv = w_in_bf[:, o_ckv:o_kr]
    wg = w_in_bf[:, o_gm:]

    rq = w_q_up.shape[0]
    wq = w_q_up.astype(BF16).reshape(rq, MLA_HEADS, MLA_QK_DIM)
    wq = jnp.concatenate([wq, jnp.zeros((rq, MLA_HEADS, LANES - MLA_QK_DIM), BF16)], axis=-1)
    rkv = w_kv_up.shape[0]
    wkv = w_kv_up.astype(BF16).reshape(rkv, MLA_HEADS, MLA_NOPE_DIM + MLA_V_DIM)
    wkn = jnp.concatenate([wkv[..., :MLA_NOPE_DIM], jnp.zeros((rkv, MLA_HEADS, LANES - MLA_NOPE_DIM), BF16)], axis=-1)
    wv_t = wkv[..., MLA_NOPE_DIM:].reshape(rkv, MLA_WIDTH).T
    return dict(wc=wc, wckv=wckv, wg=wg, wq=wq.reshape(rq, MLA_HEADS * LANES),
                wkn=wkn.reshape(rkv, MLA_HEADS * LANES), wv=wv_t)


def kernel(x, norm_in_g, w_in, b_gate, q_norm_g, w_q_up, kv_norm_g, w_kv_up, w_o_sb, w_o_mla, w_out, norm_f_g):
    b, s, d = x.shape
    depth = w_in.shape[0]
    assert s % ATT_BLOCK == 0 and s % SB_BLOCK == 0 and TOKEN_TILE == ATT_BLOCK
    tabs = _rope_tables(s)
    idx = jnp.arange(SB_BLOCK)
    from_mat = (idx[None, :] >= idx[:, None]).astype(BF16)

    x2 = x.reshape(b * s, d)
    for l in range(depth):
        w_in_bf = w_in[l].astype(BF16)
        w = _layer_weights(w_in_bf, w_q_up[l], w_kv_up[l])
        gin = norm_in_g[l][None, :]
        qsb, ksb, vsb, qf, kf, vt = _proj_call(
            x2, gin, w_in_bf,
            (w["wc"], w["wckv"], q_norm_g[l][None, :], kv_norm_g[l][None, :], w["wq"], w["wkn"], w["wv"]),
            tabs)
        r3 = lambda a: a.reshape(b, s, a.shape[-1])
        o_sb = _sb_call(r3(qsb), r3(ksb), r3(vsb), from_mat).reshape(b * s, SB_WIDTH)
        o_mla = _mla_call(r3(qf), r3(kf), vt.reshape((b, s // ATT_BLOCK) + vt.shape[1:])).reshape(b * s, MLA_WIDTH)
        x2 = _out_call(x2, o_sb, o_mla, gin, w_in_bf, w["wg"], b_gate[l][None, :], w_o_sb[l].astype(BF16),
                       w_o_mla[l].astype(BF16), w_out[l].astype(BF16), norm_f_g[None, :],
                       final_norm=(l == depth - 1))
    return x2.reshape(b, s, d)
```

```python
import functools

import jax
import jax.numpy as jnp
import numpy as np
from jax import lax
from jax.experimental import pallas as pl
from jax.experimental.pallas import tpu as pltpu

EPS = 1e-6
CHUNK = 64

SB_HEADS = 8
SB_HEAD_DIM = 64
SB_WIDTH = SB_HEADS * SB_HEAD_DIM

MLA_HEADS = 8
MLA_NOPE_DIM = 64
MLA_ROPE_DIM = 32
MLA_V_DIM = 64
MLA_Q_LORA = 384
MLA_KV_LORA = 256
MLA_QK_DIM = MLA_NOPE_DIM + MLA_ROPE_DIM
MLA_WIDTH = MLA_HEADS * MLA_V_DIM
ROPE_THETA = 10000.0

LANES = 128
ATT_BLOCK = 512
SB_BLOCK = 256
SB_DEAD_LOG2 = 160.0
LOG2_E = float(np.log2(np.e))
HEADS_PER_STEP = 8
V_ROWS = 80
TOKEN_TILE = 512
VMEM_LIMIT_BYTES = 56 * 1024 * 1024

F32 = jnp.float32
BF16 = jnp.bfloat16
NT_DIMS = (((1,), (1,)), ((), ()))


def _dot(a, b):
    return jnp.dot(a, b, preferred_element_type=F32)


def _rms(x, g):
    inv = lax.rsqrt(jnp.mean(x * x, axis=-1, keepdims=True) + EPS)
    return (x * inv) * g


def _proj_kernel(x_ref, gin_ref, wqk32_ref, wvsb_ref, wc_ref, wckv_ref, qng_ref, kvng_ref, wq_ref, wkn_ref, wv_ref,
                 cos_ref, sin_ref, qsb_ref, ksb_ref, vsbt_ref, qf_ref, kf_ref, vt_ref, wqk_ref):
    @pl.when(pl.program_id(0) == 0)
    def _():
        wqk_ref[...] = wqk32_ref[...].astype(BF16)

    h = _rms(x_ref[...], gin_ref[...]).astype(BF16)

    c = _dot(h, wc_ref[...])
    ckv = _dot(h, wckv_ref[...])
    qsb_ref[...] = (_dot(h, wqk_ref[:, :SB_WIDTH]) * (SB_HEAD_DIM ** -0.5 * LOG2_E)).astype(BF16)
    cq = _rms(c[:, :MLA_Q_LORA], qng_ref[...]).astype(BF16)
    ckv = _rms(ckv, kvng_ref[...]).astype(BF16)

    def rope(v, cos, sin_lo, sin_hi):
        return (v * cos + pltpu.roll(v, LANES - MLA_ROPE_DIM // 2, 1) * sin_lo
                + pltpu.roll(v, MLA_ROPE_DIM // 2, 1) * sin_hi)

    lane = lax.broadcasted_iota(jnp.int32, cos_ref.shape, 1)
    ck_t, sin_t = cos_ref[...], sin_ref[...]
    sk_lo = jnp.where(lane < MLA_NOPE_DIM + MLA_ROPE_DIM // 2, sin_t, 0.0)
    sk_hi = sin_t - sk_lo
    scale = MLA_QK_DIM ** -0.5 * np.log2(np.e)
    cq_t = jnp.where(lane < MLA_NOPE_DIM, scale, ck_t * scale)
    sq_lo, sq_hi = sk_lo * scale, sk_hi * scale

    k_rope = rope(c[:, MLA_Q_LORA:], ck_t, sk_lo, sk_hi)
    q = _dot(cq, wq_ref[...])
    kn = _dot(ckv, wkn_ref[...])
    v_t = lax.dot_general(wv_ref[...], ckv, NT_DIMS, preferred_element_type=F32)
    pad_rows = (lax.broadcasted_iota(jnp.int32, (V_ROWS - MLA_V_DIM, v_t.shape[1]), 0) == 0).astype(BF16)
    for hd in range(MLA_HEADS):
        vt_ref[hd, :MLA_V_DIM, :] = v_t[hd * MLA_V_DIM:(hd + 1) * MLA_V_DIM, :].astype(BF16)
        vt_ref[hd, MLA_V_DIM:, :] = pad_rows
    ksb_ref[...] = _dot(h, wqk_ref[:, SB_WIDTH:]).astype(BF16)
    v_sb_t = lax.dot_general(wvsb_ref[...], h, NT_DIMS, preferred_element_type=F32)
    for half in range(v_sb_t.shape[1] // SB_BLOCK):
        vsbt_ref[half] = v_sb_t[:, half * SB_BLOCK:(half + 1) * SB_BLOCK].astype(BF16)
    for hd in range(MLA_HEADS):
        sl = slice(hd * LANES, (hd + 1) * LANES)
        qf_ref[:, sl] = rope(q[:, sl], cq_t, sq_lo, sq_hi).astype(BF16)
        kf_ref[:, sl] = (kn[:, sl] + k_rope).astype(BF16)


def _proj_call(x2, gin, w_in, layer, wts, tabs):
    n, d = x2.shape
    s = tabs[0].shape[0]
    tm = min(TOKEN_TILE, s)
    row = lambda i: (i, 0)
    fix = lambda i: (0, 0)
    w_specs = [pl.BlockSpec(w.shape, fix) for w in wts]
    t_specs = [pl.BlockSpec((tm, LANES), lambda i: (i % (s // tm), 0)) for _ in tabs]
    vsbt_spec = pl.BlockSpec((None, tm // SB_BLOCK, SB_WIDTH, SB_BLOCK), lambda i: (i, 0, 0, 0))
    vsbt_shape = jax.ShapeDtypeStruct((n // tm, tm // SB_BLOCK, SB_WIDTH, SB_BLOCK), BF16)
    vt_spec = pl.BlockSpec((None, MLA_HEADS, V_ROWS, tm), lambda i: (i, 0, 0, 0))
    vt_shape = jax.ShapeDtypeStruct((n // tm, MLA_HEADS, V_ROWS, tm), BF16)
    tile = lambda w: pl.BlockSpec((tm, w), row)
    sds = lambda w: jax.ShapeDtypeStruct((n, w), BF16)
    return pl.pallas_call(
        _proj_kernel,
        grid=(n // tm,),
        in_specs=[pl.BlockSpec((tm, d), row), pl.BlockSpec(gin.shape, fix),
                  pl.BlockSpec((None, d, 2 * SB_WIDTH), lambda i: (layer, 0, 0), pipeline_mode=pl.Buffered(1))]
                 + w_specs + t_specs,
        out_specs=[tile(SB_WIDTH), tile(SB_WIDTH), vsbt_spec, tile(MLA_HEADS * LANES), tile(MLA_HEADS * LANES),
                   vt_spec],
        out_shape=[sds(SB_WIDTH), sds(SB_WIDTH), vsbt_shape, sds(MLA_HEADS * LANES), sds(MLA_HEADS * LANES),
                   vt_shape],
        scratch_shapes=[pltpu.VMEM((d, 2 * SB_WIDTH), BF16)],
        compiler_params=pltpu.CompilerParams(
            dimension_semantics=("arbitrary",), vmem_limit_bytes=VMEM_LIMIT_BYTES),
        name="proj",
    )(x2, gin, w_in, *wts, *tabs)


def _sb_kernel(q_ref, k_ref, vt_ref, u_ref, o_ref, acc_ref):
    t = SB_BLOCK
    qi = pl.program_id(1)

    first = lax.broadcasted_iota(jnp.int32, (LANES, t), 0) < SB_HEAD_DIM
    q_heads = []
    for p in range(SB_HEADS // 2):
        q_t = q_ref[:, p * LANES:(p + 1) * LANES].astype(F32).T
        q_heads += [jnp.where(first, q_t, 0.0).astype(BF16), jnp.where(first, 0.0, q_t).astype(BF16)]

    def block(j, r, strict):
        k_blk = k_ref[pl.ds(pl.multiple_of(j * t, t), t), :]
        heads = range(SB_HEADS)
        z = [_dot(k_blk[:, (hd // 2) * LANES:(hd // 2 + 1) * LANES], q_heads[hd]) for hd in heads]
        incl = []
        for hd in heads:
            neg_log_1m = jnp.maximum(z[hd], 0.0) + jnp.log(1.0 + jnp.exp2(-jnp.abs(z[hd]))) * LOG2_E
            if strict is not None:
                neg_log_1m = jnp.where(strict, neg_log_1m, 0.0)
            incl.append(_dot(u_ref[...], neg_log_1m.astype(BF16)))
        for hd in heads:
            w = jnp.exp2(z[hd] - incl[hd] + r[hd])
            if strict is not None:
                w = jnp.where(strict, w, 0.0)
            vt_blk = vt_ref[j, hd * SB_HEAD_DIM:(hd + 1) * SB_HEAD_DIM, :]
            acc_ref[hd] += _dot(vt_blk, w.astype(BF16))
        return tuple(r[hd] - incl[hd][0:1, :] for hd in heads)

    def r_max(r):
        return functools.reduce(jnp.maximum, [jnp.max(x) for x in r])

    acc_ref[...] = jnp.zeros_like(acc_ref)
    zero = jnp.zeros((1, t), F32)
    key_pos = lax.broadcasted_iota(jnp.int32, (t, t), 0)
    qry_pos = lax.broadcasted_iota(jnp.int32, (t, t), 1)
    r = block(qi, (zero,) * SB_HEADS, key_pos < qry_pos)

    def cond(c):
        j, rmax, _ = c
        return jnp.logical_and(j >= 0, rmax > -SB_DEAD_LOG2)

    def body(c):
        j, _, r = c
        r = block(j, r, None)
        return j - 1, r_max(r), r

    lax.while_loop(cond, body, (qi - 1, r_max(r), r))

    o_ref[...] = jnp.concatenate([acc_ref[hd] for hd in range(SB_HEADS)], axis=0).T


def _sb_call(q, k, vt, u):
    b, s, w = q.shape
    t = SB_BLOCK
    nblk = s // t
    assert vt.shape == (b, nblk, w, t)
    resident = lambda shape: pl.BlockSpec((None,) + shape, lambda bi, qi: (bi,) + (0,) * len(shape),
                                          pipeline_mode=pl.Buffered(1))
    return pl.pallas_call(
        _sb_kernel,
        grid=(b, nblk),
        in_specs=[
            pl.BlockSpec((None, t, w), lambda bi, qi: (bi, qi, 0)),
            resident((s, w)),
            resident((nblk, w, t)),
            pl.BlockSpec(u.shape, lambda bi, qi: (0, 0)),
        ],
        out_specs=pl.BlockSpec((None, t, w), lambda bi, qi: (bi, qi, 0)),
        out_shape=jax.ShapeDtypeStruct((b, s, w), F32),
        scratch_shapes=[pltpu.VMEM((SB_HEADS, SB_HEAD_DIM, t), F32)],
        compiler_params=pltpu.CompilerParams(
            dimension_semantics=("arbitrary", "arbitrary"), vmem_limit_bytes=VMEM_LIMIT_BYTES),
        name="sb_attn",
    )(q, k, vt, u)


def _mla_kernel(q_ref, k_ref, vt_ref, o_ref, acc_ref, m_ref, s0_ref, s1_ref, smax0_ref, smax1_ref):
    t = ATT_BLOCK
    qi = pl.program_id(1)

    q_heads = [q_ref[:, hd * LANES:(hd + 1) * LANES].astype(F32).T.astype(BF16) for hd in range(HEADS_PER_STEP)]

    def scores(hd, j, buf, allowed):
        s_ref, smax_ref = buf
        k_blk = k_ref[pl.ds(pl.multiple_of(j * t, t), t), hd * LANES:(hd + 1) * LANES]
        s = _dot(k_blk, q_heads[hd])
        if allowed is not None:
            s = jnp.where(allowed, s, -jnp.inf)
        s_ref[hd] = s
        smax_ref[hd] = jnp.max(s, axis=0, keepdims=True)

    def update(hd, j, buf):
        s_ref, smax_ref = buf
        m = m_ref[hd]
        m_new = jnp.maximum(m, smax_ref[hd])
        m_ref[hd] = m_new
        p = jnp.exp2(s_ref[hd] - m_new).astype(BF16)
        acc_ref[hd] = jnp.exp2(m - m_new) * acc_ref[hd] + _dot(vt_ref[j, hd], p)

    acc_ref[...] = jnp.zeros_like(acc_ref)
    m_ref[...] = jnp.full_like(m_ref, -jnp.inf)
    key_chunk = lax.broadcasted_iota(jnp.int32, (t, t), 0) // CHUNK
    qry_chunk = lax.broadcasted_iota(jnp.int32, (t, t), 1) // CHUNK
    bufs = ((s0_ref, smax0_ref), (s1_ref, smax1_ref))

    block_of = lambda k: jnp.where(k == 0, qi, k - 1)
    allowed = key_chunk <= qry_chunk
    for hd in range(HEADS_PER_STEP):
        scores(hd, qi, bufs[0], allowed)

    def body(k, carry):
        for parity in range(2):
            @pl.when(k % 2 == parity)
            def _():
                for hd in range(HEADS_PER_STEP):
                    scores(hd, block_of(k + 1), bufs[1 - parity], None)
                    update(hd, block_of(k), bufs[parity])
        return carry

    lax.fori_loop(0, qi, body, 0)

    for parity in range(2):
        @pl.when(qi % 2 == parity)
        def _():
            for hd in range(HEADS_PER_STEP):
                update(hd, block_of(qi), bufs[parity])

    o_ref[...] = jnp.concatenate(
        [acc_ref[hd, :MLA_V_DIM, :] / acc_ref[hd, MLA_V_DIM:MLA_V_DIM + 1, :] for hd in range(HEADS_PER_STEP)],
        axis=0).T


def _mla_call(qf, kf, vt):
    b, s, w = qf.shape
    t = ATT_BLOCK
    nblk = s // t
    assert vt.shape == (b, nblk, MLA_HEADS, V_ROWS, t) and HEADS_PER_STEP == MLA_HEADS
    resident = lambda shape: pl.BlockSpec((None,) + shape, lambda bi, qi: (bi,) + (0,) * len(shape),
                                          pipeline_mode=pl.Buffered(1))
    return pl.pallas_call(
        _mla_kernel,
        grid=(b, nblk),
        in_specs=[
            pl.BlockSpec((None, t, w), lambda bi, qi: (bi, qi, 0)),
            resident((s, w)),
            resident((nblk, MLA_HEADS, V_ROWS, t)),
        ],
        out_specs=pl.BlockSpec((None, t, MLA_WIDTH), lambda bi, qi: (bi, qi, 0)),
        out_shape=jax.ShapeDtypeStruct((b, s, MLA_WIDTH), F32),
        scratch_shapes=[
            pltpu.VMEM((HEADS_PER_STEP, V_ROWS, t), F32),
            pltpu.VMEM((HEADS_PER_STEP, 1, t), F32),
            pltpu.VMEM((HEADS_PER_STEP, t, t), F32),
            pltpu.VMEM((HEADS_PER_STEP, t, t), F32),
            pltpu.VMEM((HEADS_PER_STEP, 1, t), F32),
            pltpu.VMEM((HEADS_PER_STEP, 1, t), F32),
        ],
        compiler_params=pltpu.CompilerParams(
            dimension_semantics=("arbitrary", "arbitrary"), vmem_limit_bytes=VMEM_LIMIT_BYTES),
        name="mla_attn",
    )(qf, kf, vt)


def _out_kernel(x_ref, osb_ref, omla_ref, gin_ref, wgsb32_ref, wg_ref, bg_ref, wosb_ref, womla_ref, wout_ref,
                gf_ref, o_ref, wgsb_ref, *, final_norm):
    @pl.when(pl.program_id(0) == 0)
    def _():
        wgsb_ref[...] = wgsb32_ref[...].astype(BF16)

    x = x_ref[...]
    d = x.shape[-1]
    h = _rms(x, gin_ref[...]).astype(BF16)
    a_sb = (osb_ref[...] * jax.nn.silu(_dot(h, wgsb_ref[...]))).astype(BF16)
    a_mla = (omla_ref[...] * jax.nn.silu(_dot(h, wg_ref[:, :MLA_WIDTH]))).astype(BF16)
    g0 = MLA_WIDTH
    g_sb = jax.nn.sigmoid(_dot(h, wg_ref[:, g0:g0 + d]) + bg_ref[:, :d])
    merged = g_sb * _dot(a_sb, wosb_ref[...])
    g_mla = jax.nn.sigmoid(_dot(h, wg_ref[:, g0 + d:g0 + 2 * d]) + bg_ref[:, d:])
    merged = merged + g_mla * _dot(a_mla, womla_ref[...])
    y = x + _dot(merged.astype(BF16), wout_ref[...])
    o_ref[...] = _rms(y, gf_ref[...]) if final_norm else y


def _out_call(x2, osb, omla, gin, w_in, layer, wg, bg, wosb, womla, wout, gf, final_norm):
    n, d = x2.shape
    tm = min(TOKEN_TILE, n)
    row = lambda i: (i, 0)
    fix = lambda i: (0, 0)
    consts = (wg, bg, wosb, womla, wout, gf)
    return pl.pallas_call(
        functools.partial(_out_kernel, final_norm=final_norm),
        grid=(n // tm,),
        in_specs=[pl.BlockSpec((tm, d), row), pl.BlockSpec((tm, SB_WIDTH), row), pl.BlockSpec((tm, MLA_WIDTH), row),
                  pl.BlockSpec(gin.shape, fix),
                  pl.BlockSpec((None, d, SB_WIDTH), lambda i: (layer, 0, 3), pipeline_mode=pl.Buffered(1))]
                 + [pl.BlockSpec(c.shape, fix) for c in consts],
        out_specs=pl.BlockSpec((tm, d), row),
        out_shape=jax.ShapeDtypeStruct((n, d), F32),
        scratch_shapes=[pltpu.VMEM((d, SB_WIDTH), BF16)],
        compiler_params=pltpu.CompilerParams(
            dimension_semantics=("arbitrary",), vmem_limit_bytes=VMEM_LIMIT_BYTES),
        name="out",
    )(x2, osb, omla, gin, w_in, *consts)


def _rope_tables(s):
    half = MLA_ROPE_DIM // 2
    inv_freq = ROPE_THETA ** (-np.arange(half, dtype=np.float64) / half)
    ang = np.arange(s, dtype=np.float64)[:, None] * inv_freq[None, :]
    cos, sin = np.cos(ang), np.sin(ang)
    lanes = lambda lo, hi: np.concatenate(
        [np.zeros((s, MLA_NOPE_DIM)), lo, hi, np.zeros((s, LANES - MLA_QK_DIM))], axis=1).astype(np.float32)
    return jnp.asarray(lanes(cos, cos)), jnp.asarray(lanes(-sin, sin))


def _layer_weights(w_in, w_q_up, w_kv_up):
    d = w_in.shape[0]
    o_v, o_gsb, o_cq = 2 * SB_WIDTH, 3 * SB_WIDTH, 4 * SB_WIDTH
    o_ckv = o_cq + MLA_Q_LORA
    o_kr = o_ckv + MLA_KV_LORA
    o_gm = o_kr + MLA_ROPE_DIM
    cols = lambda lo, hi: w_in[:, lo:hi].astype(BF16)
    zeros = lambda r, w: jnp.zeros((r, w), BF16)
    wvsb_t = cols(o_v, o_gsb).T
    wc = jnp.concatenate([cols(o_cq, o_ckv), zeros(d, MLA_NOPE_DIM), cols(o_kr, o_gm),
                          zeros(d, LANES - MLA_QK_DIM)], axis=1)
    wckv = cols(o_ckv, o_kr)
    wg = cols(o_gm, w_in.shape[1])

    rq = w_q_up.shape[0]
    wq = w_q_up.astype(BF16).reshape(rq, MLA_HEADS, MLA_QK_DIM)
    wq = jnp.concatenate([wq, jnp.zeros((rq, MLA_HEADS, LANES - MLA_QK_DIM), BF16)], axis=-1)
    rkv = w_kv_up.shape[0]
    wkv = w_kv_up.astype(BF16).reshape(rkv, MLA_HEADS, MLA_NOPE_DIM + MLA_V_DIM)
    wkn = jnp.concatenate([wkv[..., :MLA_NOPE_DIM], jnp.zeros((rkv, MLA_HEADS, LANES - MLA_NOPE_DIM), BF16)], axis=-1)
    wv_t = wkv[..., MLA_NOPE_DIM:].reshape(rkv, MLA_WIDTH).T
    return dict(wvsb_t=wvsb_t, wc=wc, wckv=wckv, wg=wg, wq=wq.reshape(rq, MLA_HEADS * LANES),
                wkn=wkn.reshape(rkv, MLA_HEADS * LANES), wv=wv_t)


def kernel(x, norm_in_g, w_in, b_gate, q_norm_g, w_q_up, kv_norm_g, w_kv_up, w_o_sb, w_o_mla, w_out, norm_f_g):
    b, s, d = x.shape
    depth = w_in.shape[0]
    assert s % ATT_BLOCK == 0 and s % SB_BLOCK == 0 and TOKEN_TILE == ATT_BLOCK
    tabs = _rope_tables(s)
    idx = jnp.arange(SB_BLOCK)
    from_mat = (idx[None, :] >= idx[:, None]).astype(BF16)

    x2 = x.reshape(b * s, d)
    for l in range(depth):
        w = _layer_weights(w_in[l], w_q_up[l], w_kv_up[l])
        gin = norm_in_g[l][None, :]
        qsb, ksb, vsbt, qf, kf, vt = _proj_call(
            x2, gin, w_in, l,
            (w["wvsb_t"], w["wc"], w["wckv"], q_norm_g[l][None, :], kv_norm_g[l][None, :], w["wq"], w["wkn"],
             w["wv"]),
            tabs)
        r3 = lambda a: a.reshape(b, s, a.shape[-1])
        vsbt = vsbt.reshape(b, s // SB_BLOCK, SB_WIDTH, SB_BLOCK)
        vt = vt.reshape((b, s // ATT_BLOCK) + vt.shape[1:])
        o_sb = _sb_call(r3(qsb), r3(ksb), vsbt, from_mat).reshape(b * s, SB_WIDTH)
        o_mla = _mla_call(r3(qf), r3(kf), vt).reshape(b * s, MLA_WIDTH)
        x2 = _out_call(x2, o_sb, o_mla, gin, w_in, l, w["wg"], b_gate[l][None, :], w_o_sb[l].astype(BF16),
                       w_o_mla[l].astype(BF16), w_out[l].astype(BF16), norm_f_g[None, :],
                       final_norm=(l == depth - 1))
    return x2.reshape(b, s, d)
```

```python
import functools

import jax
import jax.numpy as jnp
import numpy as np
from jax import lax
from jax.experimental import pallas as pl
from jax.experimental.pallas import tpu as pltpu

EPS = 1e-6
CHUNK = 64

SB_HEADS = 8
SB_HEAD_DIM = 64
SB_WIDTH = SB_HEADS * SB_HEAD_DIM

MLA_HEADS = 8
MLA_NOPE_DIM = 64
MLA_ROPE_DIM = 32
MLA_V_DIM = 64
MLA_Q_LORA = 384
MLA_KV_LORA = 256
MLA_QK_DIM = MLA_NOPE_DIM + MLA_ROPE_DIM
MLA_WIDTH = MLA_HEADS * MLA_V_DIM
ROPE_THETA = 10000.0

LANES = 128
ATT_BLOCK = 512
SB_BLOCK = 256
SB_DEAD_LOG2 = 160.0
LOG2_E = float(np.log2(np.e))
HEADS_PER_STEP = 8
V_ROWS = 80
TOKEN_TILE = 512
OUT_TILE = 1024
VMEM_LIMIT_BYTES = 56 * 1024 * 1024

F32 = jnp.float32
BF16 = jnp.bfloat16
NT_DIMS = (((1,), (1,)), ((), ()))


def _dot(a, b):
    return jnp.dot(a, b, preferred_element_type=F32)


def _rms(x, g):
    inv = lax.rsqrt(jnp.mean(x * x, axis=-1, keepdims=True) + EPS)
    return (x * inv) * g


def _proj_kernel(x_ref, gin_ref, wqk_ref, wvsb_ref, wc_ref, wckv_ref, qng_ref, kvng_ref, wq_ref, wkn_ref, wv_ref,
                 cos_ref, sin_ref, qsb_ref, ksb_ref, vsbt_ref, qf_ref, kf_ref, vt_ref):
    h = _rms(x_ref[...], gin_ref[...]).astype(BF16)

    c = _dot(h, wc_ref[...])
    ckv = _dot(h, wckv_ref[...])
    qsb_ref[...] = (_dot(h, wqk_ref[:, :SB_WIDTH]) * (SB_HEAD_DIM ** -0.5 * LOG2_E)).astype(BF16)
    cq = _rms(c[:, :MLA_Q_LORA], qng_ref[...]).astype(BF16)
    ckv = _rms(ckv, kvng_ref[...]).astype(BF16)

    def rope(v, cos, sin_lo, sin_hi):
        return (v * cos + pltpu.roll(v, LANES - MLA_ROPE_DIM // 2, 1) * sin_lo
                + pltpu.roll(v, MLA_ROPE_DIM // 2, 1) * sin_hi)

    lane = lax.broadcasted_iota(jnp.int32, cos_ref.shape, 1)
    ck_t, sin_t = cos_ref[...], sin_ref[...]
    sk_lo = jnp.where(lane < MLA_NOPE_DIM + MLA_ROPE_DIM // 2, sin_t, 0.0)
    sk_hi = sin_t - sk_lo
    scale = MLA_QK_DIM ** -0.5 * np.log2(np.e)
    cq_t = jnp.where(lane < MLA_NOPE_DIM, scale, ck_t * scale)
    sq_lo, sq_hi = sk_lo * scale, sk_hi * scale

    k_rope = rope(c[:, MLA_Q_LORA:], ck_t, sk_lo, sk_hi)
    q = _dot(cq, wq_ref[...])
    kn = _dot(ckv, wkn_ref[...])
    v_t = lax.dot_general(wv_ref[...], ckv, NT_DIMS, preferred_element_type=F32)
    pad_rows = (lax.broadcasted_iota(jnp.int32, (V_ROWS - MLA_V_DIM, v_t.shape[1]), 0) == 0).astype(BF16)
    for hd in range(MLA_HEADS):
        vt_ref[hd, :MLA_V_DIM, :] = v_t[hd * MLA_V_DIM:(hd + 1) * MLA_V_DIM, :].astype(BF16)
        vt_ref[hd, MLA_V_DIM:, :] = pad_rows
    ksb_ref[...] = _dot(h, wqk_ref[:, SB_WIDTH:]).astype(BF16)
    v_sb_t = lax.dot_general(wvsb_ref[...], h, NT_DIMS, preferred_element_type=F32)
    for half in range(v_sb_t.shape[1] // SB_BLOCK):
        vsbt_ref[half] = v_sb_t[:, half * SB_BLOCK:(half + 1) * SB_BLOCK].astype(BF16)
    for hd in range(MLA_HEADS):
        sl = slice(hd * LANES, (hd + 1) * LANES)
        qf_ref[:, sl] = rope(q[:, sl], cq_t, sq_lo, sq_hi).astype(BF16)
        kf_ref[:, sl] = (kn[:, sl] + k_rope).astype(BF16)


def _proj_call(x2, gin, wts, tabs):
    n, d = x2.shape
    s = tabs[0].shape[0]
    tm = min(TOKEN_TILE, s)
    row = lambda i: (i, 0)
    fix = lambda i: (0, 0)
    w_specs = [pl.BlockSpec(w.shape, fix) for w in wts]
    t_specs = [pl.BlockSpec((tm, LANES), lambda i: (i % (s // tm), 0)) for _ in tabs]
    vsbt_spec = pl.BlockSpec((None, tm // SB_BLOCK, SB_WIDTH, SB_BLOCK), lambda i: (i, 0, 0, 0))
    vsbt_shape = jax.ShapeDtypeStruct((n // tm, tm // SB_BLOCK, SB_WIDTH, SB_BLOCK), BF16)
    vt_spec = pl.BlockSpec((None, MLA_HEADS, V_ROWS, tm), lambda i: (i, 0, 0, 0))
    vt_shape = jax.ShapeDtypeStruct((n // tm, MLA_HEADS, V_ROWS, tm), BF16)
    tile = lambda w: pl.BlockSpec((tm, w), row)
    sds = lambda w: jax.ShapeDtypeStruct((n, w), BF16)
    return pl.pallas_call(
        _proj_kernel,
        grid=(n // tm,),
        in_specs=[pl.BlockSpec((tm, d), row), pl.BlockSpec(gin.shape, fix)] + w_specs + t_specs,
        out_specs=[tile(SB_WIDTH), tile(SB_WIDTH), vsbt_spec, tile(MLA_HEADS * LANES), tile(MLA_HEADS * LANES),
                   vt_spec],
        out_shape=[sds(SB_WIDTH), sds(SB_WIDTH), vsbt_shape, sds(MLA_HEADS * LANES), sds(MLA_HEADS * LANES),
                   vt_shape],
        compiler_params=pltpu.CompilerParams(
            dimension_semantics=("arbitrary",), vmem_limit_bytes=VMEM_LIMIT_BYTES),
        name="proj",
    )(x2, gin, *wts, *tabs)


def _sb_kernel(q_ref, k_ref, vt_ref, u_ref, o_ref, acc_ref):
    t = SB_BLOCK
    qi = pl.program_id(1)

    first = lax.broadcasted_iota(jnp.int32, (LANES, t), 0) < SB_HEAD_DIM
    q_heads = []
    for p in range(SB_HEADS // 2):
        q_t = q_ref[:, p * LANES:(p + 1) * LANES].astype(F32).T
        q_heads += [jnp.where(first, q_t, 0.0).astype(BF16), jnp.where(first, 0.0, q_t).astype(BF16)]

    def block(j, r, strict):
        k_blk = k_ref[pl.ds(pl.multiple_of(j * t, t), t), :]
        heads = range(SB_HEADS)
        z = [_dot(k_blk[:, (hd // 2) * LANES:(hd // 2 + 1) * LANES], q_heads[hd]) for hd in heads]
        incl = []
        for hd in heads:
            neg_log_1m = jnp.maximum(z[hd], 0.0) + jnp.log(1.0 + jnp.exp2(-jnp.abs(z[hd]))) * LOG2_E
            if strict is not None:
                neg_log_1m = jnp.where(strict, neg_log_1m, 0.0)
            incl.append(_dot(u_ref[...], neg_log_1m.astype(BF16)))
        for hd in heads:
            w = jnp.exp2(z[hd] - incl[hd] + r[hd])
            if strict is not None:
                w = jnp.where(strict, w, 0.0)
            vt_blk = vt_ref[j, hd * SB_HEAD_DIM:(hd + 1) * SB_HEAD_DIM, :]
            acc_ref[hd] += _dot(vt_blk, w.astype(BF16))
        return tuple(r[hd] - incl[hd][0:1, :] for hd in heads)

    def r_max(r):
        return functools.reduce(jnp.maximum, [jnp.max(x) for x in r])

    acc_ref[...] = jnp.zeros_like(acc_ref)
    zero = jnp.zeros((1, t), F32)
    key_pos = lax.broadcasted_iota(jnp.int32, (t, t), 0)
    qry_pos = lax.broadcasted_iota(jnp.int32, (t, t), 1)
    r = block(qi, (zero,) * SB_HEADS, key_pos < qry_pos)

    def cond(c):
        j, rmax, _ = c
        return jnp.logical_and(j >= 0, rmax > -SB_DEAD_LOG2)

    def body(c):
        j, _, r = c
        r = block(j, r, None)
        return j - 1, r_max(r), r

    lax.while_loop(cond, body, (qi - 1, r_max(r), r))

    o_ref[...] = jnp.concatenate([acc_ref[hd] for hd in range(SB_HEADS)], axis=0).T


def _sb_call(q, k, vt, u):
    b, s, w = q.shape
    t = SB_BLOCK
    nblk = s // t
    assert vt.shape == (b, nblk, w, t)
    resident = lambda shape: pl.BlockSpec((None,) + shape, lambda bi, qi: (bi,) + (0,) * len(shape),
                                          pipeline_mode=pl.Buffered(1))
    return pl.pallas_call(
        _sb_kernel,
        grid=(b, nblk),
        in_specs=[
            pl.BlockSpec((None, t, w), lambda bi, qi: (bi, qi, 0)),
            resident((s, w)),
            resident((nblk, w, t)),
            pl.BlockSpec(u.shape, lambda bi, qi: (0, 0)),
        ],
        out_specs=pl.BlockSpec((None, t, w), lambda bi, qi: (bi, qi, 0)),
        out_shape=jax.ShapeDtypeStruct((b, s, w), F32),
        scratch_shapes=[pltpu.VMEM((SB_HEADS, SB_HEAD_DIM, t), F32)],
        compiler_params=pltpu.CompilerParams(
            dimension_semantics=("arbitrary", "arbitrary"), vmem_limit_bytes=VMEM_LIMIT_BYTES),
        name="sb_attn",
    )(q, k, vt, u)


def _mla_kernel(q_ref, k_ref, vt_ref, o_ref, acc_ref, m_ref, s0_ref, s1_ref, smax0_ref, smax1_ref):
    t = ATT_BLOCK
    qi = pl.program_id(1)

    q_heads = [q_ref[:, hd * LANES:(hd + 1) * LANES].astype(F32).T.astype(BF16) for hd in range(HEADS_PER_STEP)]

    def scores(hd, j, buf, allowed):
        s_ref, smax_ref = buf
        k_blk = k_ref[pl.ds(pl.multiple_of(j * t, t), t), hd * LANES:(hd + 1) * LANES]
        s = _dot(k_blk, q_heads[hd])
        if allowed is not None:
            s = jnp.where(allowed, s, -jnp.inf)
        s_ref[hd] = s
        smax_ref[hd] = jnp.max(s, axis=0, keepdims=True)

    def update(hd, j, buf):
        s_ref, smax_ref = buf
        m = m_ref[hd]
        m_new = jnp.maximum(m, smax_ref[hd])
        m_ref[hd] = m_new
        p = jnp.exp2(s_ref[hd] - m_new).astype(BF16)
        acc_ref[hd] = jnp.exp2(m - m_new) * acc_ref[hd] + _dot(vt_ref[j, hd], p)

    acc_ref[...] = jnp.zeros_like(acc_ref)
    m_ref[...] = jnp.full_like(m_ref, -jnp.inf)
    key_chunk = lax.broadcasted_iota(jnp.int32, (t, t), 0) // CHUNK
    qry_chunk = lax.broadcasted_iota(jnp.int32, (t, t), 1) // CHUNK
    bufs = ((s0_ref, smax0_ref), (s1_ref, smax1_ref))

    block_of = lambda k: jnp.where(k == 0, qi, k - 1)
    allowed = key_chunk <= qry_chunk
    for hd in range(HEADS_PER_STEP):
        scores(hd, qi, bufs[0], allowed)

    def body(k, carry):
        for parity in range(2):
            @pl.when(k % 2 == parity)
            def _():
                for hd in range(HEADS_PER_STEP):
                    scores(hd, block_of(k + 1), bufs[1 - parity], None)
                    update(hd, block_of(k), bufs[parity])
        return carry

    lax.fori_loop(0, qi, body, 0)

    for parity in range(2):
        @pl.when(qi % 2 == parity)
        def _():
            for hd in range(HEADS_PER_STEP):
                update(hd, block_of(qi), bufs[parity])

    o_ref[...] = jnp.concatenate(
        [acc_ref[hd, :MLA_V_DIM, :] / acc_ref[hd, MLA_V_DIM:MLA_V_DIM + 1, :] for hd in range(HEADS_PER_STEP)],
        axis=0).T


def _mla_call(qf, kf, vt):
    b, s, w = qf.shape
    t = ATT_BLOCK
    nblk = s // t
    assert vt.shape == (b, nblk, MLA_HEADS, V_ROWS, t) and HEADS_PER_STEP == MLA_HEADS
    resident = lambda shape: pl.BlockSpec((None,) + shape, lambda bi, qi: (bi,) + (0,) * len(shape),
                                          pipeline_mode=pl.Buffered(1))
    return pl.pallas_call(
        _mla_kernel,
        grid=(b, nblk),
        in_specs=[
            pl.BlockSpec((None, t, w), lambda bi, qi: (bi, qi, 0)),
            resident((s, w)),
            resident((nblk, MLA_HEADS, V_ROWS, t)),
        ],
        out_specs=pl.BlockSpec((None, t, MLA_WIDTH), lambda bi, qi: (bi, qi, 0)),
        out_shape=jax.ShapeDtypeStruct((b, s, MLA_WIDTH), F32),
        scratch_shapes=[
            pltpu.VMEM((HEADS_PER_STEP, V_ROWS, t), F32),
            pltpu.VMEM((HEADS_PER_STEP, 1, t), F32),
            pltpu.VMEM((HEADS_PER_STEP, t, t), F32),
            pltpu.VMEM((HEADS_PER_STEP, t, t), F32),
            pltpu.VMEM((HEADS_PER_STEP, 1, t), F32),
            pltpu.VMEM((HEADS_PER_STEP, 1, t), F32),
        ],
        compiler_params=pltpu.CompilerParams(
            dimension_semantics=("arbitrary", "arbitrary"), vmem_limit_bytes=VMEM_LIMIT_BYTES),
        name="mla_attn",
    )(qf, kf, vt)


def _out_kernel(x_ref, osb_ref, omla_ref, gin_ref, wgsb_ref, wg_ref, bg_ref, wosb_ref, womla_ref, wout_ref,
                gf_ref, o_ref, *, final_norm):
    x = x_ref[...]
    d = x.shape[-1]
    h = _rms(x, gin_ref[...]).astype(BF16)
    a_sb = (osb_ref[...] * jax.nn.silu(_dot(h, wgsb_ref[...]))).astype(BF16)
    a_mla = (omla_ref[...] * jax.nn.silu(_dot(h, wg_ref[:, :MLA_WIDTH]))).astype(BF16)
    g0 = MLA_WIDTH
    g_sb = jax.nn.sigmoid(_dot(h, wg_ref[:, g0:g0 + d]) + bg_ref[:, :d])
    merged = g_sb * _dot(a_sb, wosb_ref[...])
    g_mla = jax.nn.sigmoid(_dot(h, wg_ref[:, g0 + d:g0 + 2 * d]) + bg_ref[:, d:])
    merged = merged + g_mla * _dot(a_mla, womla_ref[...])
    y = x + _dot(merged.astype(BF16), wout_ref[...])
    o_ref[...] = _rms(y, gf_ref[...]) if final_norm else y


def _out_call(x2, osb, omla, gin, wgsb, wg, bg, wosb, womla, wout, gf, final_norm):
    n, d = x2.shape
    tm = min(OUT_TILE, n)
    row = lambda i: (i, 0)
    fix = lambda i: (0, 0)
    consts = (gin, wgsb, wg, bg, wosb, womla, wout, gf)
    return pl.pallas_call(
        functools.partial(_out_kernel, final_norm=final_norm),
        grid=(n // tm,),
        in_specs=[pl.BlockSpec((tm, d), row), pl.BlockSpec((tm, SB_WIDTH), row), pl.BlockSpec((tm, MLA_WIDTH), row)]
                 + [pl.BlockSpec(c.shape, fix, pipeline_mode=pl.Buffered(1)) for c in consts],
        out_specs=pl.BlockSpec((tm, d), row),
        out_shape=jax.ShapeDtypeStruct((n, d), F32),
        compiler_params=pltpu.CompilerParams(
            dimension_semantics=("arbitrary",), vmem_limit_bytes=VMEM_LIMIT_BYTES),
        name="out",
    )(x2, osb, omla, *consts)


def _rope_tables(s):
    half = MLA_ROPE_DIM // 2
    inv_freq = ROPE_THETA ** (-np.arange(half, dtype=np.float64) / half)
    ang = np.arange(s, dtype=np.float64)[:, None] * inv_freq[None, :]
    cos, sin = np.cos(ang), np.sin(ang)
    lanes = lambda lo, hi: np.concatenate(
        [np.zeros((s, MLA_NOPE_DIM)), lo, hi, np.zeros((s, LANES - MLA_QK_DIM))], axis=1).astype(np.float32)
    return jnp.asarray(lanes(cos, cos)), jnp.asarray(lanes(-sin, sin))


def _layer_weights(w_in, w_q_up, w_kv_up):
    d = w_in.shape[0]
    o_v, o_gsb, o_cq = 2 * SB_WIDTH, 3 * SB_WIDTH, 4 * SB_WIDTH
    o_ckv = o_cq + MLA_Q_LORA
    o_kr = o_ckv + MLA_KV_LORA
    o_gm = o_kr + MLA_ROPE_DIM
    cols = lambda lo, hi: w_in[:, lo:hi].astype(BF16)
    zeros = lambda r, w: jnp.zeros((r, w), BF16)
    wqk = cols(0, o_v)
    wvsb_t = cols(o_v, o_gsb).T
    wgsb = cols(o_gsb, o_cq)
    wc = jnp.concatenate([cols(o_cq, o_ckv), zeros(d, MLA_NOPE_DIM), cols(o_kr, o_gm),
                          zeros(d, LANES - MLA_QK_DIM)], axis=1)
    wckv = cols(o_ckv, o_kr)
    wg = cols(o_gm, w_in.shape[1])

    rq = w_q_up.shape[0]
    wq = w_q_up.astype(BF16).reshape(rq, MLA_HEADS, MLA_QK_DIM)
    wq = jnp.concatenate([wq, jnp.zeros((rq, MLA_HEADS, LANES - MLA_QK_DIM), BF16)], axis=-1)
    rkv = w_kv_up.shape[0]
    wkv = w_kv_up.astype(BF16).reshape(rkv, MLA_HEADS, MLA_NOPE_DIM + MLA_V_DIM)
    wkn = jnp.concatenate([wkv[..., :MLA_NOPE_DIM], jnp.zeros((rkv, MLA_HEADS, LANES - MLA_NOPE_DIM), BF16)], axis=-1)
    wv_t = wkv[..., MLA_NOPE_DIM:].reshape(rkv, MLA_WIDTH).T
    return dict(wqk=wqk, wvsb_t=wvsb_t, wgsb=wgsb, wc=wc, wckv=wckv, wg=wg, wq=wq.reshape(rq, MLA_HEADS * LANES),
                wkn=wkn.reshape(rkv, MLA_HEADS * LANES), wv=wv_t)


def kernel(x, norm_in_g, w_in, b_gate, q_norm_g, w_q_up, kv_norm_g, w_kv_up, w_o_sb, w_o_mla, w_out, norm_f_g):
    b, s, d = x.shape
    depth = w_in.shape[0]
    assert s % ATT_BLOCK == 0 and s % SB_BLOCK == 0 and TOKEN_TILE == ATT_BLOCK
    tabs = _rope_tables(s)
    idx = jnp.arange(SB_BLOCK)
    from_mat = (idx[None, :] >= idx[:, None]).astype(BF16)

    x2 = x.reshape(b * s, d)
    for l in range(depth):
        w = _layer_weights(w_in[l], w_q_up[l], w_kv_up[l])
        gin = norm_in_g[l][None, :]
        qsb, ksb, vsbt, qf, kf, vt = _proj_call(
            x2, gin,
            (w["wqk"], w["wvsb_t"], w["wc"], w["wckv"], q_norm_g[l][None, :], kv_norm_g[l][None, :], w["wq"], w["wkn"],
             w["wv"]),
            tabs)
        r3 = lambda a: a.reshape(b, s, a.shape[-1])
        vsbt = vsbt.reshape(b, s // SB_BLOCK, SB_WIDTH, SB_BLOCK)
        vt = vt.reshape((b, s // ATT_BLOCK) + vt.shape[1:])
        o_sb = _sb_call(r3(qsb), r3(ksb), vsbt, from_mat).reshape(b * s, SB_WIDTH)
        o_mla = _mla_call(r3(qf), r3(kf), vt).reshape(b * s, MLA_WIDTH)
        x2 = _out_call(x2, o_sb, o_mla, gin, w["wgsb"], w["wg"], b_gate[l][None, :], w_o_sb[l].astype(BF16),
                       w_o_mla[l].astype(BF16), w_out[l].astype(BF16), norm_f_g[None, :],
                       final_norm=(l == depth - 1))
    return x2.reshape(b, s, d)
```

```python
import functools

import jax
import jax.numpy as jnp
import numpy as np
from jax import lax
from jax.experimental import pallas as pl
from jax.experimental.pallas import tpu as pltpu

EPS = 1e-6
CHUNK = 64

SB_HEADS = 8
SB_HEAD_DIM = 64
SB_WIDTH = SB_HEADS * SB_HEAD_DIM

MLA_HEADS = 8
MLA_NOPE_DIM = 64
MLA_ROPE_DIM = 32
MLA_V_DIM = 64
MLA_Q_LORA = 384
MLA_KV_LORA = 256
MLA_QK_DIM = MLA_NOPE_DIM + MLA_ROPE_DIM
MLA_WIDTH = MLA_HEADS * MLA_V_DIM
ROPE_THETA = 10000.0

LANES = 128
ATT_BLOCK = 512
SB_BLOCK = 256
SB_DEAD_LOG2 = 160.0
LOG2_E = float(np.log2(np.e))
HEADS_PER_STEP = 8
V_ROWS = 80
TOKEN_TILE = 512
OUT_TILE = 1024
VMEM_LIMIT_BYTES = 56 * 1024 * 1024

F32 = jnp.float32
BF16 = jnp.bfloat16
NT_DIMS = (((1,), (1,)), ((), ()))


def _dot(a, b):
    return jnp.dot(a, b, preferred_element_type=F32)


def _rms(x, g):
    inv = lax.rsqrt(jnp.mean(x * x, axis=-1, keepdims=True) + EPS)
    return (x * inv) * g


def _proj_kernel(x_ref, gin_ref, wqk_ref, wvsb_ref, wc_ref, wckv_ref, qng_ref, kvng_ref, wq_ref, wkn_ref, wv_ref,
                 cos_ref, sin_ref, qsb_ref, ksb_ref, vsbt_ref, qf_ref, kf_ref, vt_ref):
    h = _rms(x_ref[...], gin_ref[...]).astype(BF16)

    c = _dot(h, wc_ref[...])
    ckv = _dot(h, wckv_ref[...])
    qsb_ref[...] = (_dot(h, wqk_ref[:, :SB_WIDTH]) * (SB_HEAD_DIM ** -0.5 * LOG2_E)).astype(BF16)
    cq = _rms(c[:, :MLA_Q_LORA], qng_ref[...]).astype(BF16)
    ckv = _rms(ckv, kvng_ref[...]).astype(BF16)

    def rope(v, cos, sin_lo, sin_hi):
        return (v * cos + pltpu.roll(v, LANES - MLA_ROPE_DIM // 2, 1) * sin_lo
                + pltpu.roll(v, MLA_ROPE_DIM // 2, 1) * sin_hi)

    lane = lax.broadcasted_iota(jnp.int32, cos_ref.shape, 1)
    ck_t, sin_t = cos_ref[...], sin_ref[...]
    sk_lo = jnp.where(lane < MLA_NOPE_DIM + MLA_ROPE_DIM // 2, sin_t, 0.0)
    sk_hi = sin_t - sk_lo
    scale = MLA_QK_DIM ** -0.5 * np.log2(np.e)
    cq_t = jnp.where(lane < MLA_NOPE_DIM, scale, ck_t * scale)
    sq_lo, sq_hi = sk_lo * scale, sk_hi * scale

    k_rope = rope(c[:, MLA_Q_LORA:], ck_t, sk_lo, sk_hi)
    q = _dot(cq, wq_ref[...])
    kn = _dot(ckv, wkn_ref[...])
    v_t = lax.dot_general(wv_ref[...], ckv, NT_DIMS, preferred_element_type=F32)
    pad_rows = (lax.broadcasted_iota(jnp.int32, (V_ROWS - MLA_V_DIM, v_t.shape[1]), 0) == 0).astype(BF16)
    for hd in range(MLA_HEADS):
        vt_ref[hd, :MLA_V_DIM, :] = v_t[hd * MLA_V_DIM:(hd + 1) * MLA_V_DIM, :].astype(BF16)
        vt_ref[hd, MLA_V_DIM:, :] = pad_rows
    ksb_ref[...] = _dot(h, wqk_ref[:, SB_WIDTH:]).astype(BF16)
    v_sb_t = lax.dot_general(wvsb_ref[...], h, NT_DIMS, preferred_element_type=F32)
    for half in range(v_sb_t.shape[1] // SB_BLOCK):
        vsbt_ref[half] = v_sb_t[:, half * SB_BLOCK:(half + 1) * SB_BLOCK].astype(BF16)
    for hd in range(MLA_HEADS):
        sl = slice(hd * LANES, (hd + 1) * LANES)
        qf_ref[:, sl] = rope(q[:, sl], cq_t, sq_lo, sq_hi).astype(BF16)
        kf_ref[:, sl] = (kn[:, sl] + k_rope).astype(BF16)


def _proj_call(x2, gin, wts, tabs):
    n, d = x2.shape
    s = tabs[0].shape[0]
    tm = min(TOKEN_TILE, s)
    row = lambda i: (i, 0)
    fix = lambda i: (0, 0)
    w_specs = [pl.BlockSpec(w.shape, fix) for w in wts]
    t_specs = [pl.BlockSpec((tm, LANES), lambda i: (i % (s // tm), 0)) for _ in tabs]
    vsbt_spec = pl.BlockSpec((None, tm // SB_BLOCK, SB_WIDTH, SB_BLOCK), lambda i: (i, 0, 0, 0))
    vsbt_shape = jax.ShapeDtypeStruct((n // tm, tm // SB_BLOCK, SB_WIDTH, SB_BLOCK), BF16)
    vt_spec = pl.BlockSpec((None, MLA_HEADS, V_ROWS, tm), lambda i: (i, 0, 0, 0))
    vt_shape = jax.ShapeDtypeStruct((n // tm, MLA_HEADS, V_ROWS, tm), BF16)
    tile = lambda w: pl.BlockSpec((tm, w), row)
    sds = lambda w: jax.ShapeDtypeStruct((n, w), BF16)
    return pl.pallas_call(
        _proj_kernel,
        grid=(n // tm,),
        in_specs=[pl.BlockSpec((tm, d), row), pl.BlockSpec(gin.shape, fix)] + w_specs + t_specs,
        out_specs=[tile(SB_WIDTH), tile(SB_WIDTH), vsbt_spec, tile(MLA_HEADS * LANES), tile(MLA_HEADS * LANES),
                   vt_spec],
        out_shape=[sds(SB_WIDTH), sds(SB_WIDTH), vsbt_shape, sds(MLA_HEADS * LANES), sds(MLA_HEADS * LANES),
                   vt_shape],
        compiler_params=pltpu.CompilerParams(
            dimension_semantics=("arbitrary",), vmem_limit_bytes=VMEM_LIMIT_BYTES),
        name="proj",
    )(x2, gin, *wts, *tabs)


def _sb_kernel(q_ref, k_ref, vt_ref, u_ref, o_ref, acc_ref):
    t = SB_BLOCK
    qi = pl.program_id(1)

    first = lax.broadcasted_iota(jnp.int32, (LANES, t), 0) < SB_HEAD_DIM
    q_heads = []
    for p in range(SB_HEADS // 2):
        q_t = q_ref[:, p * LANES:(p + 1) * LANES].astype(F32).T
        q_heads += [jnp.where(first, q_t, 0.0).astype(BF16), jnp.where(first, 0.0, q_t).astype(BF16)]

    def block(j, r, strict):
        k_blk = k_ref[pl.ds(pl.multiple_of(j * t, t), t), :]
        heads = range(SB_HEADS)
        z = [_dot(k_blk[:, (hd // 2) * LANES:(hd // 2 + 1) * LANES], q_heads[hd]) for hd in heads]
        incl = []
        for hd in heads:
            neg_log_1m = jnp.maximum(z[hd], 0.0) + jnp.log(1.0 + jnp.exp2(-jnp.abs(z[hd]))) * LOG2_E
            if strict is not None:
                neg_log_1m = jnp.where(strict, neg_log_1m, 0.0)
            incl.append(_dot(u_ref[...], neg_log_1m.astype(BF16)))
        for hd in heads:
            w = jnp.exp2(z[hd] - incl[hd] + r[hd])
            if strict is not None:
                w = jnp.where(strict, w, 0.0)
            vt_blk = vt_ref[j, hd * SB_HEAD_DIM:(hd + 1) * SB_HEAD_DIM, :]
            acc_ref[hd] += _dot(vt_blk, w.astype(BF16))
        return tuple(r[hd] - incl[hd][0:1, :] for hd in heads)

    def r_max(r):
        return functools.reduce(jnp.maximum, [jnp.max(x) for x in r])

    acc_ref[...] = jnp.zeros_like(acc_ref)
    zero = jnp.zeros((1, t), F32)
    key_pos = lax.broadcasted_iota(jnp.int32, (t, t), 0)
    qry_pos = lax.broadcasted_iota(jnp.int32, (t, t), 1)
    r = block(qi, (zero,) * SB_HEADS, key_pos < qry_pos)

    def cond(c):
        j, rmax, _ = c
        return jnp.logical_and(j >= 0, rmax > -SB_DEAD_LOG2)

    def body(c):
        j, _, r = c
        r = block(j, r, None)
        return j - 1, r_max(r), r

    lax.while_loop(cond, body, (qi - 1, r_max(r), r))

    o_ref[...] = jnp.concatenate([acc_ref[hd] for hd in range(SB_HEADS)], axis=0).T


def _sb_call(q, k, vt, u):
    b, s, w = q.shape
    t = SB_BLOCK
    nblk = s // t
    assert vt.shape == (b, nblk, w, t)
    resident = lambda shape: pl.BlockSpec((None,) + shape, lambda bi, qi: (bi,) + (0,) * len(shape),
                                          pipeline_mode=pl.Buffered(1))
    return pl.pallas_call(
        _sb_kernel,
        grid=(b, nblk),
        in_specs=[
            pl.BlockSpec((None, t, w), lambda bi, qi: (bi, qi, 0)),
            resident((s, w)),
            resident((nblk, w, t)),
            pl.BlockSpec(u.shape, lambda bi, qi: (0, 0)),
        ],
        out_specs=pl.BlockSpec((None, t, w), lambda bi, qi: (bi, qi, 0)),
        out_shape=jax.ShapeDtypeStruct((b, s, w), F32),
        scratch_shapes=[pltpu.VMEM((SB_HEADS, SB_HEAD_DIM, t), F32)],
        compiler_params=pltpu.CompilerParams(
            dimension_semantics=("arbitrary", "arbitrary"), vmem_limit_bytes=VMEM_LIMIT_BYTES),
        name="sb_attn",
    )(q, k, vt, u)


def _mla_kernel(q_ref, qn_ref, k_ref, vt_ref, o_ref, acc_ref, m_ref, qt_ref, s0_ref, s1_ref, smax0_ref, smax1_ref, *,
                nblk):
    t = ATT_BLOCK
    qi = pl.program_id(1)
    heads = range(HEADS_PER_STEP)
    bufs = ((s0_ref, smax0_ref), (s1_ref, smax1_ref))
    key_chunk = lax.broadcasted_iota(jnp.int32, (t, t), 0) // CHUNK
    qry_chunk = lax.broadcasted_iota(jnp.int32, (t, t), 1) // CHUNK
    allowed = key_chunk <= qry_chunk

    first_visit = (qi * (qi + 1)) // 2
    slot = qi % 2

    def transpose_q(src_ref, dst_slot, hd):
        qt_ref[dst_slot, hd] = src_ref[:, hd * LANES:(hd + 1) * LANES].astype(F32).T.astype(BF16)

    def scores(hd, j, q_slot, buf, mask):
        s_ref, smax_ref = buf
        k_blk = k_ref[pl.ds(pl.multiple_of(j * t, t), t), hd * LANES:(hd + 1) * LANES]
        s = _dot(k_blk, qt_ref[q_slot, hd])
        if mask:
            s = jnp.where(allowed, s, -jnp.inf)
        s_ref[hd] = s
        smax_ref[hd] = jnp.max(s, axis=0, keepdims=True)

    def update(hd, j, buf):
        s_ref, smax_ref = buf
        m = m_ref[hd]
        m_new = jnp.maximum(m, smax_ref[hd])
        m_ref[hd] = m_new
        p = jnp.exp2(s_ref[hd] - m_new).astype(BF16)
        acc_ref[hd] = jnp.exp2(m - m_new) * acc_ref[hd] + _dot(vt_ref[j, hd], p)

    @pl.when(qi == 0)
    def _():
        for hd in heads:
            transpose_q(q_ref, 0, hd)
            scores(hd, 0, 0, bufs[0], True)

    acc_ref[...] = jnp.zeros_like(acc_ref)
    m_ref[...] = jnp.full_like(m_ref, -jnp.inf)
    block_of = lambda k: jnp.where(k == 0, qi, k - 1)

    def body(k, carry):
        for parity in range(2):
            @pl.when((first_visit + k) % 2 == parity)
            def _():
                for hd in heads:
                    scores(hd, block_of(k + 1), slot, bufs[1 - parity], False)
                    update(hd, block_of(k), bufs[parity])
        return carry

    lax.fori_loop(0, qi, body, 0)

    next_block = jnp.minimum(qi + 1, nblk - 1)
    for parity in range(2):
        @pl.when((first_visit + qi) % 2 == parity)
        def _():
            for hd in heads:
                transpose_q(qn_ref, 1 - slot, hd)
                scores(hd, next_block, 1 - slot, bufs[1 - parity], True)
                update(hd, block_of(qi), bufs[parity])

    o_ref[...] = jnp.concatenate(
        [acc_ref[hd, :MLA_V_DIM, :] / acc_ref[hd, MLA_V_DIM:MLA_V_DIM + 1, :] for hd in heads], axis=0).T


def _mla_call(qf, kf, vt):
    b, s, w = qf.shape
    t = ATT_BLOCK
    nblk = s // t
    assert vt.shape == (b, nblk, MLA_HEADS, V_ROWS, t) and HEADS_PER_STEP == MLA_HEADS
    resident = lambda shape: pl.BlockSpec((None,) + shape, lambda bi, qi: (bi,) + (0,) * len(shape),
                                          pipeline_mode=pl.Buffered(1))
    return pl.pallas_call(
        functools.partial(_mla_kernel, nblk=nblk),
        grid=(b, nblk),
        in_specs=[
            pl.BlockSpec((None, t, w), lambda bi, qi: (bi, qi, 0)),
            pl.BlockSpec((None, t, w), lambda bi, qi: (bi, jnp.minimum(qi + 1, nblk - 1), 0)),
            resident((s, w)),
            resident((nblk, MLA_HEADS, V_ROWS, t)),
        ],
        out_specs=pl.BlockSpec((None, t, MLA_WIDTH), lambda bi, qi: (bi, qi, 0)),
        out_shape=jax.ShapeDtypeStruct((b, s, MLA_WIDTH), F32),
        scratch_shapes=[
            pltpu.VMEM((HEADS_PER_STEP, V_ROWS, t), F32),
            pltpu.VMEM((HEADS_PER_STEP, 1, t), F32),
            pltpu.VMEM((2, HEADS_PER_STEP, LANES, t), BF16),
            pltpu.VMEM((HEADS_PER_STEP, t, t), F32),
            pltpu.VMEM((HEADS_PER_STEP, t, t), F32),
            pltpu.VMEM((HEADS_PER_STEP, 1, t), F32),
            pltpu.VMEM((HEADS_PER_STEP, 1, t), F32),
        ],
        compiler_params=pltpu.CompilerParams(
            dimension_semantics=("arbitrary", "arbitrary"), vmem_limit_bytes=VMEM_LIMIT_BYTES),
        name="mla_attn",
    )(qf, qf, kf, vt)


def _out_kernel(x_ref, osb_ref, omla_ref, gin_ref, wgsb_ref, wg_ref, bg_ref, wosb_ref, womla_ref, wout_ref,
                gf_ref, o_ref, *, final_norm):
    x = x_ref[...]
    d = x.shape[-1]
    h = _rms(x, gin_ref[...]).astype(BF16)
    a_sb = (osb_ref[...] * jax.nn.silu(_dot(h, wgsb_ref[...]))).astype(BF16)
    a_mla = (omla_ref[...] * jax.nn.silu(_dot(h, wg_ref[:, :MLA_WIDTH]))).astype(BF16)
    g0 = MLA_WIDTH
    g_sb = jax.nn.sigmoid(_dot(h, wg_ref[:, g0:g0 + d]) + bg_ref[:, :d])
    merged = g_sb * _dot(a_sb, wosb_ref[...])
    g_mla = jax.nn.sigmoid(_dot(h, wg_ref[:, g0 + d:g0 + 2 * d]) + bg_ref[:, d:])
    merged = merged + g_mla * _dot(a_mla, womla_ref[...])
    y = x + _dot(merged.astype(BF16), wout_ref[...])
    o_ref[...] = _rms(y, gf_ref[...]) if final_norm else y


def _out_call(x2, osb, omla, gin, wgsb, wg, bg, wosb, womla, wout, gf, final_norm):
    n, d = x2.shape
    tm = min(OUT_TILE, n)
    row = lambda i: (i, 0)
    fix = lambda i: (0, 0)
    consts = (gin, wgsb, wg, bg, wosb, womla, wout, gf)
    return pl.pallas_call(
        functools.partial(_out_kernel, final_norm=final_norm),
        grid=(n // tm,),
        in_specs=[pl.BlockSpec((tm, d), row), pl.BlockSpec((tm, SB_WIDTH), row), pl.BlockSpec((tm, MLA_WIDTH), row)]
                 + [pl.BlockSpec(c.shape, fix, pipeline_mode=pl.Buffered(1)) for c in consts],
        out_specs=pl.BlockSpec((tm, d), row),
        out_shape=jax.ShapeDtypeStruct((n, d), F32),
        compiler_params=pltpu.CompilerParams(
            dimension_semantics=("arbitrary",), vmem_limit_bytes=VMEM_LIMIT_BYTES),
        name="out",
    )(x2, osb, omla, *consts)


def _rope_tables(s):
    half = MLA_ROPE_DIM // 2
    inv_freq = ROPE_THETA ** (-np.arange(half, dtype=np.float64) / half)
    ang = np.arange(s, dtype=np.float64)[:, None] * inv_freq[None, :]
    cos, sin = np.cos(ang), np.sin(ang)
    lanes = lambda lo, hi: np.concatenate(
        [np.zeros((s, MLA_NOPE_DIM)), lo, hi, np.zeros((s, LANES - MLA_QK_DIM))], axis=1).astype(np.float32)
    return jnp.asarray(lanes(cos, cos)), jnp.asarray(lanes(-sin, sin))


def _layer_weights(w_in, w_q_up, w_kv_up):
    d = w_in.shape[0]
    o_v, o_gsb, o_cq = 2 * SB_WIDTH, 3 * SB_WIDTH, 4 * SB_WIDTH
    o_ckv = o_cq + MLA_Q_LORA
    o_kr = o_ckv + MLA_KV_LORA
    o_gm = o_kr + MLA_ROPE_DIM
    cols = lambda lo, hi: w_in[:, lo:hi].astype(BF16)
    zeros = lambda r, w: jnp.zeros((r, w), BF16)
    wqk = cols(0, o_v)
    wvsb_t = cols(o_v, o_gsb).T
    wgsb = cols(o_gsb, o_cq)
    wc = jnp.concatenate([cols(o_cq, o_ckv), zeros(d, MLA_NOPE_DIM), cols(o_kr, o_gm),
                          zeros(d, LANES - MLA_QK_DIM)], axis=1)
    wckv = cols(o_ckv, o_kr)
    wg = cols(o_gm, w_in.shape[1])

    rq = w_q_up.shape[0]
    wq = w_q_up.astype(BF16).reshape(rq, MLA_HEADS, MLA_QK_DIM)
    wq = jnp.concatenate([wq, jnp.zeros((rq, MLA_HEADS, LANES - MLA_QK_DIM), BF16)], axis=-1)
    rkv = w_kv_up.shape[0]
    wkv = w_kv_up.astype(BF16).reshape(rkv, MLA_HEADS, MLA_NOPE_DIM + MLA_V_DIM)
    wkn = jnp.concatenate([wkv[..., :MLA_NOPE_DIM], jnp.zeros((rkv, MLA_HEADS, LANES - MLA_NOPE_DIM), BF16)], axis=-1)
    wv_t = wkv[..., MLA_NOPE_DIM:].reshape(rkv, MLA_WIDTH).T
    return dict(wqk=wqk, wvsb_t=wvsb_t, wgsb=wgsb, wc=wc, wckv=wckv, wg=wg, wq=wq.reshape(rq, MLA_HEADS * LANES),
                wkn=wkn.reshape(rkv, MLA_HEADS * LANES), wv=wv_t)


def kernel(x, norm_in_g, w_in, b_gate, q_norm_g, w_q_up, kv_norm_g, w_kv_up, w_o_sb, w_o_mla, w_out, norm_f_g):
    b, s, d = x.shape
    depth = w_in.shape[0]
    assert s % ATT_BLOCK == 0 and s % SB_BLOCK == 0 and TOKEN_TILE == ATT_BLOCK
    tabs = _rope_tables(s)
    idx = jnp.arange(SB_BLOCK)
    from_mat = (idx[None, :] >= idx[:, None]).astype(BF16)

    x2 = x.reshape(b * s, d)
    for l in range(depth):
        w = _layer_weights(w_in[l], w_q_up[l], w_kv_up[l])
        gin = norm_in_g[l][None, :]
        qsb, ksb, vsbt, qf, kf, vt = _proj_call(
            x2, gin,
            (w["wqk"], w["wvsb_t"], w["wc"], w["wckv"], q_norm_g[l][None, :], kv_norm_g[l][None, :], w["wq"], w["wkn"],
             w["wv"]),
            tabs)
        r3 = lambda a: a.reshape(b, s, a.shape[-1])
        vsbt = vsbt.reshape(b, s // SB_BLOCK, SB_WIDTH, SB_BLOCK)
        vt = vt.reshape((b, s // ATT_BLOCK) + vt.shape[1:])
        o_sb = _sb_call(r3(qsb), r3(ksb), vsbt, from_mat).reshape(b * s, SB_WIDTH)
        o_mla = _mla_call(r3(qf), r3(kf), vt).reshape(b * s, MLA_WIDTH)
        x2 = _out_call(x2, o_sb, o_mla, gin, w["wgsb"], w["wg"], b_gate[l][None, :], w_o_sb[l].astype(BF16),
                       w_o_mla[l].astype(BF16), w_out[l].astype(BF16), norm_f_g[None, :],
                       final_norm=(l == depth - 1))
    return x2.reshape(b, s, d)
```

```python
import functools

import jax
import jax.numpy as jnp
import numpy as np
from jax import lax
from jax.experimental import pallas as pl
from jax.experimental.pallas import tpu as pltpu

EPS = 1e-6
CHUNK = 64

SB_HEADS = 8
SB_HEAD_DIM = 64
SB_WIDTH = SB_HEADS * SB_HEAD_DIM

MLA_HEADS = 8
MLA_NOPE_DIM = 64
MLA_ROPE_DIM = 32
MLA_V_DIM = 64
MLA_Q_LORA = 384
MLA_KV_LORA = 256
MLA_QK_DIM = MLA_NOPE_DIM + MLA_ROPE_DIM
MLA_WIDTH = MLA_HEADS * MLA_V_DIM
ROPE_THETA = 10000.0

LANES = 128
ATT_BLOCK = 512
SB_BLOCK = 256
SB_DEAD_LOG2 = 160.0
LOG2_E = float(np.log2(np.e))
HEADS_PER_STEP = 8
V_ROWS = 80
TOKEN_TILE = 512
OUT_TILE = 1024
VMEM_LIMIT_BYTES = 56 * 1024 * 1024

F32 = jnp.float32
BF16 = jnp.bfloat16
NT_DIMS = (((1,), (1,)), ((), ()))


def _dot(a, b):
    return jnp.dot(a, b, preferred_element_type=F32)


def _rms(x, g):
    inv = lax.rsqrt(jnp.mean(x * x, axis=-1, keepdims=True) + EPS)
    return (x * inv) * g


def _proj_kernel(x_ref, gin_ref, wqk_ref, wvsb_ref, wc_ref, wckv_ref, qng_ref, kvng_ref, wq_ref, wkn_ref, wv_ref,
                 cos_ref, sin_ref, qsb_ref, ksb_ref, vsbt_ref, qf_ref, kf_ref, vt_ref):
    h = _rms(x_ref[...], gin_ref[...]).astype(BF16)

    c = _dot(h, wc_ref[...])
    ckv = _dot(h, wckv_ref[...])
    qsb_ref[...] = (_dot(h, wqk_ref[:, :SB_WIDTH]) * (SB_HEAD_DIM ** -0.5 * LOG2_E)).astype(BF16)
    cq = _rms(c[:, :MLA_Q_LORA], qng_ref[...]).astype(BF16)
    ckv = _rms(ckv, kvng_ref[...]).astype(BF16)

    def rope(v, cos, sin_lo, sin_hi):
        return (v * cos + pltpu.roll(v, LANES - MLA_ROPE_DIM // 2, 1) * sin_lo
                + pltpu.roll(v, MLA_ROPE_DIM // 2, 1) * sin_hi)

    lane = lax.broadcasted_iota(jnp.int32, cos_ref.shape, 1)
    ck_t, sin_t = cos_ref[...], sin_ref[...]
    sk_lo = jnp.where(lane < MLA_NOPE_DIM + MLA_ROPE_DIM // 2, sin_t, 0.0)
    sk_hi = sin_t - sk_lo
    scale = MLA_QK_DIM ** -0.5 * np.log2(np.e)
    cq_t = jnp.where(lane < MLA_NOPE_DIM, scale, ck_t * scale)
    sq_lo, sq_hi = sk_lo * scale, sk_hi * scale

    k_rope = rope(c[:, MLA_Q_LORA:], ck_t, sk_lo, sk_hi)
    q = _dot(cq, wq_ref[...])
    kn = _dot(ckv, wkn_ref[...])
    v_t = lax.dot_general(wv_ref[...], ckv, NT_DIMS, preferred_element_type=F32)
    pad_rows = (lax.broadcasted_iota(jnp.int32, (V_ROWS - MLA_V_DIM, v_t.shape[1]), 0) == 0).astype(BF16)
    for hd in range(MLA_HEADS):
        vt_ref[hd, :MLA_V_DIM, :] = v_t[hd * MLA_V_DIM:(hd + 1) * MLA_V_DIM, :].astype(BF16)
        vt_ref[hd, MLA_V_DIM:, :] = pad_rows
    ksb_ref[...] = _dot(h, wqk_ref[:, SB_WIDTH:]).astype(BF16)
    v_sb_t = lax.dot_general(wvsb_ref[...], h, NT_DIMS, preferred_element_type=F32)
    for half in range(v_sb_t.shape[1] // SB_BLOCK):
        vsbt_ref[half] = v_sb_t[:, half * SB_BLOCK:(half + 1) * SB_BLOCK].astype(BF16)
    for hd in range(MLA_HEADS):
        sl = slice(hd * LANES, (hd + 1) * LANES)
        qf_ref[:, sl] = rope(q[:, sl], cq_t, sq_lo, sq_hi).astype(BF16)
        kf_ref[:, sl] = (kn[:, sl] + k_rope).astype(BF16)


def _proj_call(x2, gin, wts, tabs):
    n, d = x2.shape
    s = tabs[0].shape[0]
    tm = min(TOKEN_TILE, s)
    row = lambda i: (i, 0)
    fix = lambda i: (0, 0)
    w_specs = [pl.BlockSpec(w.shape, fix) for w in wts]
    t_specs = [pl.BlockSpec((tm, LANES), lambda i: (i % (s // tm), 0)) for _ in tabs]
    vsbt_spec = pl.BlockSpec((None, tm // SB_BLOCK, SB_WIDTH, SB_BLOCK), lambda i: (i, 0, 0, 0))
    vsbt_shape = jax.ShapeDtypeStruct((n // tm, tm // SB_BLOCK, SB_WIDTH, SB_BLOCK), BF16)
    vt_spec = pl.BlockSpec((None, MLA_HEADS, V_ROWS, tm), lambda i: (i, 0, 0, 0))
    vt_shape = jax.ShapeDtypeStruct((n // tm, MLA_HEADS, V_ROWS, tm), BF16)
    tile = lambda w: pl.BlockSpec((tm, w), row)
    sds = lambda w: jax.ShapeDtypeStruct((n, w), BF16)
    return pl.pallas_call(
        _proj_kernel,
        grid=(n // tm,),
        in_specs=[pl.BlockSpec((tm, d), row), pl.BlockSpec(gin.shape, fix)] + w_specs + t_specs,
        out_specs=[tile(SB_WIDTH), tile(SB_WIDTH), vsbt_spec, tile(MLA_HEADS * LANES), tile(MLA_HEADS * LANES),
                   vt_spec],
        out_shape=[sds(SB_WIDTH), sds(SB_WIDTH), vsbt_shape, sds(MLA_HEADS * LANES), sds(MLA_HEADS * LANES),
                   vt_shape],
        compiler_params=pltpu.CompilerParams(
            dimension_semantics=("arbitrary",), vmem_limit_bytes=VMEM_LIMIT_BYTES),
        name="proj",
    )(x2, gin, *wts, *tabs)


def _sb_kernel(q_ref, k_ref, vt_ref, u_ref, o_ref, acc_ref):
    t = SB_BLOCK
    qi = pl.program_id(1)

    first = lax.broadcasted_iota(jnp.int32, (LANES, t), 0) < SB_HEAD_DIM
    q_heads = []
    for p in range(SB_HEADS // 2):
        q_t = q_ref[:, p * LANES:(p + 1) * LANES].astype(F32).T
        q_heads += [jnp.where(first, q_t, 0.0).astype(BF16), jnp.where(first, 0.0, q_t).astype(BF16)]

    def block(j, r, strict):
        k_blk = k_ref[pl.ds(pl.multiple_of(j * t, t), t), :]
        heads = range(SB_HEADS)
        z = [_dot(k_blk[:, (hd // 2) * LANES:(hd // 2 + 1) * LANES], q_heads[hd]) for hd in heads]
        log_beta, later, first_term = [], [], []
        for hd in heads:
            neg_log_1m = jnp.maximum(z[hd], 0.0) + jnp.log(1.0 + jnp.exp2(-jnp.abs(z[hd]))) * LOG2_E
            log_beta.append(z[hd] - neg_log_1m)
            if strict is not None:
                neg_log_1m = jnp.where(strict, neg_log_1m, 0.0)
            neg_log_1m = neg_log_1m.astype(BF16)
            later.append(_dot(u_ref[...], neg_log_1m))
            first_term.append(neg_log_1m[0:1, :].astype(F32))
        for hd in heads:
            w = jnp.exp2(log_beta[hd] - later[hd] + r[hd])
            if strict is not None:
                w = jnp.where(strict, w, 0.0)
            vt_blk = vt_ref[j, hd * SB_HEAD_DIM:(hd + 1) * SB_HEAD_DIM, :]
            acc_ref[hd] += _dot(vt_blk, w.astype(BF16))
        return tuple(r[hd] - later[hd][0:1, :] - first_term[hd] for hd in heads)

    def r_max(r):
        return functools.reduce(jnp.maximum, [jnp.max(x) for x in r])

    acc_ref[...] = jnp.zeros_like(acc_ref)
    zero = jnp.zeros((1, t), F32)
    key_pos = lax.broadcasted_iota(jnp.int32, (t, t), 0)
    qry_pos = lax.broadcasted_iota(jnp.int32, (t, t), 1)
    r = block(qi, (zero,) * SB_HEADS, key_pos < qry_pos)

    def cond(c):
        j, rmax, _ = c
        return jnp.logical_and(j >= 0, rmax > -SB_DEAD_LOG2)

    def body(c):
        j, _, r = c
        r = block(j, r, None)
        return j - 1, r_max(r), r

    lax.while_loop(cond, body, (qi - 1, r_max(r), r))

    o_ref[...] = jnp.concatenate([acc_ref[hd] for hd in range(SB_HEADS)], axis=0).T


def _sb_call(q, k, vt, u):
    b, s, w = q.shape
    t = SB_BLOCK
    nblk = s // t
    assert vt.shape == (b, nblk, w, t)
    resident = lambda shape: pl.BlockSpec((None,) + shape, lambda bi, qi: (bi,) + (0,) * len(shape),
                                          pipeline_mode=pl.Buffered(1))
    return pl.pallas_call(
        _sb_kernel,
        grid=(b, nblk),
        in_specs=[
            pl.BlockSpec((None, t, w), lambda bi, qi: (bi, qi, 0)),
            resident((s, w)),
            resident((nblk, w, t)),
            pl.BlockSpec(u.shape, lambda bi, qi: (0, 0)),
        ],
        out_specs=pl.BlockSpec((None, t, w), lambda bi, qi: (bi, qi, 0)),
        out_shape=jax.ShapeDtypeStruct((b, s, w), F32),
        scratch_shapes=[pltpu.VMEM((SB_HEADS, SB_HEAD_DIM, t), F32)],
        compiler_params=pltpu.CompilerParams(
            dimension_semantics=("arbitrary", "arbitrary"), vmem_limit_bytes=VMEM_LIMIT_BYTES),
        name="sb_attn",
    )(q, k, vt, u)


def _mla_kernel(q_ref, k_ref, vt_ref, o_ref, acc_ref, m_ref, s0_ref, s1_ref, smax0_ref, smax1_ref):
    t = ATT_BLOCK
    qi = pl.program_id(1)

    q_heads = [q_ref[:, hd * LANES:(hd + 1) * LANES].astype(F32).T.astype(BF16) for hd in range(HEADS_PER_STEP)]

    def scores(hd, j, buf, allowed):
        s_ref, smax_ref = buf
        k_blk = k_ref[pl.ds(pl.multiple_of(j * t, t), t), hd * LANES:(hd + 1) * LANES]
        s = _dot(k_blk, q_heads[hd])
        if allowed is not None:
            s = jnp.where(allowed, s, -jnp.inf)
        s_ref[hd] = s
        smax_ref[hd] = jnp.max(s, axis=0, keepdims=True)

    def update(hd, j, buf):
        s_ref, smax_ref = buf
        m = m_ref[hd]
        m_new = jnp.maximum(m, smax_ref[hd])
        m_ref[hd] = m_new
        p = jnp.exp2(s_ref[hd] - m_new).astype(BF16)
        acc_ref[hd] = jnp.exp2(m - m_new) * acc_ref[hd] + _dot(vt_ref[j, hd], p)

    acc_ref[...] = jnp.zeros_like(acc_ref)
    m_ref[...] = jnp.full_like(m_ref, -jnp.inf)
    key_chunk = lax.broadcasted_iota(jnp.int32, (t, t), 0) // CHUNK
    qry_chunk = lax.broadcasted_iota(jnp.int32, (t, t), 1) // CHUNK
    bufs = ((s0_ref, smax0_ref), (s1_ref, smax1_ref))

    block_of = lambda k: jnp.where(k == 0, qi, k - 1)
    allowed = key_chunk <= qry_chunk
    for hd in range(HEADS_PER_STEP):
        scores(hd, qi, bufs[0], allowed)

    def body(k, carry):
        for parity in range(2):
            @pl.when(k % 2 == parity)
            def _():
                for hd in range(HEADS_PER_STEP):
                    scores(hd, block_of(k + 1), bufs[1 - parity], None)
                    update(hd, block_of(k), bufs[parity])
        return carry

    lax.fori_loop(0, qi, body, 0)

    for parity in range(2):
        @pl.when(qi % 2 == parity)
        def _():
            for hd in range(HEADS_PER_STEP):
                update(hd, block_of(qi), bufs[parity])

    o_ref[...] = jnp.concatenate(
        [acc_ref[hd, :MLA_V_DIM, :] / acc_ref[hd, MLA_V_DIM:MLA_V_DIM + 1, :] for hd in range(HEADS_PER_STEP)],
        axis=0).T


def _mla_call(qf, kf, vt):
    b, s, w = qf.shape
    t = ATT_BLOCK
    nblk = s // t
    assert vt.shape == (b, nblk, MLA_HEADS, V_ROWS, t) and HEADS_PER_STEP == MLA_HEADS
    resident = lambda shape: pl.BlockSpec((None,) + shape, lambda bi, qi: (bi,) + (0,) * len(shape),
                                          pipeline_mode=pl.Buffered(1))
    return pl.pallas_call(
        _mla_kernel,
        grid=(b, nblk),
        in_specs=[
            pl.BlockSpec((None, t, w), lambda bi, qi: (bi, qi, 0)),
            resident((s, w)),
            resident((nblk, MLA_HEADS, V_ROWS, t)),
        ],
        out_specs=pl.BlockSpec((None, t, MLA_WIDTH), lambda bi, qi: (bi, qi, 0)),
        out_shape=jax.ShapeDtypeStruct((b, s, MLA_WIDTH), F32),
        scratch_shapes=[
            pltpu.VMEM((HEADS_PER_STEP, V_ROWS, t), F32),
            pltpu.VMEM((HEADS_PER_STEP, 1, t), F32),
            pltpu.VMEM((HEADS_PER_STEP, t, t), F32),
            pltpu.VMEM((HEADS_PER_STEP, t, t), F32),
            pltpu.VMEM((HEADS_PER_STEP, 1, t), F32),
            pltpu.VMEM((HEADS_PER_STEP, 1, t), F32),
        ],
        compiler_params=pltpu.CompilerParams(
            dimension_semantics=("arbitrary", "arbitrary"), vmem_limit_bytes=VMEM_LIMIT_BYTES),
        name="mla_attn",
    )(qf, kf, vt)


def _out_kernel(x_ref, osb_ref, omla_ref, gin_ref, wgsb_ref, wg_ref, bg_ref, wosb_ref, womla_ref, wout_ref,
                gf_ref, o_ref, *, final_norm):
    x = x_ref[...]
    d = x.shape[-1]
    h = _rms(x, gin_ref[...]).astype(BF16)
    a_sb = (osb_ref[...] * jax.nn.silu(_dot(h, wgsb_ref[...]))).astype(BF16)
    a_mla = (omla_ref[...] * jax.nn.silu(_dot(h, wg_ref[:, :MLA_WIDTH]))).astype(BF16)
    g0 = MLA_WIDTH
    g_sb = jax.nn.sigmoid(_dot(h, wg_ref[:, g0:g0 + d]) + bg_ref[:, :d])
    merged = g_sb * _dot(a_sb, wosb_ref[...])
    g_mla = jax.nn.sigmoid(_dot(h, wg_ref[:, g0 + d:g0 + 2 * d]) + bg_ref[:, d:])
    merged = merged + g_mla * _dot(a_mla, womla_ref[...])
    y = x + _dot(merged.astype(BF16), wout_ref[...])
    o_ref[...] = _rms(y, gf_ref[...]) if final_norm else y


def _out_call(x2, osb, omla, gin, wgsb, wg, bg, wosb, womla, wout, gf, final_norm):
    n, d = x2.shape
    tm = min(OUT_TILE, n)
    row = lambda i: (i, 0)
    fix = lambda i: (0, 0)
    consts = (gin, wgsb, wg, bg, wosb, womla, wout, gf)
    return pl.pallas_call(
        functools.partial(_out_kernel, final_norm=final_norm),
        grid=(n // tm,),
        in_specs=[pl.BlockSpec((tm, d), row), pl.BlockSpec((tm, SB_WIDTH), row), pl.BlockSpec((tm, MLA_WIDTH), row)]
                 + [pl.BlockSpec(c.shape, fix, pipeline_mode=pl.Buffered(1)) for c in consts],
        out_specs=pl.BlockSpec((tm, d), row),
        out_shape=jax.ShapeDtypeStruct((n, d), F32),
        compiler_params=pltpu.CompilerParams(
            dimension_semantics=("arbitrary",), vmem_limit_bytes=VMEM_LIMIT_BYTES),
        name="out",
    )(x2, osb, omla, *consts)


def _rope_tables(s):
    half = MLA_ROPE_DIM // 2
    inv_freq = ROPE_THETA ** (-np.arange(half, dtype=np.float64) / half)
    ang = np.arange(s, dtype=np.float64)[:, None] * inv_freq[None, :]
    cos, sin = np.cos(ang), np.sin(ang)
    lanes = lambda lo, hi: np.concatenate(
        [np.zeros((s, MLA_NOPE_DIM)), lo, hi, np.zeros((s, LANES - MLA_QK_DIM))], axis=1).astype(np.float32)
    return jnp.asarray(lanes(cos, cos)), jnp.asarray(lanes(-sin, sin))


def _layer_weights(w_in, w_q_up, w_kv_up):
    d = w_in.shape[0]
    o_v, o_gsb, o_cq = 2 * SB_WIDTH, 3 * SB_WIDTH, 4 * SB_WIDTH
    o_ckv = o_cq + MLA_Q_LORA
    o_kr = o_ckv + MLA_KV_LORA
    o_gm = o_kr + MLA_ROPE_DIM
    cols = lambda lo, hi: w_in[:, lo:hi].astype(BF16)
    zeros = lambda r, w: jnp.zeros((r, w), BF16)
    wqk = cols(0, o_v)
    wvsb_t = cols(o_v, o_gsb).T
    wgsb = cols(o_gsb, o_cq)
    wc = jnp.concatenate([cols(o_cq, o_ckv), zeros(d, MLA_NOPE_DIM), cols(o_kr, o_gm),
                          zeros(d, LANES - MLA_QK_DIM)], axis=1)
    wckv = cols(o_ckv, o_kr)
    wg = cols(o_gm, w_in.shape[1])

    rq = w_q_up.shape[0]
    wq = w_q_up.astype(BF16).reshape(rq, MLA_HEADS, MLA_QK_DIM)
    wq = jnp.concatenate([wq, jnp.zeros((rq, MLA_HEADS, LANES - MLA_QK_DIM), BF16)], axis=-1)
    rkv = w_kv_up.shape[0]
    wkv = w_kv_up.astype(BF16).reshape(rkv, MLA_HEADS, MLA_NOPE_DIM + MLA_V_DIM)
    wkn = jnp.concatenate([wkv[..., :MLA_NOPE_DIM], jnp.zeros((rkv, MLA_HEADS, LANES - MLA_NOPE_DIM), BF16)], axis=-1)
    wv_t = wkv[..., MLA_NOPE_DIM:].reshape(rkv, MLA_WIDTH).T
    return dict(wqk=wqk, wvsb_t=wvsb_t, wgsb=wgsb, wc=wc, wckv=wckv, wg=wg, wq=wq.reshape(rq, MLA_HEADS * LANES),
                wkn=wkn.reshape(rkv, MLA_HEADS * LANES), wv=wv_t)


def kernel(x, norm_in_g, w_in, b_gate, q_norm_g, w_q_up, kv_norm_g, w_kv_up, w_o_sb, w_o_mla, w_out, norm_f_g):
    b, s, d = x.shape
    depth = w_in.shape[0]
    assert s % ATT_BLOCK == 0 and s % SB_BLOCK == 0 and TOKEN_TILE == ATT_BLOCK
    tabs = _rope_tables(s)
    idx = jnp.arange(SB_BLOCK)
    after_mat = (idx[None, :] > idx[:, None]).astype(BF16)

    x2 = x.reshape(b * s, d)
    for l in range(depth):
        w = _layer_weights(w_in[l], w_q_up[l], w_kv_up[l])
        gin = norm_in_g[l][None, :]
        qsb, ksb, vsbt, qf, kf, vt = _proj_call(
            x2, gin,
            (w["wqk"], w["wvsb_t"], w["wc"], w["wckv"], q_norm_g[l][None, :], kv_norm_g[l][None, :], w["wq"], w["wkn"],
             w["wv"]),
            tabs)
        r3 = lambda a: a.reshape(b, s, a.shape[-1])
        vsbt = vsbt.reshape(b, s // SB_BLOCK, SB_WIDTH, SB_BLOCK)
        vt = vt.reshape((b, s // ATT_BLOCK) + vt.shape[1:])
        o_sb = _sb_call(r3(qsb), r3(ksb), vsbt, after_mat).reshape(b * s, SB_WIDTH)
        o_mla = _mla_call(r3(qf), r3(kf), vt).reshape(b * s, MLA_WIDTH)
        x2 = _out_call(x2, o_sb, o_mla, gin, w["wgsb"], w["wg"], b_gate[l][None, :], w_o_sb[l].astype(BF16),
                       w_o_mla[l].astype(BF16), w_out[l].astype(BF16), norm_f_g[None, :],
                       final_norm=(l == depth - 1))
    return x2.reshape(b, s, d)
```

```python
import functools

import jax
import jax.numpy as jnp
import numpy as np
from jax import lax
from jax.experimental import pallas as pl
from jax.experimental.pallas import tpu as pltpu

EPS = 1e-6
CHUNK = 64

SB_HEADS = 8
SB_HEAD_DIM = 64
SB_WIDTH = SB_HEADS * SB_HEAD_DIM

MLA_HEADS = 8
MLA_NOPE_DIM = 64
MLA_ROPE_DIM = 32
MLA_V_DIM = 64
MLA_Q_LORA = 384
MLA_KV_LORA = 256
MLA_QK_DIM = MLA_NOPE_DIM + MLA_ROPE_DIM
MLA_WIDTH = MLA_HEADS * MLA_V_DIM
ROPE_THETA = 10000.0

LANES = 128
ATT_BLOCK = 512
SB_BLOCK = 256
SB_DEAD_LOG2 = 160.0
LOG2_E = float(np.log2(np.e))
HEADS_PER_STEP = 8
V_ROWS = 80
TOKEN_TILE = 512
OUT_TILE = 1024
VMEM_LIMIT_BYTES = 56 * 1024 * 1024

F32 = jnp.float32
BF16 = jnp.bfloat16
NT_DIMS = (((1,), (1,)), ((), ()))


def _dot(a, b):
    return jnp.dot(a, b, preferred_element_type=F32)


def _rms(x, g):
    inv = lax.rsqrt(jnp.mean(x * x, axis=-1, keepdims=True) + EPS)
    return (x * inv) * g


def _proj_kernel(x_ref, gin_ref, wqk_ref, wvsb_ref, wc_ref, wckv_ref, qng_ref, kvng_ref, wq_ref, wkn_ref, wv_ref,
                 cos_ref, sin_ref, qsb_ref, ksb_ref, vsbt_ref, qf_ref, kf_ref, vt_ref):
    h = _rms(x_ref[...], gin_ref[...]).astype(BF16)

    c = _dot(h, wc_ref[...])
    ckv = _dot(h, wckv_ref[...])
    qsb_ref[...] = (_dot(h, wqk_ref[:, :SB_WIDTH]) * (SB_HEAD_DIM ** -0.5 * LOG2_E)).astype(BF16)
    cq = _rms(c[:, :MLA_Q_LORA], qng_ref[...]).astype(BF16)
    ckv = _rms(ckv, kvng_ref[...]).astype(BF16)

    def rope(v, cos, sin_lo, sin_hi):
        return (v * cos + pltpu.roll(v, LANES - MLA_ROPE_DIM // 2, 1) * sin_lo
                + pltpu.roll(v, MLA_ROPE_DIM // 2, 1) * sin_hi)

    lane = lax.broadcasted_iota(jnp.int32, cos_ref.shape, 1)
    ck_t, sin_t = cos_ref[...], sin_ref[...]
    sk_lo = jnp.where(lane < MLA_NOPE_DIM + MLA_ROPE_DIM // 2, sin_t, 0.0)
    sk_hi = sin_t - sk_lo
    scale = MLA_QK_DIM ** -0.5 * np.log2(np.e)
    cq_t = jnp.where(lane < MLA_NOPE_DIM, scale, ck_t * scale)
    sq_lo, sq_hi = sk_lo * scale, sk_hi * scale

    k_rope = rope(c[:, MLA_Q_LORA:], ck_t, sk_lo, sk_hi)
    q = _dot(cq, wq_ref[...])
    kn = _dot(ckv, wkn_ref[...])
    v_t = lax.dot_general(wv_ref[...], ckv, NT_DIMS, preferred_element_type=F32)
    pad_rows = (lax.broadcasted_iota(jnp.int32, (V_ROWS - MLA_V_DIM, v_t.shape[1]), 0) == 0).astype(BF16)
    for hd in range(MLA_HEADS):
        vt_ref[hd, :MLA_V_DIM, :] = v_t[hd * MLA_V_DIM:(hd + 1) * MLA_V_DIM, :].astype(BF16)
        vt_ref[hd, MLA_V_DIM:, :] = pad_rows
    ksb_ref[...] = _dot(h, wqk_ref[:, SB_WIDTH:]).astype(BF16)
    v_sb_t = lax.dot_general(wvsb_ref[...], h, NT_DIMS, preferred_element_type=F32)
    for half in range(v_sb_t.shape[1] // SB_BLOCK):
        vsbt_ref[half] = v_sb_t[:, half * SB_BLOCK:(half + 1) * SB_BLOCK].astype(BF16)
    for hd in range(MLA_HEADS):
        sl = slice(hd * LANES, (hd + 1) * LANES)
        qf_ref[:, sl] = rope(q[:, sl], cq_t, sq_lo, sq_hi).astype(BF16)
        kf_ref[:, sl] = (kn[:, sl] + k_rope).astype(BF16)


def _proj_call(x2, gin, wts, tabs):
    n, d = x2.shape
    s = tabs[0].shape[0]
    tm = min(TOKEN_TILE, s)
    row = lambda i: (i, 0)
    fix = lambda i: (0, 0)
    w_specs = [pl.BlockSpec(w.shape, fix) for w in wts]
    t_specs = [pl.BlockSpec((tm, LANES), lambda i: (i % (s // tm), 0)) for _ in tabs]
    vsbt_spec = pl.BlockSpec((None, tm // SB_BLOCK, SB_WIDTH, SB_BLOCK), lambda i: (i, 0, 0, 0))
    vsbt_shape = jax.ShapeDtypeStruct((n // tm, tm // SB_BLOCK, SB_WIDTH, SB_BLOCK), BF16)
    vt_spec = pl.BlockSpec((None, MLA_HEADS, V_ROWS, tm), lambda i: (i, 0, 0, 0))
    vt_shape = jax.ShapeDtypeStruct((n // tm, MLA_HEADS, V_ROWS, tm), BF16)
    tile = lambda w: pl.BlockSpec((tm, w), row)
    sds = lambda w: jax.ShapeDtypeStruct((n, w), BF16)
    return pl.pallas_call(
        _proj_kernel,
        grid=(n // tm,),
        in_specs=[pl.BlockSpec((tm, d), row), pl.BlockSpec(gin.shape, fix)] + w_specs + t_specs,
        out_specs=[tile(SB_WIDTH), tile(SB_WIDTH), vsbt_spec, tile(MLA_HEADS * LANES), tile(MLA_HEADS * LANES),
                   vt_spec],
        out_shape=[sds(SB_WIDTH), sds(SB_WIDTH), vsbt_shape, sds(MLA_HEADS * LANES), sds(MLA_HEADS * LANES),
                   vt_shape],
        compiler_params=pltpu.CompilerParams(
            dimension_semantics=("arbitrary",), vmem_limit_bytes=VMEM_LIMIT_BYTES),
        name="proj",
    )(x2, gin, *wts, *tabs)


def _sb_kernel(q_ref, k_ref, vt_ref, u_ref, o_ref, acc_ref):
    t = SB_BLOCK
    qi = pl.program_id(1)

    first = lax.broadcasted_iota(jnp.int32, (LANES, t), 0) < SB_HEAD_DIM
    q_heads = []
    for p in range(SB_HEADS // 2):
        q_t = q_ref[:, p * LANES:(p + 1) * LANES].astype(F32).T
        q_heads += [jnp.where(first, q_t, 0.0).astype(BF16), jnp.where(first, 0.0, q_t).astype(BF16)]

    def block(j, r, strict):
        k_blk = k_ref[pl.ds(pl.multiple_of(j * t, t), t), :]
        heads = range(SB_HEADS)
        def stage_scores(hd):
            return _dot(k_blk[:, (hd // 2) * LANES:(hd // 2 + 1) * LANES], q_heads[hd])

        def stage_sums(z):
            neg_log_1m = jnp.maximum(z, 0.0) + jnp.log(1.0 + jnp.exp2(-jnp.abs(z))) * LOG2_E
            log_beta = z - neg_log_1m
            if strict is not None:
                neg_log_1m = jnp.where(strict, neg_log_1m, 0.0)
            neg_log_1m = neg_log_1m.astype(BF16)
            return log_beta, _dot(u_ref[...], neg_log_1m), neg_log_1m[0:1, :].astype(F32)

        def stage_values(hd, log_beta, later, first_term):
            w = jnp.exp2(log_beta - later + r[hd])
            if strict is not None:
                w = jnp.where(strict, w, 0.0)
            vt_blk = vt_ref[j, hd * SB_HEAD_DIM:(hd + 1) * SB_HEAD_DIM, :]
            acc_ref[hd] += _dot(vt_blk, w.astype(BF16))
            return r[hd] - later[0:1, :] - first_term

        n = SB_HEADS
        z, sums, r_new = {}, {}, []
        for step in range(n + 2):
            if step < n:
                z[step] = stage_scores(step)
            if 0 <= step - 1 < n:
                sums[step - 1] = stage_sums(z.pop(step - 1))
            if 0 <= step - 2 < n:
                r_new.append(stage_values(step - 2, *sums.pop(step - 2)))
        return tuple(r_new)

    def r_max(r):
        return functools.reduce(jnp.maximum, [jnp.max(x) for x in r])

    acc_ref[...] = jnp.zeros_like(acc_ref)
    zero = jnp.zeros((1, t), F32)
    key_pos = lax.broadcasted_iota(jnp.int32, (t, t), 0)
    qry_pos = lax.broadcasted_iota(jnp.int32, (t, t), 1)
    r = block(qi, (zero,) * SB_HEADS, key_pos < qry_pos)

    def cond(c):
        j, rmax, _ = c
        return jnp.logical_and(j >= 0, rmax > -SB_DEAD_LOG2)

    def body(c):
        j, _, r = c
        r = block(j, r, None)
        return j - 1, r_max(r), r

    lax.while_loop(cond, body, (qi - 1, r_max(r), r))

    o_ref[...] = jnp.concatenate([acc_ref[hd] for hd in range(SB_HEADS)], axis=0).T


def _sb_call(q, k, vt, u):
    b, s, w = q.shape
    t = SB_BLOCK
    nblk = s // t
    assert vt.shape == (b, nblk, w, t)
    resident = lambda shape: pl.BlockSpec((None,) + shape, lambda bi, qi: (bi,) + (0,) * len(shape),
                                          pipeline_mode=pl.Buffered(1))
    return pl.pallas_call(
        _sb_kernel,
        grid=(b, nblk),
        in_specs=[
            pl.BlockSpec((None, t, w), lambda bi, qi: (bi, qi, 0)),
            resident((s, w)),
            resident((nblk, w, t)),
            pl.BlockSpec(u.shape, lambda bi, qi: (0, 0)),
        ],
        out_specs=pl.BlockSpec((None, t, w), lambda bi, qi: (bi, qi, 0)),
        out_shape=jax.ShapeDtypeStruct((b, s, w), F32),
        scratch_shapes=[pltpu.VMEM((SB_HEADS, SB_HEAD_DIM, t), F32)],
        compiler_params=pltpu.CompilerParams(
            dimension_semantics=("arbitrary", "arbitrary"), vmem_limit_bytes=VMEM_LIMIT_BYTES),
        name="sb_attn",
    )(q, k, vt, u)


def _mla_kernel(q_ref, k_ref, vt_ref, o_ref, acc_ref, m_ref, s0_ref, s1_ref, smax0_ref, smax1_ref):
    t = ATT_BLOCK
    qi = pl.program_id(1)

    q_heads = [q_ref[:, hd * LANES:(hd + 1) * LANES].astype(F32).T.astype(BF16) for hd in range(HEADS_PER_STEP)]

    def scores(hd, j, buf, allowed):
        s_ref, smax_ref = buf
        k_blk = k_ref[pl.ds(pl.multiple_of(j * t, t), t), hd * LANES:(hd + 1) * LANES]
        s = _dot(k_blk, q_heads[hd])
        if allowed is not None:
            s = jnp.where(allowed, s, -jnp.inf)
        s_ref[hd] = s
        smax_ref[hd] = jnp.max(s, axis=0, keepdims=True)

    def update(hd, j, buf):
        s_ref, smax_ref = buf
        m = m_ref[hd]
        m_new = jnp.maximum(m, smax_ref[hd])
        m_ref[hd] = m_new
        p = jnp.exp2(s_ref[hd] - m_new).astype(BF16)
        acc_ref[hd] = jnp.exp2(m - m_new) * acc_ref[hd] + _dot(vt_ref[j, hd], p)

    acc_ref[...] = jnp.zeros_like(acc_ref)
    m_ref[...] = jnp.full_like(m_ref, -jnp.inf)
    key_chunk = lax.broadcasted_iota(jnp.int32, (t, t), 0) // CHUNK
    qry_chunk = lax.broadcasted_iota(jnp.int32, (t, t), 1) // CHUNK
    bufs = ((s0_ref, smax0_ref), (s1_ref, smax1_ref))

    block_of = lambda k: jnp.where(k == 0, qi, k - 1)
    allowed = key_chunk <= qry_chunk
    for hd in range(HEADS_PER_STEP):
        scores(hd, qi, bufs[0], allowed)

    def body(k, carry):
        for parity in range(2):
            @pl.when(k % 2 == parity)
            def _():
                for hd in range(HEADS_PER_STEP):
                    scores(hd, block_of(k + 1), bufs[1 - parity], None)
                    update(hd, block_of(k), bufs[parity])
        return carry

    lax.fori_loop(0, qi, body, 0)

    for parity in range(2):
        @pl.when(qi % 2 == parity)
        def _():
            for hd in range(HEADS_PER_STEP):
                update(hd, block_of(qi), bufs[parity])

    o_ref[...] = jnp.concatenate(
        [acc_ref[hd, :MLA_V_DIM, :] / acc_ref[hd, MLA_V_DIM:MLA_V_DIM + 1, :] for hd in range(HEADS_PER_STEP)],
        axis=0).T


def _mla_call(qf, kf, vt):
    b, s, w = qf.shape
    t = ATT_BLOCK
    nblk = s // t
    assert vt.shape == (b, nblk, MLA_HEADS, V_ROWS, t) and HEADS_PER_STEP == MLA_HEADS
    resident = lambda shape: pl.BlockSpec((None,) + shape, lambda bi, qi: (bi,) + (0,) * len(shape),
                                          pipeline_mode=pl.Buffered(1))
    return pl.pallas_call(
        _mla_kernel,
        grid=(b, nblk),
        in_specs=[
            pl.BlockSpec((None, t, w), lambda bi, qi: (bi, qi, 0)),
            resident((s, w)),
            resident((nblk, MLA_HEADS, V_ROWS, t)),
        ],
        out_specs=pl.BlockSpec((None, t, MLA_WIDTH), lambda bi, qi: (bi, qi, 0)),
        out_shape=jax.ShapeDtypeStruct((b, s, MLA_WIDTH), F32),
        scratch_shapes=[
            pltpu.VMEM((HEADS_PER_STEP, V_ROWS, t), F32),
            pltpu.VMEM((HEADS_PER_STEP, 1, t), F32),
            pltpu.VMEM((HEADS_PER_STEP, t, t), F32),
            pltpu.VMEM((HEADS_PER_STEP, t, t), F32),
            pltpu.VMEM((HEADS_PER_STEP, 1, t), F32),
            pltpu.VMEM((HEADS_PER_STEP, 1, t), F32),
        ],
        compiler_params=pltpu.CompilerParams(
            dimension_semantics=("arbitrary", "arbitrary"), vmem_limit_bytes=VMEM_LIMIT_BYTES),
        name="mla_attn",
    )(qf, kf, vt)


def _out_kernel(x_ref, osb_ref, omla_ref, gin_ref, wgsb_ref, wg_ref, bg_ref, wosb_ref, womla_ref, wout_ref,
                gf_ref, o_ref, *, final_norm):
    x = x_ref[...]
    d = x.shape[-1]
    h = _rms(x, gin_ref[...]).astype(BF16)
    a_sb = (osb_ref[...] * jax.nn.silu(_dot(h, wgsb_ref[...]))).astype(BF16)
    a_mla = (omla_ref[...] * jax.nn.silu(_dot(h, wg_ref[:, :MLA_WIDTH]))).astype(BF16)
    g0 = MLA_WIDTH
    g_sb = jax.nn.sigmoid(_dot(h, wg_ref[:, g0:g0 + d]) + bg_ref[:, :d])
    merged = g_sb * _dot(a_sb, wosb_ref[...])
    g_mla = jax.nn.sigmoid(_dot(h, wg_ref[:, g0 + d:g0 + 2 * d]) + bg_ref[:, d:])
    merged = merged + g_mla * _dot(a_mla, womla_ref[...])
    y = x + _dot(merged.astype(BF16), wout_ref[...])
    o_ref[...] = _rms(y, gf_ref[...]) if final_norm else y


def _out_call(x2, osb, omla, gin, wgsb, wg, bg, wosb, womla, wout, gf, final_norm):
    n, d = x2.shape
    tm = min(OUT_TILE, n)
    row = lambda i: (i, 0)
    fix = lambda i: (0, 0)
    consts = (gin, wgsb, wg, bg, wosb, womla, wout, gf)
    return pl.pallas_call(
        functools.partial(_out_kernel, final_norm=final_norm),
        grid=(n // tm,),
        in_specs=[pl.BlockSpec((tm, d), row), pl.BlockSpec((tm, SB_WIDTH), row), pl.BlockSpec((tm, MLA_WIDTH), row)]
                 + [pl.BlockSpec(c.shape, fix, pipeline_mode=pl.Buffered(1)) for c in consts],
        out_specs=pl.BlockSpec((tm, d), row),
        out_shape=jax.ShapeDtypeStruct((n, d), F32),
        compiler_params=pltpu.CompilerParams(
            dimension_semantics=("arbitrary",), vmem_limit_bytes=VMEM_LIMIT_BYTES),
        name="out",
    )(x2, osb, omla, *consts)


def _rope_tables(s):
    half = MLA_ROPE_DIM // 2
    inv_freq = ROPE_THETA ** (-np.arange(half, dtype=np.float64) / half)
    ang = np.arange(s, dtype=np.float64)[:, None] * inv_freq[None, :]
    cos, sin = np.cos(ang), np.sin(ang)
    lanes = lambda lo, hi: np.concatenate(
        [np.zeros((s, MLA_NOPE_DIM)), lo, hi, np.zeros((s, LANES - MLA_QK_DIM))], axis=1).astype(np.float32)
    return jnp.asarray(lanes(cos, cos)), jnp.asarray(lanes(-sin, sin))


def _layer_weights(w_in, w_q_up, w_kv_up):
    d = w_in.shape[0]
    o_v, o_gsb, o_cq = 2 * SB_WIDTH, 3 * SB_WIDTH, 4 * SB_WIDTH
    o_ckv = o_cq + MLA_Q_LORA
    o_kr = o_ckv + MLA_KV_LORA
    o_gm = o_kr + MLA_ROPE_DIM
    cols = lambda lo, hi: w_in[:, lo:hi].astype(BF16)
    zeros = lambda r, w: jnp.zeros((r, w), BF16)
    wqk = cols(0, o_v)
    wvsb_t = cols(o_v, o_gsb).T
    wgsb = cols(o_gsb, o_cq)
    wc = jnp.concatenate([cols(o_cq, o_ckv), zeros(d, MLA_NOPE_DIM), cols(o_kr, o_gm),
                          zeros(d, LANES - MLA_QK_DIM)], axis=1)
    wckv = cols(o_ckv, o_kr)
    wg = cols(o_gm, w_in.shape[1])

    rq = w_q_up.shape[0]
    wq = w_q_up.astype(BF16).reshape(rq, MLA_HEADS, MLA_QK_DIM)
    wq = jnp.concatenate([wq, jnp.zeros((rq, MLA_HEADS, LANES - MLA_QK_DIM), BF16)], axis=-1)
    rkv = w_kv_up.shape[0]
    wkv = w_kv_up.astype(BF16).reshape(rkv, MLA_HEADS, MLA_NOPE_DIM + MLA_V_DIM)
    wkn = jnp.concatenate([wkv[..., :MLA_NOPE_DIM], jnp.zeros((rkv, MLA_HEADS, LANES - MLA_NOPE_DIM), BF16)], axis=-1)
    wv_t = wkv[..., MLA_NOPE_DIM:].reshape(rkv, MLA_WIDTH).T
    return dict(wqk=wqk, wvsb_t=wvsb_t, wgsb=wgsb, wc=wc, wckv=wckv, wg=wg, wq=wq.reshape(rq, MLA_HEADS * LANES),
                wkn=wkn.reshape(rkv, MLA_HEADS * LANES), wv=wv_t)


def kernel(x, norm_in_g, w_in, b_gate, q_norm_g, w_q_up, kv_norm_g, w_kv_up, w_o_sb, w_o_mla, w_out, norm_f_g):
    b, s, d = x.shape
    depth = w_in.shape[0]
    assert s % ATT_BLOCK == 0 and s % SB_BLOCK == 0 and TOKEN_TILE == ATT_BLOCK
    tabs = _rope_tables(s)
    idx = jnp.arange(SB_BLOCK)
    after_mat = (idx[None, :] > idx[:, None]).astype(BF16)

    x2 = x.reshape(b * s, d)
    for l in range(depth):
        w = _layer_weights(w_in[l], w_q_up[l], w_kv_up[l])
        gin = norm_in_g[l][None, :]
        qsb, ksb, vsbt, qf, kf, vt = _proj_call(
            x2, gin,
            (w["wqk"], w["wvsb_t"], w["wc"], w["wckv"], q_norm_g[l][None, :], kv_norm_g[l][None, :], w["wq"], w["wkn"],
             w["wv"]),
            tabs)
        r3 = lambda a: a.reshape(b, s, a.shape[-1])
        vsbt = vsbt.reshape(b, s // SB_BLOCK, SB_WIDTH, SB_BLOCK)
        vt = vt.reshape((b, s // ATT_BLOCK) + vt.shape[1:])
        o_sb = _sb_call(r3(qsb), r3(ksb), vsbt, after_mat).reshape(b * s, SB_WIDTH)
        o_mla = _mla_call(r3(qf), r3(kf), vt).reshape(b * s, MLA_WIDTH)
        x2 = _out_call(x2, o_sb, o_mla, gin, w["wgsb"], w["wg"], b_gate[l][None, :], w_o_sb[l].astype(BF16),
                       w_o_mla[l].astype(BF16), w_out[l].astype(BF16), norm_f_g[None, :],
                       final_norm=(l == depth - 1))
    return x2.reshape(b, s, d)
```

```python
import functools

import jax
import jax.numpy as jnp
import numpy as np
from jax import lax
from jax.experimental import pallas as pl
from jax.experimental.pallas import tpu as pltpu

EPS = 1e-6
CHUNK = 64

SB_HEADS = 8
SB_HEAD_DIM = 64
SB_WIDTH = SB_HEADS * SB_HEAD_DIM

MLA_HEADS = 8
MLA_NOPE_DIM = 64
MLA_ROPE_DIM = 32
MLA_V_DIM = 64
MLA_Q_LORA = 384
MLA_KV_LORA = 256
MLA_QK_DIM = MLA_NOPE_DIM + MLA_ROPE_DIM
MLA_WIDTH = MLA_HEADS * MLA_V_DIM
ROPE_THETA = 10000.0

LANES = 128
ATT_BLOCK = 512
SB_BLOCK = 256
SB_DEAD_LOG2 = 160.0
LOG2_E = float(np.log2(np.e))
HEADS_PER_STEP = 8
V_ROWS = 80
TOKEN_TILE = 512
OUT_TILE = 1024
VMEM_LIMIT_BYTES = 56 * 1024 * 1024

F32 = jnp.float32
BF16 = jnp.bfloat16
NT_DIMS = (((1,), (1,)), ((), ()))


def _dot(a, b):
    return jnp.dot(a, b, preferred_element_type=F32)


def _rms(x, g):
    inv = lax.rsqrt(jnp.mean(x * x, axis=-1, keepdims=True) + EPS)
    return (x * inv) * g


def _proj_kernel(x_ref, gin_ref, wqk_ref, wvsb_ref, wc_ref, wckv_ref, qng_ref, kvng_ref, wq_ref, wkn_ref, wv_ref,
                 cos_ref, sin_ref, qsb_ref, ksb_ref, vsbt_ref, qf_ref, kf_ref, vt_ref):
    h = _rms(x_ref[...], gin_ref[...]).astype(BF16)

    c = _dot(h, wc_ref[...])
    ckv = _dot(h, wckv_ref[...])
    qsb_ref[...] = (_dot(h, wqk_ref[:, :SB_WIDTH]) * (SB_HEAD_DIM ** -0.5 * LOG2_E)).astype(BF16)
    cq = _rms(c[:, :MLA_Q_LORA], qng_ref[...]).astype(BF16)
    ckv = _rms(ckv, kvng_ref[...]).astype(BF16)

    def rope(v, cos, sin_lo, sin_hi):
        return (v * cos + pltpu.roll(v, LANES - MLA_ROPE_DIM // 2, 1) * sin_lo
                + pltpu.roll(v, MLA_ROPE_DIM // 2, 1) * sin_hi)

    lane = lax.broadcasted_iota(jnp.int32, cos_ref.shape, 1)
    ck_t, sin_t = cos_ref[...], sin_ref[...]
    sk_lo = jnp.where(lane < MLA_NOPE_DIM + MLA_ROPE_DIM // 2, sin_t, 0.0)
    sk_hi = sin_t - sk_lo
    scale = MLA_QK_DIM ** -0.5 * np.log2(np.e)
    cq_t = jnp.where(lane < MLA_NOPE_DIM, scale, ck_t * scale)
    sq_lo, sq_hi = sk_lo * scale, sk_hi * scale

    k_rope = rope(c[:, MLA_Q_LORA:], ck_t, sk_lo, sk_hi)
    q = _dot(cq, wq_ref[...])
    kn = _dot(ckv, wkn_ref[...])
    v_t = lax.dot_general(wv_ref[...], ckv, NT_DIMS, preferred_element_type=F32)
    pad_rows = (lax.broadcasted_iota(jnp.int32, (V_ROWS - MLA_V_DIM, v_t.shape[1]), 0) == 0).astype(BF16)
    for hd in range(MLA_HEADS):
        vt_ref[hd, :MLA_V_DIM, :] = v_t[hd * MLA_V_DIM:(hd + 1) * MLA_V_DIM, :].astype(BF16)
        vt_ref[hd, MLA_V_DIM:, :] = pad_rows
    ksb_ref[...] = _dot(h, wqk_ref[:, SB_WIDTH:]).astype(BF16)
    v_sb_t = lax.dot_general(wvsb_ref[...], h, NT_DIMS, preferred_element_type=F32)
    for half in range(v_sb_t.shape[1] // SB_BLOCK):
        vsbt_ref[half] = v_sb_t[:, half * SB_BLOCK:(half + 1) * SB_BLOCK].astype(BF16)
    for hd in range(MLA_HEADS):
        sl = slice(hd * LANES, (hd + 1) * LANES)
        qf_ref[:, sl] = rope(q[:, sl], cq_t, sq_lo, sq_hi).astype(BF16)
        kf_ref[:, sl] = (kn[:, sl] + k_rope).astype(BF16)


def _proj_call(x2, gin, wts, tabs):
    n, d = x2.shape
    s = tabs[0].shape[0]
    tm = min(TOKEN_TILE, s)
    row = lambda i: (i, 0)
    fix = lambda i: (0, 0)
    w_specs = [pl.BlockSpec(w.shape, fix) for w in wts]
    t_specs = [pl.BlockSpec((tm, LANES), lambda i: (i % (s // tm), 0)) for _ in tabs]
    vsbt_spec = pl.BlockSpec((None, tm // SB_BLOCK, SB_WIDTH, SB_BLOCK), lambda i: (i, 0, 0, 0))
    vsbt_shape = jax.ShapeDtypeStruct((n // tm, tm // SB_BLOCK, SB_WIDTH, SB_BLOCK), BF16)
    vt_spec = pl.BlockSpec((None, MLA_HEADS, V_ROWS, tm), lambda i: (i, 0, 0, 0))
    vt_shape = jax.ShapeDtypeStruct((n // tm, MLA_HEADS, V_ROWS, tm), BF16)
    tile = lambda w: pl.BlockSpec((tm, w), row)
    sds = lambda w: jax.ShapeDtypeStruct((n, w), BF16)
    return pl.pallas_call(
        _proj_kernel,
        grid=(n // tm,),
        in_specs=[pl.BlockSpec((tm, d), row), pl.BlockSpec(gin.shape, fix)] + w_specs + t_specs,
        out_specs=[tile(SB_WIDTH), tile(SB_WIDTH), vsbt_spec, tile(MLA_HEADS * LANES), tile(MLA_HEADS * LANES),
                   vt_spec],
        out_shape=[sds(SB_WIDTH), sds(SB_WIDTH), vsbt_shape, sds(MLA_HEADS * LANES), sds(MLA_HEADS * LANES),
                   vt_shape],
        compiler_params=pltpu.CompilerParams(
            dimension_semantics=("arbitrary",), vmem_limit_bytes=VMEM_LIMIT_BYTES),
        name="proj",
    )(x2, gin, *wts, *tabs)


def _sb_kernel(q_ref, k_ref, vt_ref, u_ref, o_ref, acc_ref):
    t = SB_BLOCK
    qi = pl.program_id(1)

    first = lax.broadcasted_iota(jnp.int32, (LANES, t), 0) < SB_HEAD_DIM
    q_heads = []
    for p in range(SB_HEADS // 2):
        q_t = q_ref[:, p * LANES:(p + 1) * LANES].astype(F32).T
        q_heads += [jnp.where(first, q_t, 0.0).astype(BF16), jnp.where(first, 0.0, q_t).astype(BF16)]

    def block(j, r, strict):
        k_blk = k_ref[pl.ds(pl.multiple_of(j * t, t), t), :]
        heads = range(SB_HEADS)
        z = [_dot(k_blk[:, (hd // 2) * LANES:(hd // 2 + 1) * LANES], q_heads[hd]) for hd in heads]
        log_beta, later, first_term = [], [], []
        for hd in heads:
            neg_log_1m = jnp.maximum(z[hd], 0.0) + jnp.log(1.0 + jnp.exp2(-jnp.abs(z[hd]))) * LOG2_E
            log_beta.append(z[hd] - neg_log_1m)
            if strict is not None:
                neg_log_1m = jnp.where(strict, neg_log_1m, 0.0)
            neg_log_1m = neg_log_1m.astype(BF16)
            later.append(_dot(u_ref[...], neg_log_1m))
            first_term.append(neg_log_1m[0:1, :].astype(F32))
        for hd in heads:
            w = jnp.exp2(log_beta[hd] - later[hd])
            if strict is not None:
                w = jnp.where(strict, w, 0.0)
            vt_blk = vt_ref[j, hd * SB_HEAD_DIM:(hd + 1) * SB_HEAD_DIM, :]
            pv = _dot(vt_blk, w.astype(BF16))
            acc_ref[hd] += pv if r is None else jnp.exp2(r[hd]) * pv
        return tuple((0.0 if r is None else r[hd]) - later[hd][0:1, :] - first_term[hd] for hd in heads)

    def r_max(r):
        return functools.reduce(jnp.maximum, [jnp.max(x) for x in r])

    acc_ref[...] = jnp.zeros_like(acc_ref)
    key_pos = lax.broadcasted_iota(jnp.int32, (t, t), 0)
    qry_pos = lax.broadcasted_iota(jnp.int32, (t, t), 1)
    r = block(qi, None, key_pos < qry_pos)

    def cond(c):
        j, rmax, _ = c
        return jnp.logical_and(j >= 0, rmax > -SB_DEAD_LOG2)

    def body(c):
        j, _, r = c
        r = block(j, r, None)
        return j - 1, r_max(r), r

    lax.while_loop(cond, body, (qi - 1, r_max(r), r))

    o_ref[...] = jnp.concatenate([acc_ref[hd] for hd in range(SB_HEADS)], axis=0).T


def _sb_call(q, k, vt, u):
    b, s, w = q.shape
    t = SB_BLOCK
    nblk = s // t
    assert vt.shape == (b, nblk, w, t)
    resident = lambda shape: pl.BlockSpec((None,) + shape, lambda bi, qi: (bi,) + (0,) * len(shape))
    return pl.pallas_call(
        _sb_kernel,
        grid=(b, nblk),
        in_specs=[
            pl.BlockSpec((None, t, w), lambda bi, qi: (bi, qi, 0)),
            resident((s, w)),
            resident((nblk, w, t)),
            pl.BlockSpec(u.shape, lambda bi, qi: (0, 0)),
        ],
        out_specs=pl.BlockSpec((None, t, w), lambda bi, qi: (bi, qi, 0)),
        out_shape=jax.ShapeDtypeStruct((b, s, w), F32),
        scratch_shapes=[pltpu.VMEM((SB_HEADS, SB_HEAD_DIM, t), F32)],
        compiler_params=pltpu.CompilerParams(
            dimension_semantics=("arbitrary", "arbitrary"), vmem_limit_bytes=VMEM_LIMIT_BYTES),
        name="sb_attn",
    )(q, k, vt, u)


def _mla_kernel(q_ref, k_ref, vt_ref, o_ref, acc_ref, m_ref, s0_ref, s1_ref, smax0_ref, smax1_ref):
    t = ATT_BLOCK
    qi = pl.program_id(1)

    q_heads = [q_ref[:, hd * LANES:(hd + 1) * LANES].astype(F32).T.astype(BF16) for hd in range(HEADS_PER_STEP)]

    def scores(hd, j, buf, allowed):
        s_ref, smax_ref = buf
        k_blk = k_ref[pl.ds(pl.multiple_of(j * t, t), t), hd * LANES:(hd + 1) * LANES]
        s = _dot(k_blk, q_heads[hd])
        if allowed is not None:
            s = jnp.where(allowed, s, -jnp.inf)
        s_ref[hd] = s
        smax_ref[hd] = jnp.max(s, axis=0, keepdims=True)

    def update(hd, j, buf):
        s_ref, smax_ref = buf
        m = m_ref[hd]
        m_new = jnp.maximum(m, smax_ref[hd])
        m_ref[hd] = m_new
        p = jnp.exp2(s_ref[hd] - m_new).astype(BF16)
        acc_ref[hd] = jnp.exp2(m - m_new) * acc_ref[hd] + _dot(vt_ref[j, hd], p)

    acc_ref[...] = jnp.zeros_like(acc_ref)
    m_ref[...] = jnp.full_like(m_ref, -jnp.inf)
    key_chunk = lax.broadcasted_iota(jnp.int32, (t, t), 0) // CHUNK
    qry_chunk = lax.broadcasted_iota(jnp.int32, (t, t), 1) // CHUNK
    bufs = ((s0_ref, smax0_ref), (s1_ref, smax1_ref))

    block_of = lambda k: jnp.where(k == 0, qi, k - 1)
    allowed = key_chunk <= qry_chunk
    for hd in range(HEADS_PER_STEP):
        scores(hd, qi, bufs[0], allowed)

    def body(k, carry):
        for parity in range(2):
            @pl.when(k % 2 == parity)
            def _():
                for hd in range(HEADS_PER_STEP):
                    scores(hd, block_of(k + 1), bufs[1 - parity], None)
                    update(hd, block_of(k), bufs[parity])
        return carry

    lax.fori_loop(0, qi, body, 0)

    for parity in range(2):
        @pl.when(qi % 2 == parity)
        def _():
            for hd in range(HEADS_PER_STEP):
                update(hd, block_of(qi), bufs[parity])

    o_ref[...] = jnp.concatenate(
        [acc_ref[hd, :MLA_V_DIM, :] / acc_ref[hd, MLA_V_DIM:MLA_V_DIM + 1, :] for hd in range(HEADS_PER_STEP)],
        axis=0).T


def _mla_call(qf, kf, vt):
    b, s, w = qf.shape
    t = ATT_BLOCK
    nblk = s // t
    assert vt.shape == (b, nblk, MLA_HEADS, V_ROWS, t) and HEADS_PER_STEP == MLA_HEADS
    resident = lambda shape: pl.BlockSpec((None,) + shape, lambda bi, qi: (bi,) + (0,) * len(shape),
                                          pipeline_mode=pl.Buffered(1))
    return pl.pallas_call(
        _mla_kernel,
        grid=(b, nblk),
        in_specs=[
            pl.BlockSpec((None, t, w), lambda bi, qi: (bi, qi, 0)),
            resident((s, w)),
            resident((nblk, MLA_HEADS, V_ROWS, t)),
        ],
        out_specs=pl.BlockSpec((None, t, MLA_WIDTH), lambda bi, qi: (bi, qi, 0)),
        out_shape=jax.ShapeDtypeStruct((b, s, MLA_WIDTH), F32),
        scratch_shapes=[
            pltpu.VMEM((HEADS_PER_STEP, V_ROWS, t), F32),
            pltpu.VMEM((HEADS_PER_STEP, 1, t), F32),
            pltpu.VMEM((HEADS_PER_STEP, t, t), F32),
            pltpu.VMEM((HEADS_PER_STEP, t, t), F32),
            pltpu.VMEM((HEADS_PER_STEP, 1, t), F32),
            pltpu.VMEM((HEADS_PER_STEP, 1, t), F32),
        ],
        compiler_params=pltpu.CompilerParams(
            dimension_semantics=("arbitrary", "arbitrary"), vmem_limit_bytes=VMEM_LIMIT_BYTES),
        name="mla_attn",
    )(qf, kf, vt)


def _out_kernel(x_ref, osb_ref, omla_ref, gin_ref, wgsb_ref, wg_ref, bg_ref, wosb_ref, womla_ref, wout_ref,
                gf_ref, o_ref, *, final_norm):
    x = x_ref[...]
    d = x.shape[-1]
    h = _rms(x, gin_ref[...]).astype(BF16)
    a_sb = (osb_ref[...] * jax.nn.silu(_dot(h, wgsb_ref[...]))).astype(BF16)
    a_mla = (omla_ref[...] * jax.nn.silu(_dot(h, wg_ref[:, :MLA_WIDTH]))).astype(BF16)
    g0 = MLA_WIDTH
    g_sb = jax.nn.sigmoid(_dot(h, wg_ref[:, g0:g0 + d]) + bg_ref[:, :d])
    merged = g_sb * _dot(a_sb, wosb_ref[...])
    g_mla = jax.nn.sigmoid(_dot(h, wg_ref[:, g0 + d:g0 + 2 * d]) + bg_ref[:, d:])
    merged = merged + g_mla * _dot(a_mla, womla_ref[...])
    y = x + _dot(merged.astype(BF16), wout_ref[...])
    o_ref[...] = _rms(y, gf_ref[...]) if final_norm else y


def _out_call(x2, osb, omla, gin, wgsb, wg, bg, wosb, womla, wout, gf, final_norm):
    n, d = x2.shape
    tm = min(OUT_TILE, n)
    row = lambda i: (i, 0)
    fix = lambda i: (0, 0)
    consts = (gin, wgsb, wg, bg, wosb, womla, wout, gf)
    return pl.pallas_call(
        functools.partial(_out_kernel, final_norm=final_norm),
        grid=(n // tm,),
        in_specs=[pl.BlockSpec((tm, d), row), pl.BlockSpec((tm, SB_WIDTH), row), pl.BlockSpec((tm, MLA_WIDTH), row)]
                 + [pl.BlockSpec(c.shape, fix, pipeline_mode=pl.Buffered(1)) for c in consts],
        out_specs=pl.BlockSpec((tm, d), row),
        out_shape=jax.ShapeDtypeStruct((n, d), F32),
        compiler_params=pltpu.CompilerParams(
            dimension_semantics=("arbitrary",), vmem_limit_bytes=VMEM_LIMIT_BYTES),
        name="out",
    )(x2, osb, omla, *consts)


def _rope_tables(s):
    half = MLA_ROPE_DIM // 2
    inv_freq = ROPE_THETA ** (-np.arange(half, dtype=np.float64) / half)
    ang = np.arange(s, dtype=np.float64)[:, None] * inv_freq[None, :]
    cos, sin = np.cos(ang), np.sin(ang)
    lanes = lambda lo, hi: np.concatenate(
        [np.zeros((s, MLA_NOPE_DIM)), lo, hi, np.zeros((s, LANES - MLA_QK_DIM))], axis=1).astype(np.float32)
    return jnp.asarray(lanes(cos, cos)), jnp.asarray(lanes(-sin, sin))


def _layer_weights(w_in, w_q_up, w_kv_up):
    d = w_in.shape[0]
    o_v, o_gsb, o_cq = 2 * SB_WIDTH, 3 * SB_WIDTH, 4 * SB_WIDTH
    o_ckv = o_cq + MLA_Q_LORA
    o_kr = o_ckv + MLA_KV_LORA
    o_gm = o_kr + MLA_ROPE_DIM
    cols = lambda lo, hi: w_in[:, lo:hi].astype(BF16)
    zeros = lambda r, w: jnp.zeros((r, w), BF16)
    wqk = cols(0, o_v)
    wvsb_t = cols(o_v, o_gsb).T
    wgsb = cols(o_gsb, o_cq)
    wc = jnp.concatenate([cols(o_cq, o_ckv), zeros(d, MLA_NOPE_DIM), cols(o_kr, o_gm),
                          zeros(d, LANES - MLA_QK_DIM)], axis=1)
    wckv = cols(o_ckv, o_kr)
    wg = cols(o_gm, w_in.shape[1])

    rq = w_q_up.shape[0]
    wq = w_q_up.astype(BF16).reshape(rq, MLA_HEADS, MLA_QK_DIM)
    wq = jnp.concatenate([wq, jnp.zeros((rq, MLA_HEADS, LANES - MLA_QK_DIM), BF16)], axis=-1)
    rkv = w_kv_up.shape[0]
    wkv = w_kv_up.astype(BF16).reshape(rkv, MLA_HEADS, MLA_NOPE_DIM + MLA_V_DIM)
    wkn = jnp.concatenate([wkv[..., :MLA_NOPE_DIM], jnp.zeros((rkv, MLA_HEADS, LANES - MLA_NOPE_DIM), BF16)], axis=-1)
    wv_t = wkv[..., MLA_NOPE_DIM:].reshape(rkv, MLA_WIDTH).T
    return dict(wqk=wqk, wvsb_t=wvsb_t, wgsb=wgsb, wc=wc, wckv=wckv, wg=wg, wq=wq.reshape(rq, MLA_HEADS * LANES),
                wkn=wkn.reshape(rkv, MLA_HEADS * LANES), wv=wv_t)


def kernel(x, norm_in_g, w_in, b_gate, q_norm_g, w_q_up, kv_norm_g, w_kv_up, w_o_sb, w_o_mla, w_out, norm_f_g):
    b, s, d = x.shape
    depth = w_in.shape[0]
    assert s % ATT_BLOCK == 0 and s % SB_BLOCK == 0 and TOKEN_TILE == ATT_BLOCK
    tabs = _rope_tables(s)
    idx = jnp.arange(SB_BLOCK)
    after_mat = (idx[None, :] > idx[:, None]).astype(BF16)

    x2 = x.reshape(b * s, d)
    for l in range(depth):
        w = _layer_weights(w_in[l], w_q_up[l], w_kv_up[l])
        gin = norm_in_g[l][None, :]
        qsb, ksb, vsbt, qf, kf, vt = _proj_call(
            x2, gin,
            (w["wqk"], w["wvsb_t"], w["wc"], w["wckv"], q_norm_g[l][None, :], kv_norm_g[l][None, :], w["wq"], w["wkn"],
             w["wv"]),
            tabs)
        r3 = lambda a: a.reshape(b, s, a.shape[-1])
        vsbt = vsbt.reshape(b, s // SB_BLOCK, SB_WIDTH, SB_BLOCK)
        vt = vt.reshape((b, s // ATT_BLOCK) + vt.shape[1:])
        o_sb = _sb_call(r3(qsb), r3(ksb), vsbt, after_mat).reshape(b * s, SB_WIDTH)
        o_mla = _mla_call(r3(qf), r3(kf), vt).reshape(b * s, MLA_WIDTH)
        x2 = _out_call(x2, o_sb, o_mla, gin, w["wgsb"], w["wg"], b_gate[l][None, :], w_o_sb[l].astype(BF16),
                       w_o_mla[l].astype(BF16), w_out[l].astype(BF16), norm_f_g[None, :],
                       final_norm=(l == depth - 1))
    return x2.reshape(b, s, d)
```

```python
import functools

import jax
import jax.numpy as jnp
import numpy as np
from jax import lax
from jax.experimental import pallas as pl
from jax.experimental.pallas import tpu as pltpu

EPS = 1e-6
CHUNK = 64

SB_HEADS = 8
SB_HEAD_DIM = 64
SB_WIDTH = SB_HEADS * SB_HEAD_DIM

MLA_HEADS = 8
MLA_NOPE_DIM = 64
MLA_ROPE_DIM = 32
MLA_V_DIM = 64
MLA_Q_LORA = 384
MLA_KV_LORA = 256
MLA_QK_DIM = MLA_NOPE_DIM + MLA_ROPE_DIM
MLA_WIDTH = MLA_HEADS * MLA_V_DIM
ROPE_THETA = 10000.0

LANES = 128
ATT_BLOCK = 512
SB_BLOCK = 256
SB_DEAD_LOG2 = 160.0
LOG2_E = float(np.log2(np.e))
HEADS_PER_STEP = 8
V_ROWS = 80
TOKEN_TILE = 512
OUT_TILE = 1024
VMEM_LIMIT_BYTES = 56 * 1024 * 1024

F32 = jnp.float32
BF16 = jnp.bfloat16
NT_DIMS = (((1,), (1,)), ((), ()))


def _dot(a, b):
    return jnp.dot(a, b, preferred_element_type=F32)


def _rms(x, g):
    inv = lax.rsqrt(jnp.mean(x * x, axis=-1, keepdims=True) + EPS)
    return (x * inv) * g


def _proj_kernel(x_ref, gin_ref, wqk_ref, wvsb_ref, wc_ref, wckv_ref, qng_ref, kvng_ref, wq_ref, wkn_ref, wv_ref,
                 cos_ref, sin_ref, qsb_ref, ksb_ref, vsbt_ref, qf_ref, kf_ref, vt_ref):
    h = _rms(x_ref[...], gin_ref[...]).astype(BF16)

    c = _dot(h, wc_ref[...])
    ckv = _dot(h, wckv_ref[...])
    qsb_ref[...] = (_dot(h, wqk_ref[:, :SB_WIDTH]) * (SB_HEAD_DIM ** -0.5 * LOG2_E)).astype(BF16)
    cq = _rms(c[:, :MLA_Q_LORA], qng_ref[...]).astype(BF16)
    ckv = _rms(ckv, kvng_ref[...]).astype(BF16)

    def rope(v, cos, sin_lo, sin_hi):
        return (v * cos + pltpu.roll(v, LANES - MLA_ROPE_DIM // 2, 1) * sin_lo
                + pltpu.roll(v, MLA_ROPE_DIM // 2, 1) * sin_hi)

    lane = lax.broadcasted_iota(jnp.int32, cos_ref.shape, 1)
    ck_t, sin_t = cos_ref[...], sin_ref[...]
    sk_lo = jnp.where(lane < MLA_NOPE_DIM + MLA_ROPE_DIM // 2, sin_t, 0.0)
    sk_hi = sin_t - sk_lo
    scale = MLA_QK_DIM ** -0.5 * np.log2(np.e)
    cq_t = jnp.where(lane < MLA_NOPE_DIM, scale, ck_t * scale)
    sq_lo, sq_hi = sk_lo * scale, sk_hi * scale

    k_rope = rope(c[:, MLA_Q_LORA:], ck_t, sk_lo, sk_hi)
    q = _dot(cq, wq_ref[...])
    kn = _dot(ckv, wkn_ref[...])
    v_t = lax.dot_general(wv_ref[...], ckv, NT_DIMS, preferred_element_type=F32)
    pad_rows = (lax.broadcasted_iota(jnp.int32, (V_ROWS - MLA_V_DIM, v_t.shape[1]), 0) == 0).astype(BF16)
    for hd in range(MLA_HEADS):
        vt_ref[hd, :MLA_V_DIM, :] = v_t[hd * MLA_V_DIM:(hd + 1) * MLA_V_DIM, :].astype(BF16)
        vt_ref[hd, MLA_V_DIM:, :] = pad_rows
    ksb_ref[...] = _dot(h, wqk_ref[:, SB_WIDTH:]).astype(BF16)
    v_sb_t = lax.dot_general(wvsb_ref[...], h, NT_DIMS, preferred_element_type=F32)
    for half in range(v_sb_t.shape[1] // SB_BLOCK):
        vsbt_ref[half] = v_sb_t[:, half * SB_BLOCK:(half + 1) * SB_BLOCK].astype(BF16)
    for hd in range(MLA_HEADS):
        sl = slice(hd * LANES, (hd + 1) * LANES)
        qf_ref[:, sl] = rope(q[:, sl], cq_t, sq_lo, sq_hi).astype(BF16)
        kf_ref[:, sl] = (kn[:, sl] + k_rope).astype(BF16)


def _proj_call(x2, gin, wts, tabs):
    n, d = x2.shape
    s = tabs[0].shape[0]
    tm = min(TOKEN_TILE, s)
    row = lambda i: (i, 0)
    fix = lambda i: (0, 0)
    w_specs = [pl.BlockSpec(w.shape, fix) for w in wts]
    t_specs = [pl.BlockSpec((tm, LANES), lambda i: (i % (s // tm), 0)) for _ in tabs]
    vsbt_spec = pl.BlockSpec((None, tm // SB_BLOCK, SB_WIDTH, SB_BLOCK), lambda i: (i, 0, 0, 0))
    vsbt_shape = jax.ShapeDtypeStruct((n // tm, tm // SB_BLOCK, SB_WIDTH, SB_BLOCK), BF16)
    vt_spec = pl.BlockSpec((None, MLA_HEADS, V_ROWS, tm), lambda i: (i, 0, 0, 0))
    vt_shape = jax.ShapeDtypeStruct((n // tm, MLA_HEADS, V_ROWS, tm), BF16)
    tile = lambda w: pl.BlockSpec((tm, w), row)
    sds = lambda w: jax.ShapeDtypeStruct((n, w), BF16)
    return pl.pallas_call(
        _proj_kernel,
        grid=(n // tm,),
        in_specs=[pl.BlockSpec((tm, d), row), pl.BlockSpec(gin.shape, fix)] + w_specs + t_specs,
        out_specs=[tile(SB_WIDTH), tile(SB_WIDTH), vsbt_spec, tile(MLA_HEADS * LANES), tile(MLA_HEADS * LANES),
                   vt_spec],
        out_shape=[sds(SB_WIDTH), sds(SB_WIDTH), vsbt_shape, sds(MLA_HEADS * LANES), sds(MLA_HEADS * LANES),
                   vt_shape],
        compiler_params=pltpu.CompilerParams(
            dimension_semantics=("arbitrary",), vmem_limit_bytes=VMEM_LIMIT_BYTES),
        name="proj",
    )(x2, gin, *wts, *tabs)


def _sb_kernel(q_ref, kin_ref, vtin_ref, u_ref, o_ref, k_ref, vt_ref, acc_ref):
    t = SB_BLOCK
    qi = pl.program_id(1)
    k_ref[qi] = kin_ref[...]
    vt_ref[qi] = vtin_ref[...]

    first = lax.broadcasted_iota(jnp.int32, (LANES, t), 0) < SB_HEAD_DIM
    q_heads = []
    for p in range(SB_HEADS // 2):
        q_t = q_ref[:, p * LANES:(p + 1) * LANES].astype(F32).T
        q_heads += [jnp.where(first, q_t, 0.0).astype(BF16), jnp.where(first, 0.0, q_t).astype(BF16)]

    def block(j, r, strict):
        k_blk = k_ref[j]
        heads = range(SB_HEADS)
        z = [_dot(k_blk[:, (hd // 2) * LANES:(hd // 2 + 1) * LANES], q_heads[hd]) for hd in heads]
        log_beta, later, first_term = [], [], []
        for hd in heads:
            neg_log_1m = jnp.maximum(z[hd], 0.0) + jnp.log(1.0 + jnp.exp2(-jnp.abs(z[hd]))) * LOG2_E
            log_beta.append(z[hd] - neg_log_1m)
            if strict is not None:
                neg_log_1m = jnp.where(strict, neg_log_1m, 0.0)
            neg_log_1m = neg_log_1m.astype(BF16)
            later.append(_dot(u_ref[...], neg_log_1m))
            first_term.append(neg_log_1m[0:1, :].astype(F32))
        for hd in heads:
            w = jnp.exp2(log_beta[hd] - later[hd] + r[hd])
            if strict is not None:
                w = jnp.where(strict, w, 0.0)
            vt_blk = vt_ref[j, hd * SB_HEAD_DIM:(hd + 1) * SB_HEAD_DIM, :]
            acc_ref[hd] += _dot(vt_blk, w.astype(BF16))
        return tuple(r[hd] - later[hd][0:1, :] - first_term[hd] for hd in heads)

    def r_max(r):
        return functools.reduce(jnp.maximum, [jnp.max(x) for x in r])

    acc_ref[...] = jnp.zeros_like(acc_ref)
    zero = jnp.zeros((1, t), F32)
    key_pos = lax.broadcasted_iota(jnp.int32, (t, t), 0)
    qry_pos = lax.broadcasted_iota(jnp.int32, (t, t), 1)
    r = block(qi, (zero,) * SB_HEADS, key_pos < qry_pos)

    def cond(c):
        j, rmax, _ = c
        return jnp.logical_and(j >= 0, rmax > -SB_DEAD_LOG2)

    def body(c):
        j, _, r = c
        r = block(j, r, None)
        return j - 1, r_max(r), r

    lax.while_loop(cond, body, (qi - 1, r_max(r), r))

    o_ref[...] = jnp.concatenate([acc_ref[hd] for hd in range(SB_HEADS)], axis=0).T


def _sb_call(q, k, vt, u):
    b, s, w = q.shape
    t = SB_BLOCK
    nblk = s // t
    assert vt.shape == (b, nblk, w, t)
    return pl.pallas_call(
        _sb_kernel,
        grid=(b, nblk),
        in_specs=[
            pl.BlockSpec((None, t, w), lambda bi, qi: (bi, qi, 0)),
            pl.BlockSpec((None, t, w), lambda bi, qi: (bi, qi, 0)),
            pl.BlockSpec((None, None, w, t), lambda bi, qi: (bi, qi, 0, 0)),
            pl.BlockSpec(u.shape, lambda bi, qi: (0, 0)),
        ],
        out_specs=pl.BlockSpec((None, t, w), lambda bi, qi: (bi, qi, 0)),
        out_shape=jax.ShapeDtypeStruct((b, s, w), F32),
        scratch_shapes=[
            pltpu.VMEM((nblk, t, w), BF16),
            pltpu.VMEM((nblk, w, t), BF16),
            pltpu.VMEM((SB_HEADS, SB_HEAD_DIM, t), F32),
        ],
        compiler_params=pltpu.CompilerParams(
            dimension_semantics=("arbitrary", "arbitrary"), vmem_limit_bytes=VMEM_LIMIT_BYTES),
        name="sb_attn",
    )(q, k, vt, u)


def _mla_kernel(q_ref, kin_ref, vtin_ref, o_ref, k_ref, vt_ref, acc_ref, m_ref, s0_ref, s1_ref, smax0_ref,
                smax1_ref):
    t = ATT_BLOCK
    qi = pl.program_id(1)
    k_ref[qi] = kin_ref[...]
    vt_ref[qi] = vtin_ref[...]

    q_heads = [q_ref[:, hd * LANES:(hd + 1) * LANES].astype(F32).T.astype(BF16) for hd in range(HEADS_PER_STEP)]

    def scores(hd, j, buf, allowed):
        s_ref, smax_ref = buf
        k_blk = k_ref[j, :, hd * LANES:(hd + 1) * LANES]
        s = _dot(k_blk, q_heads[hd])
        if allowed is not None:
            s = jnp.where(allowed, s, -jnp.inf)
        s_ref[hd] = s
        smax_ref[hd] = jnp.max(s, axis=0, keepdims=True)

    def update(hd, j, buf):
        s_ref, smax_ref = buf
        m = m_ref[hd]
        m_new = jnp.maximum(m, smax_ref[hd])
        m_ref[hd] = m_new
        p = jnp.exp2(s_ref[hd] - m_new).astype(BF16)
        acc_ref[hd] = jnp.exp2(m - m_new) * acc_ref[hd] + _dot(vt_ref[j, hd], p)

    acc_ref[...] = jnp.zeros_like(acc_ref)
    m_ref[...] = jnp.full_like(m_ref, -jnp.inf)
    key_chunk = lax.broadcasted_iota(jnp.int32, (t, t), 0) // CHUNK
    qry_chunk = lax.broadcasted_iota(jnp.int32, (t, t), 1) // CHUNK
    bufs = ((s0_ref, smax0_ref), (s1_ref, smax1_ref))

    block_of = lambda k: jnp.where(k == 0, qi, k - 1)
    allowed = key_chunk <= qry_chunk
    for hd in range(HEADS_PER_STEP):
        scores(hd, qi, bufs[0], allowed)

    def body(k, carry):
        for parity in range(2):
            @pl.when(k % 2 == parity)
            def _():
                for hd in range(HEADS_PER_STEP):
                    scores(hd, block_of(k + 1), bufs[1 - parity], None)
                    update(hd, block_of(k), bufs[parity])
        return carry

    lax.fori_loop(0, qi, body, 0)

    for parity in range(2):
        @pl.when(qi % 2 == parity)
        def _():
            for hd in range(HEADS_PER_STEP):
                update(hd, block_of(qi), bufs[parity])

    o_ref[...] = jnp.concatenate(
        [acc_ref[hd, :MLA_V_DIM, :] / acc_ref[hd, MLA_V_DIM:MLA_V_DIM + 1, :] for hd in range(HEADS_PER_STEP)],
        axis=0).T


def _mla_call(qf, kf, vt):
    b, s, w = qf.shape
    t = ATT_BLOCK
    nblk = s // t
    assert vt.shape == (b, nblk, MLA_HEADS, V_ROWS, t) and HEADS_PER_STEP == MLA_HEADS
    return pl.pallas_call(
        _mla_kernel,
        grid=(b, nblk),
        in_specs=[
            pl.BlockSpec((None, t, w), lambda bi, qi: (bi, qi, 0)),
            pl.BlockSpec((None, t, w), lambda bi, qi: (bi, qi, 0)),
            pl.BlockSpec((None, None, MLA_HEADS, V_ROWS, t), lambda bi, qi: (bi, qi, 0, 0, 0)),
        ],
        out_specs=pl.BlockSpec((None, t, MLA_WIDTH), lambda bi, qi: (bi, qi, 0)),
        out_shape=jax.ShapeDtypeStruct((b, s, MLA_WIDTH), F32),
        scratch_shapes=[
            pltpu.VMEM((nblk, t, w), BF16),
            pltpu.VMEM((nblk, MLA_HEADS, V_ROWS, t), BF16),
            pltpu.VMEM((HEADS_PER_STEP, V_ROWS, t), F32),
            pltpu.VMEM((HEADS_PER_STEP, 1, t), F32),
            pltpu.VMEM((HEADS_PER_STEP, t, t), F32),
            pltpu.VMEM((HEADS_PER_STEP, t, t), F32),
            pltpu.VMEM((HEADS_PER_STEP, 1, t), F32),
            pltpu.VMEM((HEADS_PER_STEP, 1, t), F32),
        ],
        compiler_params=pltpu.CompilerParams(
            dimension_semantics=("arbitrary", "arbitrary"), vmem_limit_bytes=VMEM_LIMIT_BYTES),
        name="mla_attn",
    )(qf, kf, vt)


def _out_kernel(x_ref, osb_ref, omla_ref, gin_ref, wgsb_ref, wg_ref, bg_ref, wosb_ref, womla_ref, wout_ref,
                gf_ref, o_ref, *, final_norm):
    x = x_ref[...]
    d = x.shape[-1]
    h = _rms(x, gin_ref[...]).astype(BF16)
    a_sb = (osb_ref[...] * jax.nn.silu(_dot(h, wgsb_ref[...]))).astype(BF16)
    a_mla = (omla_ref[...] * jax.nn.silu(_dot(h, wg_ref[:, :MLA_WIDTH]))).astype(BF16)
    g0 = MLA_WIDTH
    g_sb = jax.nn.sigmoid(_dot(h, wg_ref[:, g0:g0 + d]) + bg_ref[:, :d])
    merged = g_sb * _dot(a_sb, wosb_ref[...])
    g_mla = jax.nn.sigmoid(_dot(h, wg_ref[:, g0 + d:g0 + 2 * d]) + bg_ref[:, d:])
    merged = merged + g_mla * _dot(a_mla, womla_ref[...])
    y = x + _dot(merged.astype(BF16), wout_ref[...])
    o_ref[...] = _rms(y, gf_ref[...]) if final_norm else y


def _out_call(x2, osb, omla, gin, wgsb, wg, bg, wosb, womla, wout, gf, final_norm):
    n, d = x2.shape
    tm = min(OUT_TILE, n)
    row = lambda i: (i, 0)
    fix = lambda i: (0, 0)
    consts = (gin, wgsb, wg, bg, wosb, womla, wout, gf)
    return pl.pallas_call(
        functools.partial(_out_kernel, final_norm=final_norm),
        grid=(n // tm,),
        in_specs=[pl.BlockSpec((tm, d), row), pl.BlockSpec((tm, SB_WIDTH), row), pl.BlockSpec((tm, MLA_WIDTH), row)]
                 + [pl.BlockSpec(c.shape, fix, pipeline_mode=pl.Buffered(1)) for c in consts],
        out_specs=pl.BlockSpec((tm, d), row),
        out_shape=jax.ShapeDtypeStruct((n, d), F32),
        compiler_params=pltpu.CompilerParams(
            dimension_semantics=("arbitrary",), vmem_limit_bytes=VMEM_LIMIT_BYTES),
        name="out",
    )(x2, osb, omla, *consts)


def _rope_tables(s):
    half = MLA_ROPE_DIM // 2
    inv_freq = ROPE_THETA ** (-np.arange(half, dtype=np.float64) / half)
    ang = np.arange(s, dtype=np.float64)[:, None] * inv_freq[None, :]
    cos, sin = np.cos(ang), np.sin(ang)
    lanes = lambda lo, hi: np.concatenate(
        [np.zeros((s, MLA_NOPE_DIM)), lo, hi, np.zeros((s, LANES - MLA_QK_DIM))], axis=1).astype(np.float32)
    return jnp.asarray(lanes(cos, cos)), jnp.asarray(lanes(-sin, sin))


def _layer_weights(w_in, w_q_up, w_kv_up):
    d = w_in.shape[0]
    o_v, o_gsb, o_cq = 2 * SB_WIDTH, 3 * SB_WIDTH, 4 * SB_WIDTH
    o_ckv = o_cq + MLA_Q_LORA
    o_kr = o_ckv + MLA_KV_LORA
    o_gm = o_kr + MLA_ROPE_DIM
    cols = lambda lo, hi: w_in[:, lo:hi].astype(BF16)
    zeros = lambda r, w: jnp.zeros((r, w), BF16)
    wqk = cols(0, o_v)
    wvsb_t = cols(o_v, o_gsb).T
    wgsb = cols(o_gsb, o_cq)
    wc = jnp.concatenate([cols(o_cq, o_ckv), zeros(d, MLA_NOPE_DIM), cols(o_kr, o_gm),
                          zeros(d, LANES - MLA_QK_DIM)], axis=1)
    wckv = cols(o_ckv, o_kr)
    wg = cols(o_gm, w_in.shape[1])

    rq = w_q_up.shape[0]
    wq = w_q_up.astype(BF16).reshape(rq, MLA_HEADS, MLA_QK_DIM)
    wq = jnp.concatenate([wq, jnp.zeros((rq, MLA_HEADS, LANES - MLA_QK_DIM), BF16)], axis=-1)
    rkv = w_kv_up.shape[0]
    wkv = w_kv_up.astype(BF16).reshape(rkv, MLA_HEADS, MLA_NOPE_DIM + MLA_V_DIM)
    wkn = jnp.concatenate([wkv[..., :MLA_NOPE_DIM], jnp.zeros((rkv, MLA_HEADS, LANES - MLA_NOPE_DIM), BF16)], axis=-1)
    wv_t = wkv[..., MLA_NOPE_DIM:].reshape(rkv, MLA_WIDTH).T
    return dict(wqk=wqk, wvsb_t=wvsb_t, wgsb=wgsb, wc=wc, wckv=wckv, wg=wg, wq=wq.reshape(rq, MLA_HEADS * LANES),
                wkn=wkn.reshape(rkv, MLA_HEADS * LANES), wv=wv_t)


def kernel(x, norm_in_g, w_in, b_gate, q_norm_g, w_q_up, kv_norm_g, w_kv_up, w_o_sb, w_o_mla, w_out, norm_f_g):
    b, s, d = x.shape
    depth = w_in.shape[0]
    assert s % ATT_BLOCK == 0 and s % SB_BLOCK == 0 and TOKEN_TILE == ATT_BLOCK
    tabs = _rope_tables(s)
    idx = jnp.arange(SB_BLOCK)
    after_mat = (idx[None, :] > idx[:, None]).astype(BF16)

    x2 = x.reshape(b * s, d)
    for l in range(depth):
        w = _layer_weights(w_in[l], w_q_up[l], w_kv_up[l])
        gin = norm_in_g[l][None, :]
        qsb, ksb, vsbt, qf, kf, vt = _proj_call(
            x2, gin,
            (w["wqk"], w["wvsb_t"], w["wc"], w["wckv"], q_norm_g[l][None, :], kv_norm_g[l][None, :], w["wq"], w["wkn"],
             w["wv"]),
            tabs)
        r3 = lambda a: a.reshape(b, s, a.shape[-1])
        vsbt = vsbt.reshape(b, s // SB_BLOCK, SB_WIDTH, SB_BLOCK)
        vt = vt.reshape((b, s // ATT_BLOCK) + vt.shape[1:])
        o_sb = _sb_call(r3(qsb), r3(ksb), vsbt, after_mat).reshape(b * s, SB_WIDTH)
        o_mla = _mla_call(r3(qf), r3(kf), vt).reshape(b * s, MLA_WIDTH)
        x2 = _out_call(x2, o_sb, o_mla, gin, w["wgsb"], w["wg"], b_gate[l][None, :], w_o_sb[l].astype(BF16),
                       w_o_mla[l].astype(BF16), w_out[l].astype(BF16), norm_f_g[None, :],
                       final_norm=(l == depth - 1))
    return x2.reshape(b, s, d)
```

```python
import functools

import jax
import jax.numpy as jnp
import numpy as np
from jax import lax
from jax.experimental import pallas as pl
from jax.experimental.pallas import tpu as pltpu

EPS = 1e-6
CHUNK = 64

SB_HEADS = 8
SB_HEAD_DIM = 64
SB_WIDTH = SB_HEADS * SB_HEAD_DIM

MLA_HEADS = 8
MLA_NOPE_DIM = 64
MLA_ROPE_DIM = 32
MLA_V_DIM = 64
MLA_Q_LORA = 384
MLA_KV_LORA = 256
MLA_QK_DIM = MLA_NOPE_DIM + MLA_ROPE_DIM
MLA_WIDTH = MLA_HEADS * MLA_V_DIM
ROPE_THETA = 10000.0

LANES = 128
ATT_BLOCK = 512
SB_BLOCK = 256
SB_DEAD_LOG2 = 160.0
LOG2_E = float(np.log2(np.e))
HEADS_PER_STEP = 8
V_ROWS = 80
TOKEN_TILE = 512
OUT_TILE = 1024
WPREP_ROWS = 256
VMEM_LIMIT_BYTES = 56 * 1024 * 1024

F32 = jnp.float32
BF16 = jnp.bfloat16
NT_DIMS = (((1,), (1,)), ((), ()))


def _dot(a, b):
    return jnp.dot(a, b, preferred_element_type=F32)


def _rms(x, g):
    inv = lax.rsqrt(jnp.mean(x * x, axis=-1, keepdims=True) + EPS)
    return (x * inv) * g


def _wprep_kernel(wt_ref, wqk_ref, wvsbt_ref, wgsb_ref, wc_ref, wckv_ref, wg_ref):
    o_v, o_gsb, o_cq = 2 * SB_WIDTH, 3 * SB_WIDTH, 4 * SB_WIDTH
    o_ckv = o_cq + MLA_Q_LORA
    o_kr = o_ckv + MLA_KV_LORA
    o_gm = o_kr + MLA_ROPE_DIM
    rows = wt_ref.shape[1]
    piece = lambda lo, hi: wt_ref[lo:hi, :]
    wqk_ref[...] = piece(0, o_v).T.astype(BF16)
    wvsbt_ref[...] = piece(o_v, o_gsb).astype(BF16)
    wgsb_ref[...] = piece(o_gsb, o_cq).T.astype(BF16)
    wc_t = jnp.concatenate([piece(o_cq, o_ckv), jnp.zeros((MLA_NOPE_DIM, rows), F32), piece(o_kr, o_gm),
                            jnp.zeros((LANES - MLA_QK_DIM, rows), F32)], axis=0)
    wc_ref[...] = wc_t.T.astype(BF16)
    wckv_ref[...] = piece(o_ckv, o_kr).T.astype(BF16)
    wg_ref[...] = piece(o_gm, wt_ref.shape[0]).T.astype(BF16)


def _wprep_call(w_in_t, layer):
    _, cols, d = w_in_t.shape
    rows = WPREP_ROWS
    n_g = cols - (4 * SB_WIDTH + MLA_Q_LORA + MLA_KV_LORA + MLA_ROPE_DIM)
    row = lambda i: (i, 0)
    widths = (2 * SB_WIDTH, None, SB_WIDTH, MLA_Q_LORA + LANES, MLA_KV_LORA, n_g)
    out_specs = [pl.BlockSpec((SB_WIDTH, rows), lambda i: (0, i)) if wd is None else pl.BlockSpec((rows, wd), row)
                 for wd in widths]
    out_shape = [jax.ShapeDtypeStruct((SB_WIDTH, d) if wd is None else (d, wd), BF16) for wd in widths]
    return pl.pallas_call(
        _wprep_kernel,
        grid=(d // rows,),
        in_specs=[pl.BlockSpec((None, cols, rows), lambda i: (layer, 0, i))],
        out_specs=out_specs,
        out_shape=out_shape,
        compiler_params=pltpu.CompilerParams(
            dimension_semantics=("arbitrary",), vmem_limit_bytes=VMEM_LIMIT_BYTES),
        name="wprep",
    )(w_in_t)


def _proj_kernel(x_ref, gin_ref, wqk_ref, wvsb_ref, wc_ref, wckv_ref, qng_ref, kvng_ref, wq_ref, wkn_ref, wv_ref,
                 cos_ref, sin_ref, qsb_ref, ksb_ref, vsbt_ref, qf_ref, kf_ref, vt_ref):
    h = _rms(x_ref[...], gin_ref[...]).astype(BF16)

    c = _dot(h, wc_ref[...])
    ckv = _dot(h, wckv_ref[...])
    qsb_ref[...] = (_dot(h, wqk_ref[:, :SB_WIDTH]) * (SB_HEAD_DIM ** -0.5 * LOG2_E)).astype(BF16)
    cq = _rms(c[:, :MLA_Q_LORA], qng_ref[...]).astype(BF16)
    ckv = _rms(ckv, kvng_ref[...]).astype(BF16)

    def rope(v, cos, sin_lo, sin_hi):
        return (v * cos + pltpu.roll(v, LANES - MLA_ROPE_DIM // 2, 1) * sin_lo
                + pltpu.roll(v, MLA_ROPE_DIM // 2, 1) * sin_hi)

    lane = lax.broadcasted_iota(jnp.int32, cos_ref.shape, 1)
    ck_t, sin_t = cos_ref[...], sin_ref[...]
    sk_lo = jnp.where(lane < MLA_NOPE_DIM + MLA_ROPE_DIM // 2, sin_t, 0.0)
    sk_hi = sin_t - sk_lo
    scale = MLA_QK_DIM ** -0.5 * np.log2(np.e)
    cq_t = jnp.where(lane < MLA_NOPE_DIM, scale, ck_t * scale)
    sq_lo, sq_hi = sk_lo * scale, sk_hi * scale

    k_rope = rope(c[:, MLA_Q_LORA:], ck_t, sk_lo, sk_hi)
    q = _dot(cq, wq_ref[...])
    kn = _dot(ckv, wkn_ref[...])
    v_t = lax.dot_general(wv_ref[...], ckv, NT_DIMS, preferred_element_type=F32)
    pad_rows = (lax.broadcasted_iota(jnp.int32, (V_ROWS - MLA_V_DIM, v_t.shape[1]), 0) == 0).astype(BF16)
    for hd in range(MLA_HEADS):
        vt_ref[hd, :MLA_V_DIM, :] = v_t[hd * MLA_V_DIM:(hd + 1) * MLA_V_DIM, :].astype(BF16)
        vt_ref[hd, MLA_V_DIM:, :] = pad_rows
    ksb_ref[...] = _dot(h, wqk_ref[:, SB_WIDTH:]).astype(BF16)
    v_sb_t = lax.dot_general(wvsb_ref[...], h, NT_DIMS, preferred_element_type=F32)
    for half in range(v_sb_t.shape[1] // SB_BLOCK):
        vsbt_ref[half] = v_sb_t[:, half * SB_BLOCK:(half + 1) * SB_BLOCK].astype(BF16)
    for hd in range(MLA_HEADS):
        sl = slice(hd * LANES, (hd + 1) * LANES)
        qf_ref[:, sl] = rope(q[:, sl], cq_t, sq_lo, sq_hi).astype(BF16)
        kf_ref[:, sl] = (kn[:, sl] + k_rope).astype(BF16)


def _proj_call(x2, gin, wts, tabs):
    n, d = x2.shape
    s = tabs[0].shape[0]
    tm = min(TOKEN_TILE, s)
    row = lambda i: (i, 0)
    fix = lambda i: (0, 0)
    w_specs = [pl.BlockSpec(w.shape, fix) for w in wts]
    t_specs = [pl.BlockSpec((tm, LANES), lambda i: (i % (s // tm), 0)) for _ in tabs]
    vsbt_spec = pl.BlockSpec((None, tm // SB_BLOCK, SB_WIDTH, SB_BLOCK), lambda i: (i, 0, 0, 0))
    vsbt_shape = jax.ShapeDtypeStruct((n // tm, tm // SB_BLOCK, SB_WIDTH, SB_BLOCK), BF16)
    vt_spec = pl.BlockSpec((None, MLA_HEADS, V_ROWS, tm), lambda i: (i, 0, 0, 0))
    vt_shape = jax.ShapeDtypeStruct((n // tm, MLA_HEADS, V_ROWS, tm), BF16)
    tile = lambda w: pl.BlockSpec((tm, w), row)
    sds = lambda w: jax.ShapeDtypeStruct((n, w), BF16)
    return pl.pallas_call(
        _proj_kernel,
        grid=(n // tm,),
        in_specs=[pl.BlockSpec((tm, d), row), pl.BlockSpec(gin.shape, fix)] + w_specs + t_specs,
        out_specs=[tile(SB_WIDTH), tile(SB_WIDTH), vsbt_spec, tile(MLA_HEADS * LANES), tile(MLA_HEADS * LANES),
                   vt_spec],
        out_shape=[sds(SB_WIDTH), sds(SB_WIDTH), vsbt_shape, sds(MLA_HEADS * LANES), sds(MLA_HEADS * LANES),
                   vt_shape],
        compiler_params=pltpu.CompilerParams(
            dimension_semantics=("arbitrary",), vmem_limit_bytes=VMEM_LIMIT_BYTES),
        name="proj",
    )(x2, gin, *wts, *tabs)


def _sb_kernel(q_ref, kin_ref, vtin_ref, u_ref, o_ref, k_ref, vt_ref, acc_ref):
    t = SB_BLOCK
    qi = pl.program_id(1)
    k_ref[qi] = kin_ref[...]
    vt_ref[qi] = vtin_ref[...]

    first = lax.broadcasted_iota(jnp.int32, (LANES, t), 0) < SB_HEAD_DIM
    q_heads = []
    for p in range(SB_HEADS // 2):
        q_t = q_ref[:, p * LANES:(p + 1) * LANES].astype(F32).T
        q_heads += [jnp.where(first, q_t, 0.0).astype(BF16), jnp.where(first, 0.0, q_t).astype(BF16)]

    def block(j, r, strict):
        k_blk = k_ref[j]
        heads = range(SB_HEADS)
        z = [_dot(k_blk[:, (hd // 2) * LANES:(hd // 2 + 1) * LANES], q_heads[hd]) for hd in heads]
        log_beta, later, first_term = [], [], []
        for hd in heads:
            neg_log_1m = jnp.maximum(z[hd], 0.0) + jnp.log(1.0 + jnp.exp2(-jnp.abs(z[hd]))) * LOG2_E
            log_beta.append(z[hd] - neg_log_1m)
            if strict is not None:
                neg_log_1m = jnp.where(strict, neg_log_1m, 0.0)
            neg_log_1m = neg_log_1m.astype(BF16)
            later.append(_dot(u_ref[...], neg_log_1m))
            first_term.append(neg_log_1m[0:1, :].astype(F32))
        for hd in heads:
            w = jnp.exp2(log_beta[hd] - later[hd] + r[hd])
            if strict is not None:
                w = jnp.where(strict, w, 0.0)
            vt_blk = vt_ref[j, hd * SB_HEAD_DIM:(hd + 1) * SB_HEAD_DIM, :]
            acc_ref[hd] += _dot(vt_blk, w.astype(BF16))
        return tuple(r[hd] - later[hd][0:1, :] - first_term[hd] for hd in heads)

    def r_max(r):
        return functools.reduce(jnp.maximum, [jnp.max(x) for x in r])

    acc_ref[...] = jnp.zeros_like(acc_ref)
    zero = jnp.zeros((1, t), F32)
    key_pos = lax.broadcasted_iota(jnp.int32, (t, t), 0)
    qry_pos = lax.broadcasted_iota(jnp.int32, (t, t), 1)
    r = block(qi, (zero,) * SB_HEADS, key_pos < qry_pos)

    def cond(c):
        j, rmax, _ = c
        return jnp.logical_and(j >= 0, rmax > -SB_DEAD_LOG2)

    def body(c):
        j, _, r = c
        r = block(j, r, None)
        return j - 1, r_max(r), r

    lax.while_loop(cond, body, (qi - 1, r_max(r), r))

    o_ref[...] = jnp.concatenate([acc_ref[hd] for hd in range(SB_HEADS)], axis=0).T


def _sb_call(q, k, vt, u):
    b, s, w = q.shape
    t = SB_BLOCK
    nblk = s // t
    assert vt.shape == (b, nblk, w, t)
    return pl.pallas_call(
        _sb_kernel,
        grid=(b, nblk),
        in_specs=[
            pl.BlockSpec((None, t, w), lambda bi, qi: (bi, qi, 0)),
            pl.BlockSpec((None, t, w), lambda bi, qi: (bi, qi, 0)),
            pl.BlockSpec((None, None, w, t), lambda bi, qi: (bi, qi, 0, 0)),
            pl.BlockSpec(u.shape, lambda bi, qi: (0, 0)),
        ],
        out_specs=pl.BlockSpec((None, t, w), lambda bi, qi: (bi, qi, 0)),
        out_shape=jax.ShapeDtypeStruct((b, s, w), F32),
        scratch_shapes=[
            pltpu.VMEM((nblk, t, w), BF16),
            pltpu.VMEM((nblk, w, t), BF16),
            pltpu.VMEM((SB_HEADS, SB_HEAD_DIM, t), F32),
        ],
        compiler_params=pltpu.CompilerParams(
            dimension_semantics=("arbitrary", "arbitrary"), vmem_limit_bytes=VMEM_LIMIT_BYTES),
        name="sb_attn",
    )(q, k, vt, u)


def _mla_kernel(q_ref, kin_ref, vtin_ref, o_ref, k_ref, vt_ref, acc_ref, m_ref, s0_ref, s1_ref, smax0_ref,
                smax1_ref):
    t = ATT_BLOCK
    qi = pl.program_id(1)
    k_ref[qi] = kin_ref[...]
    vt_ref[qi] = vtin_ref[...]

    q_heads = [q_ref[:, hd * LANES:(hd + 1) * LANES].astype(F32).T.astype(BF16) for hd in range(HEADS_PER_STEP)]

    def scores(hd, j, buf, allowed):
        s_ref, smax_ref = buf
        k_blk = k_ref[j, :, hd * LANES:(hd + 1) * LANES]
        s = _dot(k_blk, q_heads[hd])
        if allowed is not None:
            s = jnp.where(allowed, s, -jnp.inf)
        s_ref[hd] = s
        smax_ref[hd] = jnp.max(s, axis=0, keepdims=True)

    def update(hd, j, buf):
        s_ref, smax_ref = buf
        m = m_ref[hd]
        m_new = jnp.maximum(m, smax_ref[hd])
        m_ref[hd] = m_new
        p = jnp.exp2(s_ref[hd] - m_new).astype(BF16)
        acc_ref[hd] = jnp.exp2(m - m_new) * acc_ref[hd] + _dot(vt_ref[j, hd], p)

    acc_ref[...] = jnp.zeros_like(acc_ref)
    m_ref[...] = jnp.full_like(m_ref, -jnp.inf)
    key_chunk = lax.broadcasted_iota(jnp.int32, (t, t), 0) // CHUNK
    qry_chunk = lax.broadcasted_iota(jnp.int32, (t, t), 1) // CHUNK
    bufs = ((s0_ref, smax0_ref), (s1_ref, smax1_ref))

    block_of = lambda k: jnp.where(k == 0, qi, k - 1)
    allowed = key_chunk <= qry_chunk
    for hd in range(HEADS_PER_STEP):
        scores(hd, qi, bufs[0], allowed)

    def body(k, carry):
        for parity in range(2):
            @pl.when(k % 2 == parity)
            def _():
                for hd in range(HEADS_PER_STEP):
                    scores(hd, block_of(k + 1), bufs[1 - parity], None)
                    update(hd, block_of(k), bufs[parity])
        return carry

    lax.fori_loop(0, qi, body, 0)

    for parity in range(2):
        @pl.when(qi % 2 == parity)
        def _():
            for hd in range(HEADS_PER_STEP):
                update(hd, block_of(qi), bufs[parity])

    o_ref[...] = jnp.concatenate(
        [acc_ref[hd, :MLA_V_DIM, :] / acc_ref[hd, MLA_V_DIM:MLA_V_DIM + 1, :] for hd in range(HEADS_PER_STEP)],
        axis=0).T


def _mla_call(qf, kf, vt):
    b, s, w = qf.shape
    t = ATT_BLOCK
    nblk = s // t
    assert vt.shape == (b, nblk, MLA_HEADS, V_ROWS, t) and HEADS_PER_STEP == MLA_HEADS
    return pl.pallas_call(
        _mla_kernel,
        grid=(b, nblk),
        in_specs=[
            pl.BlockSpec((None, t, w), lambda bi, qi: (bi, qi, 0)),
            pl.BlockSpec((None, t, w), lambda bi, qi: (bi, qi, 0)),
            pl.BlockSpec((None, None, MLA_HEADS, V_ROWS, t), lambda bi, qi: (bi, qi, 0, 0, 0)),
        ],
        out_specs=pl.BlockSpec((None, t, MLA_WIDTH), lambda bi, qi: (bi, qi, 0)),
        out_shape=jax.ShapeDtypeStruct((b, s, MLA_WIDTH), F32),
        scratch_shapes=[
            pltpu.VMEM((nblk, t, w), BF16),
            pltpu.VMEM((nblk, MLA_HEADS, V_ROWS, t), BF16),
            pltpu.VMEM((HEADS_PER_STEP, V_ROWS, t), F32),
            pltpu.VMEM((HEADS_PER_STEP, 1, t), F32),
            pltpu.VMEM((HEADS_PER_STEP, t, t), F32),
            pltpu.VMEM((HEADS_PER_STEP, t, t), F32),
            pltpu.VMEM((HEADS_PER_STEP, 1, t), F32),
            pltpu.VMEM((HEADS_PER_STEP, 1, t), F32),
        ],
        compiler_params=pltpu.CompilerParams(
            dimension_semantics=("arbitrary", "arbitrary"), vmem_limit_bytes=VMEM_LIMIT_BYTES),
        name="mla_attn",
    )(qf, kf, vt)


def _out_kernel(x_ref, osb_ref, omla_ref, gin_ref, wgsb_ref, wg_ref, bg_ref, wosb_ref, womla_ref, wout_ref,
                gf_ref, o_ref, *, final_norm):
    x = x_ref[...]
    d = x.shape[-1]
    h = _rms(x, gin_ref[...]).astype(BF16)
    a_sb = (osb_ref[...] * jax.nn.silu(_dot(h, wgsb_ref[...]))).astype(BF16)
    a_mla = (omla_ref[...] * jax.nn.silu(_dot(h, wg_ref[:, :MLA_WIDTH]))).astype(BF16)
    g0 = MLA_WIDTH
    g_sb = jax.nn.sigmoid(_dot(h, wg_ref[:, g0:g0 + d]) + bg_ref[:, :d])
    merged = g_sb * _dot(a_sb, wosb_ref[...])
    g_mla = jax.nn.sigmoid(_dot(h, wg_ref[:, g0 + d:g0 + 2 * d]) + bg_ref[:, d:])
    merged = merged + g_mla * _dot(a_mla, womla_ref[...])
    y = x + _dot(merged.astype(BF16), wout_ref[...])
    o_ref[...] = _rms(y, gf_ref[...]) if final_norm else y


def _out_call(x2, osb, omla, gin, wgsb, wg, bg, wosb, womla, wout, gf, final_norm):
    n, d = x2.shape
    tm = min(OUT_TILE, n)
    row = lambda i: (i, 0)
    fix = lambda i: (0, 0)
    consts = (gin, wgsb, wg, bg, wosb, womla, wout, gf)
    return pl.pallas_call(
        functools.partial(_out_kernel, final_norm=final_norm),
        grid=(n // tm,),
        in_specs=[pl.BlockSpec((tm, d), row), pl.BlockSpec((tm, SB_WIDTH), row), pl.BlockSpec((tm, MLA_WIDTH), row)]
                 + [pl.BlockSpec(c.shape, fix, pipeline_mode=pl.Buffered(1)) for c in consts],
        out_specs=pl.BlockSpec((tm, d), row),
        out_shape=jax.ShapeDtypeStruct((n, d), F32),
        compiler_params=pltpu.CompilerParams(
            dimension_semantics=("arbitrary",), vmem_limit_bytes=VMEM_LIMIT_BYTES),
        name="out",
    )(x2, osb, omla, *consts)


def _rope_tables(s):
    half = MLA_ROPE_DIM // 2
    inv_freq = ROPE_THETA ** (-np.arange(half, dtype=np.float64) / half)
    ang = np.arange(s, dtype=np.float64)[:, None] * inv_freq[None, :]
    cos, sin = np.cos(ang), np.sin(ang)
    lanes = lambda lo, hi: np.concatenate(
        [np.zeros((s, MLA_NOPE_DIM)), lo, hi, np.zeros((s, LANES - MLA_QK_DIM))], axis=1).astype(np.float32)
    return jnp.asarray(lanes(cos, cos)), jnp.asarray(lanes(-sin, sin))


def _up_weights(w_q_up, w_kv_up):
    rq = w_q_up.shape[0]
    wq = w_q_up.astype(BF16).reshape(rq, MLA_HEADS, MLA_QK_DIM)
    wq = jnp.concatenate([wq, jnp.zeros((rq, MLA_HEADS, LANES - MLA_QK_DIM), BF16)], axis=-1)
    rkv = w_kv_up.shape[0]
    wkv = w_kv_up.astype(BF16).reshape(rkv, MLA_HEADS, MLA_NOPE_DIM + MLA_V_DIM)
    wkn = jnp.concatenate([wkv[..., :MLA_NOPE_DIM], jnp.zeros((rkv, MLA_HEADS, LANES - MLA_NOPE_DIM), BF16)], axis=-1)
    wv_t = wkv[..., MLA_NOPE_DIM:].reshape(rkv, MLA_WIDTH).T
    return wq.reshape(rq, MLA_HEADS * LANES), wkn.reshape(rkv, MLA_HEADS * LANES), wv_t


def kernel(x, norm_in_g, w_in, b_gate, q_norm_g, w_q_up, kv_norm_g, w_kv_up, w_o_sb, w_o_mla, w_out, norm_f_g):
    b, s, d = x.shape
    depth = w_in.shape[0]
    assert s % ATT_BLOCK == 0 and s % SB_BLOCK == 0 and TOKEN_TILE == ATT_BLOCK
    tabs = _rope_tables(s)
    idx = jnp.arange(SB_BLOCK)
    after_mat = (idx[None, :] > idx[:, None]).astype(BF16)

    w_in_t = jnp.swapaxes(w_in, 1, 2)
    x2 = x.reshape(b * s, d)
    for l in range(depth):
        wqk, wvsb_t, wgsb, wc, wckv, wg = _wprep_call(w_in_t, l)
        wq, wkn, wv_t = _up_weights(w_q_up[l], w_kv_up[l])
        gin = norm_in_g[l][None, :]
        qsb, ksb, vsbt, qf, kf, vt = _proj_call(
            x2, gin,
            (wqk, wvsb_t, wc, wckv, q_norm_g[l][None, :], kv_norm_g[l][None, :], wq, wkn, wv_t),
            tabs)
        r3 = lambda a: a.reshape(b, s, a.shape[-1])
        vsbt = vsbt.reshape(b, s // SB_BLOCK, SB_WIDTH, SB_BLOCK)
        vt = vt.reshape((b, s // ATT_BLOCK) + vt.shape[1:])
        o_sb = _sb_call(r3(qsb), r3(ksb), vsbt, after_mat).reshape(b * s, SB_WIDTH)
        o_mla = _mla_call(r3(qf), r3(kf), vt).reshape(b * s, MLA_WIDTH)
        x2 = _out_call(x2, o_sb, o_mla, gin, wgsb, wg, b_gate[l][None, :], w_o_sb[l].astype(BF16),
                       w_o_mla[l].astype(BF16), w_out[l].astype(BF16), norm_f_g[None, :],
                       final_norm=(l == depth - 1))
    return x2.reshape(b, s, d)
```

```python
import functools

import jax
import jax.numpy as jnp
import numpy as np
from jax import lax
from jax.experimental import pallas as pl
from jax.experimental.pallas import tpu as pltpu

EPS = 1e-6
CHUNK = 64

SB_HEADS = 8
SB_HEAD_DIM = 64
SB_WIDTH = SB_HEADS * SB_HEAD_DIM

MLA_HEADS = 8
MLA_NOPE_DIM = 64
MLA_ROPE_DIM = 32
MLA_V_DIM = 64
MLA_Q_LORA = 384
MLA_KV_LORA = 256
MLA_QK_DIM = MLA_NOPE_DIM + MLA_ROPE_DIM
MLA_WIDTH = MLA_HEADS * MLA_V_DIM
ROPE_THETA = 10000.0

LANES = 128
ATT_BLOCK = 512
SB_BLOCK = 256
SB_DEAD_LOG2 = 160.0
LOG2_E = float(np.log2(np.e))
HEADS_PER_STEP = 8
V_ROWS = 80
TOKEN_TILE = 512
OUT_TILE = 1024
WPREP_ROWS = 256
VMEM_LIMIT_BYTES = 56 * 1024 * 1024

F32 = jnp.float32
BF16 = jnp.bfloat16
NT_DIMS = (((1,), (1,)), ((), ()))


def _dot(a, b):
    return jnp.dot(a, b, preferred_element_type=F32)


def _rms(x, g):
    inv = lax.rsqrt(jnp.mean(x * x, axis=-1, keepdims=True) + EPS)
    return (x * inv) * g


def _wprep_kernel(wt_ref, wqk_ref, wvsbt_ref, wgsb_ref, wc_ref, wckv_ref, wg_ref):
    o_v, o_gsb, o_cq = 2 * SB_WIDTH, 3 * SB_WIDTH, 4 * SB_WIDTH
    o_ckv = o_cq + MLA_Q_LORA
    o_kr = o_ckv + MLA_KV_LORA
    o_gm = o_kr + MLA_ROPE_DIM
    rows = wt_ref.shape[1]
    piece = lambda lo, hi: wt_ref[lo:hi, :]
    wqk_ref[...] = piece(0, o_v).T.astype(BF16)
    wvsbt_ref[...] = piece(o_v, o_gsb).astype(BF16)
    wgsb_ref[...] = piece(o_gsb, o_cq).T.astype(BF16)
    wc_t = jnp.concatenate([piece(o_cq, o_ckv), jnp.zeros((MLA_NOPE_DIM, rows), F32), piece(o_kr, o_gm),
                            jnp.zeros((LANES - MLA_QK_DIM, rows), F32)], axis=0)
    wc_ref[...] = wc_t.T.astype(BF16)
    wckv_ref[...] = piece(o_ckv, o_kr).T.astype(BF16)
    wg_ref[...] = piece(o_gm, wt_ref.shape[0]).T.astype(BF16)


def _wprep_call(w_in_t, layer):
    _, cols, d = w_in_t.shape
    rows = WPREP_ROWS
    n_g = cols - (4 * SB_WIDTH + MLA_Q_LORA + MLA_KV_LORA + MLA_ROPE_DIM)
    row = lambda i: (i, 0)
    widths = (2 * SB_WIDTH, None, SB_WIDTH, MLA_Q_LORA + LANES, MLA_KV_LORA, n_g)
    out_specs = [pl.BlockSpec((SB_WIDTH, rows), lambda i: (0, i)) if wd is None else pl.BlockSpec((rows, wd), row)
                 for wd in widths]
    out_shape = [jax.ShapeDtypeStruct((SB_WIDTH, d) if wd is None else (d, wd), BF16) for wd in widths]
    return pl.pallas_call(
        _wprep_kernel,
        grid=(d // rows,),
        in_specs=[pl.BlockSpec((None, cols, rows), lambda i: (layer, 0, i))],
        out_specs=out_specs,
        out_shape=out_shape,
        compiler_params=pltpu.CompilerParams(
            dimension_semantics=("arbitrary",), vmem_limit_bytes=VMEM_LIMIT_BYTES),
        name="wprep",
    )(w_in_t)


def _proj_kernel(x_ref, gin_ref, wqk_ref, wvsb_ref, wc_ref, wckv_ref, qng_ref, kvng_ref, wq_ref, wkn_ref, wv_ref,
                 cos_ref, sin_ref, qsb_ref, ksb_ref, vsbt_ref, qf_ref, kf_ref, vt_ref):
    h = _rms(x_ref[...], gin_ref[...]).astype(BF16)

    c = _dot(h, wc_ref[...])
    ckv = _dot(h, wckv_ref[...])
    qsb_ref[...] = (_dot(h, wqk_ref[:, :SB_WIDTH]) * (SB_HEAD_DIM ** -0.5 * LOG2_E)).astype(BF16)
    cq = _rms(c[:, :MLA_Q_LORA], qng_ref[...]).astype(BF16)
    ckv = _rms(ckv, kvng_ref[...]).astype(BF16)

    def rope(v, cos, sin_lo, sin_hi):
        return (v * cos + pltpu.roll(v, LANES - MLA_ROPE_DIM // 2, 1) * sin_lo
                + pltpu.roll(v, MLA_ROPE_DIM // 2, 1) * sin_hi)

    lane = lax.broadcasted_iota(jnp.int32, cos_ref.shape, 1)
    ck_t, sin_t = cos_ref[...], sin_ref[...]
    sk_lo = jnp.where(lane < MLA_NOPE_DIM + MLA_ROPE_DIM // 2, sin_t, 0.0)
    sk_hi = sin_t - sk_lo
    scale = MLA_QK_DIM ** -0.5 * LOG2_E
    cq_t = jnp.where(lane < MLA_NOPE_DIM, scale, ck_t * scale)
    sq_lo, sq_hi = sk_lo * scale, sk_hi * scale

    k_rope = rope(c[:, MLA_Q_LORA:], ck_t, sk_lo, sk_hi)
    q = _dot(cq, wq_ref[...])
    kn = _dot(ckv, wkn_ref[...])
    v_t = lax.dot_general(wv_ref[...], ckv, NT_DIMS, preferred_element_type=F32)
    pad_rows = (lax.broadcasted_iota(jnp.int32, (V_ROWS - MLA_V_DIM, v_t.shape[1]), 0) == 0).astype(BF16)
    for hd in range(MLA_HEADS):
        vt_ref[hd, :MLA_V_DIM, :] = v_t[hd * MLA_V_DIM:(hd + 1) * MLA_V_DIM, :].astype(BF16)
        vt_ref[hd, MLA_V_DIM:, :] = pad_rows
    ksb_ref[...] = _dot(h, wqk_ref[:, SB_WIDTH:]).astype(BF16)
    v_sb_t = lax.dot_general(wvsb_ref[...], h, NT_DIMS, preferred_element_type=F32)
    for half in range(v_sb_t.shape[1] // SB_BLOCK):
        vsbt_ref[half] = v_sb_t[:, half * SB_BLOCK:(half + 1) * SB_BLOCK].astype(BF16)
    for hd in range(MLA_HEADS):
        sl = slice(hd * LANES, (hd + 1) * LANES)
        qf_ref[:, sl] = rope(q[:, sl], cq_t, sq_lo, sq_hi).astype(BF16)
        kf_ref[:, sl] = (kn[:, sl] + k_rope).astype(BF16)


def _proj_call(x2, gin, wts, tabs):
    n, d = x2.shape
    s = tabs[0].shape[0]
    tm = min(TOKEN_TILE, s)
    row = lambda i: (i, 0)
    fix = lambda i: (0, 0)
    w_specs = [pl.BlockSpec(w.shape, fix) for w in wts]
    t_specs = [pl.BlockSpec((tm, LANES), lambda i: (i % (s // tm), 0)) for _ in tabs]
    vsbt_spec = pl.BlockSpec((None, tm // SB_BLOCK, SB_WIDTH, SB_BLOCK), lambda i: (i, 0, 0, 0))
    vsbt_shape = jax.ShapeDtypeStruct((n // tm, tm // SB_BLOCK, SB_WIDTH, SB_BLOCK), BF16)
    vt_spec = pl.BlockSpec((None, MLA_HEADS, V_ROWS, tm), lambda i: (i, 0, 0, 0))
    vt_shape = jax.ShapeDtypeStruct((n // tm, MLA_HEADS, V_ROWS, tm), BF16)
    tile = lambda w: pl.BlockSpec((tm, w), row)
    sds = lambda w: jax.ShapeDtypeStruct((n, w), BF16)
    return pl.pallas_call(
        _proj_kernel,
        grid=(n // tm,),
        in_specs=[pl.BlockSpec((tm, d), row), pl.BlockSpec(gin.shape, fix)] + w_specs + t_specs,
        out_specs=[tile(SB_WIDTH), tile(SB_WIDTH), vsbt_spec, tile(MLA_HEADS * LANES), tile(MLA_HEADS * LANES),
                   vt_spec],
        out_shape=[sds(SB_WIDTH), sds(SB_WIDTH), vsbt_shape, sds(MLA_HEADS * LANES), sds(MLA_HEADS * LANES),
                   vt_shape],
        compiler_params=pltpu.CompilerParams(
            dimension_semantics=("arbitrary",), vmem_limit_bytes=VMEM_LIMIT_BYTES),
        name="proj",
    )(x2, gin, *wts, *tabs)


def _sb_kernel(q_ref, kin_ref, vtin_ref, u_ref, o_ref, k_ref, vt_ref, acc_ref):
    t = SB_BLOCK
    qi = pl.program_id(1)
    k_ref[qi] = kin_ref[...]
    vt_ref[qi] = vtin_ref[...]

    first = lax.broadcasted_iota(jnp.int32, (LANES, t), 0) < SB_HEAD_DIM
    q_heads = []
    for p in range(SB_HEADS // 2):
        q_t = q_ref[:, p * LANES:(p + 1) * LANES].astype(F32).T
        q_heads += [jnp.where(first, q_t, 0.0).astype(BF16), jnp.where(first, 0.0, q_t).astype(BF16)]

    def block(j, r, strict):
        k_blk = k_ref[j]
        heads = range(SB_HEADS)
        z = [_dot(k_blk[:, (hd // 2) * LANES:(hd // 2 + 1) * LANES], q_heads[hd]) for hd in heads]
        log_beta, later, first_term = [], [], []
        for hd in heads:
            neg_log_1m = jnp.maximum(z[hd], 0.0) + jnp.log(1.0 + jnp.exp2(-jnp.abs(z[hd]))) * LOG2_E
            log_beta.append(z[hd] - neg_log_1m)
            if strict is not None:
                neg_log_1m = jnp.where(strict, neg_log_1m, 0.0)
            neg_log_1m = neg_log_1m.astype(BF16)
            later.append(_dot(u_ref[...], neg_log_1m))
            first_term.append(neg_log_1m[0:1, :].astype(F32))
        for hd in heads:
            w = jnp.exp2(log_beta[hd] - later[hd] + r[hd])
            if strict is not None:
                w = jnp.where(strict, w, 0.0)
            vt_blk = vt_ref[j, hd * SB_HEAD_DIM:(hd + 1) * SB_HEAD_DIM, :]
            acc_ref[hd] += _dot(vt_blk, w.astype(BF16))
        return tuple(r[hd] - later[hd][0:1, :] - first_term[hd] for hd in heads)

    def r_max(r):
        return functools.reduce(jnp.maximum, [jnp.max(x) for x in r])

    acc_ref[...] = jnp.zeros_like(acc_ref)
    zero = jnp.zeros((1, t), F32)
    key_pos = lax.broadcasted_iota(jnp.int32, (t, t), 0)
    qry_pos = lax.broadcasted_iota(jnp.int32, (t, t), 1)
    r = block(qi, (zero,) * SB_HEADS, key_pos < qry_pos)

    def cond(c):
        j, rmax, _ = c
        return jnp.logical_and(j >= 0, rmax > -SB_DEAD_LOG2)

    def body(c):
        j, _, r = c
        r = block(j, r, None)
        return j - 1, r_max(r), r

    lax.while_loop(cond, body, (qi - 1, r_max(r), r))

    o_ref[...] = jnp.concatenate([acc_ref[hd] for hd in range(SB_HEADS)], axis=0).T


def _sb_call(q, k, vt, u):
    b, s, w = q.shape
    t = SB_BLOCK
    nblk = s // t
    assert vt.shape == (b, nblk, w, t)
    return pl.pallas_call(
        _sb_kernel,
        grid=(b, nblk),
        in_specs=[
            pl.BlockSpec((None, t, w), lambda bi, qi: (bi, qi, 0)),
            pl.BlockSpec((None, t, w), lambda bi, qi: (bi, qi, 0)),
            pl.BlockSpec((None, None, w, t), lambda bi, qi: (bi, qi, 0, 0)),
            pl.BlockSpec(u.shape, lambda bi, qi: (0, 0)),
        ],
        out_specs=pl.BlockSpec((None, t, w), lambda bi, qi: (bi, qi, 0)),
        out_shape=jax.ShapeDtypeStruct((b, s, w), F32),
        scratch_shapes=[
            pltpu.VMEM((nblk, t, w), BF16),
            pltpu.VMEM((nblk, w, t), BF16),
            pltpu.VMEM((SB_HEADS, SB_HEAD_DIM, t), F32),
        ],
        compiler_params=pltpu.CompilerParams(
            dimension_semantics=("arbitrary", "arbitrary"), vmem_limit_bytes=VMEM_LIMIT_BYTES),
        name="sb_attn",
    )(q, k, vt, u)


def _mla_kernel(q_ref, kin_ref, vtin_ref, o_ref, k_ref, vt_ref, acc_ref, m_ref, s0_ref, s1_ref, smax0_ref,
                smax1_ref):
    t = ATT_BLOCK
    qi = pl.program_id(1)
    k_ref[qi] = kin_ref[...]
    vt_ref[qi] = vtin_ref[...]

    q_heads = [q_ref[:, hd * LANES:(hd + 1) * LANES].astype(F32).T.astype(BF16) for hd in range(HEADS_PER_STEP)]

    def scores(hd, j, buf, allowed):
        s_ref, smax_ref = buf
        k_blk = k_ref[j, :, hd * LANES:(hd + 1) * LANES]
        s = _dot(k_blk, q_heads[hd])
        if allowed is not None:
            s = jnp.where(allowed, s, -jnp.inf)
        s_ref[hd] = s
        smax_ref[hd] = jnp.max(s, axis=0, keepdims=True)

    def update(hd, j, buf):
        s_ref, smax_ref = buf
        m = m_ref[hd]
        m_new = jnp.maximum(m, smax_ref[hd])
        m_ref[hd] = m_new
        p = jnp.exp2(s_ref[hd] - m_new).astype(BF16)
        acc_ref[hd] = jnp.exp2(m - m_new) * acc_ref[hd] + _dot(vt_ref[j, hd], p)

    acc_ref[...] = jnp.zeros_like(acc_ref)
    m_ref[...] = jnp.full_like(m_ref, -jnp.inf)
    key_chunk = lax.broadcasted_iota(jnp.int32, (t, t), 0) // CHUNK
    qry_chunk = lax.broadcasted_iota(jnp.int32, (t, t), 1) // CHUNK
    bufs = ((s0_ref, smax0_ref), (s1_ref, smax1_ref))

    block_of = lambda k: jnp.where(k == 0, qi, k - 1)
    allowed = key_chunk <= qry_chunk
    for hd in range(HEADS_PER_STEP):
        scores(hd, qi, bufs[0], allowed)

    def body(k, carry):
        for parity in range(2):
            @pl.when(k % 2 == parity)
            def _():
                for hd in range(HEADS_PER_STEP):
                    scores(hd, block_of(k + 1), bufs[1 - parity], None)
                    update(hd, block_of(k), bufs[parity])
        return carry

    lax.fori_loop(0, qi, body, 0)

    for parity in range(2):
        @pl.when(qi % 2 == parity)
        def _():
            for hd in range(HEADS_PER_STEP):
                update(hd, block_of(qi), bufs[parity])

    o_ref[...] = jnp.concatenate(
        [acc_ref[hd, :MLA_V_DIM, :] / acc_ref[hd, MLA_V_DIM:MLA_V_DIM + 1, :] for hd in range(HEADS_PER_STEP)],
        axis=0).T


def _mla_call(qf, kf, vt):
    b, s, w = qf.shape
    t = ATT_BLOCK
    nblk = s // t
    assert vt.shape == (b, nblk, MLA_HEADS, V_ROWS, t) and HEADS_PER_STEP == MLA_HEADS
    return pl.pallas_call(
        _mla_kernel,
        grid=(b, nblk),
        in_specs=[
            pl.BlockSpec((None, t, w), lambda bi, qi: (bi, qi, 0)),
            pl.BlockSpec((None, t, w), lambda bi, qi: (bi, qi, 0)),
            pl.BlockSpec((None, None, MLA_HEADS, V_ROWS, t), lambda bi, qi: (bi, qi, 0, 0, 0)),
        ],
        out_specs=pl.BlockSpec((None, t, MLA_WIDTH), lambda bi, qi: (bi, qi, 0)),
        out_shape=jax.ShapeDtypeStruct((b, s, MLA_WIDTH), F32),
        scratch_shapes=[
            pltpu.VMEM((nblk, t, w), BF16),
            pltpu.VMEM((nblk, MLA_HEADS, V_ROWS, t), BF16),
            pltpu.VMEM((HEADS_PER_STEP, V_ROWS, t), F32),
            pltpu.VMEM((HEADS_PER_STEP, 1, t), F32),
            pltpu.VMEM((HEADS_PER_STEP, t, t), F32),
            pltpu.VMEM((HEADS_PER_STEP, t, t), F32),
            pltpu.VMEM((HEADS_PER_STEP, 1, t), F32),
            pltpu.VMEM((HEADS_PER_STEP, 1, t), F32),
        ],
        compiler_params=pltpu.CompilerParams(
            dimension_semantics=("arbitrary", "arbitrary"), vmem_limit_bytes=VMEM_LIMIT_BYTES),
        name="mla_attn",
    )(qf, kf, vt)


def _out_kernel(x_ref, osb_ref, omla_ref, gin_ref, wgsb_ref, wg_ref, bg_ref, wosb_ref, womla_ref, wout_ref,
                gf_ref, o_ref, *, final_norm):
    x = x_ref[...]
    d = x.shape[-1]
    h = _rms(x, gin_ref[...]).astype(BF16)
    a_sb = (osb_ref[...] * jax.nn.silu(_dot(h, wgsb_ref[...]))).astype(BF16)
    a_mla = (omla_ref[...] * jax.nn.silu(_dot(h, wg_ref[:, :MLA_WIDTH]))).astype(BF16)
    g0 = MLA_WIDTH
    g_sb = jax.nn.sigmoid(_dot(h, wg_ref[:, g0:g0 + d]) + bg_ref[:, :d])
    merged = g_sb * _dot(a_sb, wosb_ref[...])
    g_mla = jax.nn.sigmoid(_dot(h, wg_ref[:, g0 + d:g0 + 2 * d]) + bg_ref[:, d:])
    merged = merged + g_mla * _dot(a_mla, womla_ref[...])
    y = x + _dot(merged.astype(BF16), wout_ref[...])
    o_ref[...] = _rms(y, gf_ref[...]) if final_norm else y


def _out_call(x2, osb, omla, gin, wgsb, wg, bg, wosb, womla, wout, gf, final_norm):
    n, d = x2.shape
    tm = min(OUT_TILE, n)
    row = lambda i: (i, 0)
    fix = lambda i: (0, 0)
    consts = (gin, wgsb, wg, bg, wosb, womla, wout, gf)
    return pl.pallas_call(
        functools.partial(_out_kernel, final_norm=final_norm),
        grid=(n // tm,),
        in_specs=[pl.BlockSpec((tm, d), row), pl.BlockSpec((tm, SB_WIDTH), row), pl.BlockSpec((tm, MLA_WIDTH), row)]
                 + [pl.BlockSpec(c.shape, fix, pipeline_mode=pl.Buffered(1)) for c in consts],
        out_specs=pl.BlockSpec((tm, d), row),
        out_shape=jax.ShapeDtypeStruct((n, d), F32),
        compiler_params=pltpu.CompilerParams(
            dimension_semantics=("arbitrary",), vmem_limit_bytes=VMEM_LIMIT_BYTES),
        name="out",
    )(x2, osb, omla, *consts)


def _rope_tables(s):
    half = MLA_ROPE_DIM // 2
    inv_freq = ROPE_THETA ** (-np.arange(half, dtype=np.float64) / half)
    ang = np.arange(s, dtype=np.float64)[:, None] * inv_freq[None, :]
    cos, sin = np.cos(ang), np.sin(ang)
    lanes = lambda lo, hi: np.concatenate(
        [np.zeros((s, MLA_NOPE_DIM)), lo, hi, np.zeros((s, LANES - MLA_QK_DIM))], axis=1).astype(np.float32)
    return jnp.asarray(lanes(cos, cos)), jnp.asarray(lanes(-sin, sin))


def _up_weights(w_q_up, w_kv_up):
    rq = w_q_up.shape[0]
    wq = w_q_up.astype(BF16).reshape(rq, MLA_HEADS, MLA_QK_DIM)
    wq = jnp.concatenate([wq, jnp.zeros((rq, MLA_HEADS, LANES - MLA_QK_DIM), BF16)], axis=-1)
    rkv = w_kv_up.shape[0]
    wkv = w_kv_up.astype(BF16).reshape(rkv, MLA_HEADS, MLA_NOPE_DIM + MLA_V_DIM)
    wkn = jnp.concatenate([wkv[..., :MLA_NOPE_DIM], jnp.zeros((rkv, MLA_HEADS, LANES - MLA_NOPE_DIM), BF16)], axis=-1)
    wv_t = wkv[..., MLA_NOPE_DIM:].reshape(rkv, MLA_WIDTH).T
    return wq.reshape(rq, MLA_HEADS * LANES), wkn.reshape(rkv, MLA_HEADS * LANES), wv_t


def kernel(x, norm_in_g, w_in, b_gate, q_norm_g, w_q_up, kv_norm_g, w_kv_up, w_o_sb, w_o_mla, w_out, norm_f_g):
    b, s, d = x.shape
    depth = w_in.shape[0]
    assert s % ATT_BLOCK == 0 and s % SB_BLOCK == 0 and TOKEN_TILE == ATT_BLOCK
    tabs = _rope_tables(s)
    idx = jnp.arange(SB_BLOCK)
    after_mat = (idx[None, :] > idx[:, None]).astype(BF16)

    w_in_t = jnp.swapaxes(w_in, 1, 2)
    x2 = x.reshape(b * s, d)
    for l in range(depth):
        wqk, wvsb_t, wgsb, wc, wckv, wg = _wprep_call(w_in_t, l)
        wq, wkn, wv_t = _up_weights(w_q_up[l], w_kv_up[l])
        gin = norm_in_g[l][None, :]
        qsb, ksb, vsbt, qf, kf, vt = _proj_call(
            x2, gin,
            (wqk, wvsb_t, wc, wckv, q_norm_g[l][None, :], kv_norm_g[l][None, :], wq, wkn, wv_t),
            tabs)
        r3 = lambda a: a.reshape(b, s, a.shape[-1])
        vsbt = vsbt.reshape(b, s // SB_BLOCK, SB_WIDTH, SB_BLOCK)
        vt = vt.reshape((b, s // ATT_BLOCK) + vt.shape[1:])
        o_sb = _sb_call(r3(qsb), r3(ksb), vsbt, after_mat).reshape(b * s, SB_WIDTH)
        o_mla = _mla_call(r3(qf), r3(kf), vt).reshape(b * s, MLA_WIDTH)
        x2 = _out_call(x2, o_sb, o_mla, gin, wgsb, wg, b_gate[l][None, :], w_o_sb[l].astype(BF16),
                       w_o_mla[l].astype(BF16), w_out[l].astype(BF16), norm_f_g[None, :],
                       final_norm=(l == depth - 1))
    return x2.reshape(b, s, d)
```

```python
import functools

import jax
import jax.numpy as jnp
import numpy as np
from jax import lax
from jax.experimental import pallas as pl
from jax.experimental.pallas import tpu as pltpu

EPS = 1e-6
CHUNK = 64

SB_HEADS = 8
SB_HEAD_DIM = 64
SB_WIDTH = SB_HEADS * SB_HEAD_DIM

MLA_HEADS = 8
MLA_NOPE_DIM = 64
MLA_ROPE_DIM = 32
MLA_V_DIM = 64
MLA_Q_LORA = 384
MLA_KV_LORA = 256
MLA_QK_DIM = MLA_NOPE_DIM + MLA_ROPE_DIM
MLA_WIDTH = MLA_HEADS * MLA_V_DIM
ROPE_THETA = 10000.0

LANES = 128
ATT_BLOCK = 512
SB_BLOCK = 256
SB_DEAD_LOG2 = 160.0
LOG2_E = float(np.log2(np.e))
HEADS_PER_STEP = 8
V_ROWS = 80
TOKEN_TILE = 512
OUT_TILE = 1024
WPREP_ROWS = 256
VMEM_LIMIT_BYTES = 56 * 1024 * 1024

F32 = jnp.float32
BF16 = jnp.bfloat16
NT_DIMS = (((1,), (1,)), ((), ()))


def _dot(a, b):
    return jnp.dot(a, b, preferred_element_type=F32)


def _rms(x, g):
    inv = lax.rsqrt(jnp.mean(x * x, axis=-1, keepdims=True) + EPS)
    return (x * inv) * g


def _wprep_kernel(wt_ref, wqk_ref, wvsbt_ref, wgsb_ref, wc_ref, wckv_ref, wg_ref):
    o_v, o_gsb, o_cq = 2 * SB_WIDTH, 3 * SB_WIDTH, 4 * SB_WIDTH
    o_ckv = o_cq + MLA_Q_LORA
    o_kr = o_ckv + MLA_KV_LORA
    o_gm = o_kr + MLA_ROPE_DIM
    rows = wt_ref.shape[1]
    piece = lambda lo, hi: wt_ref[lo:hi, :]
    wqk_ref[...] = piece(0, o_v).T.astype(BF16)
    wvsbt_ref[...] = piece(o_v, o_gsb).astype(BF16)
    wgsb_ref[...] = piece(o_gsb, o_cq).T.astype(BF16)
    wc_t = jnp.concatenate([piece(o_cq, o_ckv), jnp.zeros((MLA_NOPE_DIM, rows), F32), piece(o_kr, o_gm),
                            jnp.zeros((LANES - MLA_QK_DIM, rows), F32)], axis=0)
    wc_ref[...] = wc_t.T.astype(BF16)
    wckv_ref[...] = piece(o_ckv, o_kr).T.astype(BF16)
    wg_ref[...] = piece(o_gm, wt_ref.shape[0]).T.astype(BF16)


def _wprep_call(w_in_t, layer):
    _, cols, d = w_in_t.shape
    rows = WPREP_ROWS
    n_g = cols - (4 * SB_WIDTH + MLA_Q_LORA + MLA_KV_LORA + MLA_ROPE_DIM)
    row = lambda i: (i, 0)
    widths = (2 * SB_WIDTH, None, SB_WIDTH, MLA_Q_LORA + LANES, MLA_KV_LORA, n_g)
    out_specs = [pl.BlockSpec((SB_WIDTH, rows), lambda i: (0, i)) if wd is None else pl.BlockSpec((rows, wd), row)
                 for wd in widths]
    out_shape = [jax.ShapeDtypeStruct((SB_WIDTH, d) if wd is None else (d, wd), BF16) for wd in widths]
    return pl.pallas_call(
        _wprep_kernel,
        grid=(d // rows,),
        in_specs=[pl.BlockSpec((None, cols, rows), lambda i: (layer, 0, i))],
        out_specs=out_specs,
        out_shape=out_shape,
        compiler_params=pltpu.CompilerParams(
            dimension_semantics=("arbitrary",), vmem_limit_bytes=VMEM_LIMIT_BYTES),
        name="wprep",
    )(w_in_t)


def _proj_kernel(x_ref, gin_ref, wqk_ref, wvsb_ref, wc_ref, wckv_ref, qng_ref, kvng_ref, wq_ref, wkn_ref, wv_ref,
                 cos_ref, sin_ref, qsb_ref, ksb_ref, vsbt_ref, qf_ref, kf_ref, vt_ref):
    h = _rms(x_ref[...], gin_ref[...]).astype(BF16)

    c = _dot(h, wc_ref[...])
    ckv = _dot(h, wckv_ref[...])
    qsb_ref[...] = (_dot(h, wqk_ref[:, :SB_WIDTH]) * (SB_HEAD_DIM ** -0.5 * LOG2_E)).astype(BF16)
    cq = _rms(c[:, :MLA_Q_LORA], qng_ref[...]).astype(BF16)
    ckv = _rms(ckv, kvng_ref[...]).astype(BF16)

    def rope(v, cos, sin_lo, sin_hi):
        return (v * cos + pltpu.roll(v, LANES - MLA_ROPE_DIM // 2, 1) * sin_lo
                + pltpu.roll(v, MLA_ROPE_DIM // 2, 1) * sin_hi)

    lane = lax.broadcasted_iota(jnp.int32, cos_ref.shape, 1)
    ck_t, sin_t = cos_ref[...], sin_ref[...]
    sk_lo = jnp.where(lane < MLA_NOPE_DIM + MLA_ROPE_DIM // 2, sin_t, 0.0)
    sk_hi = sin_t - sk_lo
    scale = MLA_QK_DIM ** -0.5 * LOG2_E
    cq_t = jnp.where(lane < MLA_NOPE_DIM, scale, ck_t * scale)
    sq_lo, sq_hi = sk_lo * scale, sk_hi * scale

    k_rope = rope(c[:, MLA_Q_LORA:], ck_t, sk_lo, sk_hi)
    q = _dot(cq, wq_ref[...])
    kn = _dot(ckv, wkn_ref[...])
    v_t = lax.dot_general(wv_ref[...], ckv, NT_DIMS, preferred_element_type=F32)
    pad_rows = (lax.broadcasted_iota(jnp.int32, (V_ROWS - MLA_V_DIM, v_t.shape[1]), 0) == 0).astype(BF16)
    for hd in range(MLA_HEADS):
        vt_ref[hd, :MLA_V_DIM, :] = v_t[hd * MLA_V_DIM:(hd + 1) * MLA_V_DIM, :].astype(BF16)
        vt_ref[hd, MLA_V_DIM:, :] = pad_rows
    ksb_ref[...] = _dot(h, wqk_ref[:, SB_WIDTH:]).astype(BF16)
    v_sb_t = lax.dot_general(wvsb_ref[...], h, NT_DIMS, preferred_element_type=F32)
    for half in range(v_sb_t.shape[1] // SB_BLOCK):
        vsbt_ref[half] = v_sb_t[:, half * SB_BLOCK:(half + 1) * SB_BLOCK].astype(BF16)
    for hd in range(MLA_HEADS):
        sl = slice(hd * LANES, (hd + 1) * LANES)
        qf_ref[:, sl] = rope(q[:, sl], cq_t, sq_lo, sq_hi).astype(BF16)
        kf_ref[:, sl] = (kn[:, sl] + k_rope).astype(BF16)


def _proj_call(x2, gin, wts, tabs):
    n, d = x2.shape
    s = tabs[0].shape[0]
    tm = min(TOKEN_TILE, s)
    row = lambda i: (i, 0)
    fix = lambda i: (0, 0)
    w_specs = [pl.BlockSpec(w.shape, fix) for w in wts]
    t_specs = [pl.BlockSpec((tm, LANES), lambda i: (i % (s // tm), 0)) for _ in tabs]
    vsbt_spec = pl.BlockSpec((None, tm // SB_BLOCK, SB_WIDTH, SB_BLOCK), lambda i: (i, 0, 0, 0))
    vsbt_shape = jax.ShapeDtypeStruct((n // tm, tm // SB_BLOCK, SB_WIDTH, SB_BLOCK), BF16)
    vt_spec = pl.BlockSpec((None, MLA_HEADS, V_ROWS, tm), lambda i: (i, 0, 0, 0))
    vt_shape = jax.ShapeDtypeStruct((n // tm, MLA_HEADS, V_ROWS, tm), BF16)
    tile = lambda w: pl.BlockSpec((tm, w), row)
    sds = lambda w: jax.ShapeDtypeStruct((n, w), BF16)
    return pl.pallas_call(
        _proj_kernel,
        grid=(n // tm,),
        in_specs=[pl.BlockSpec((tm, d), row), pl.BlockSpec(gin.shape, fix)] + w_specs + t_specs,
        out_specs=[tile(SB_WIDTH), tile(SB_WIDTH), vsbt_spec, tile(MLA_HEADS * LANES), tile(MLA_HEADS * LANES),
                   vt_spec],
        out_shape=[sds(SB_WIDTH), sds(SB_WIDTH), vsbt_shape, sds(MLA_HEADS * LANES), sds(MLA_HEADS * LANES),
                   vt_shape],
        compiler_params=pltpu.CompilerParams(
            dimension_semantics=("arbitrary",), vmem_limit_bytes=VMEM_LIMIT_BYTES),
        name="proj",
    )(x2, gin, *wts, *tabs)


def _sb_kernel(q_ref, kin_ref, vtin_ref, u_ref, o_ref, k_ref, vt_ref, acc_ref):
    t = SB_BLOCK
    qi = pl.program_id(1)
    k_ref[qi] = kin_ref[...]
    vt_ref[qi] = vtin_ref[...]

    first = lax.broadcasted_iota(jnp.int32, (LANES, t), 0) < SB_HEAD_DIM
    q_heads = []
    for p in range(SB_HEADS // 2):
        q_t = q_ref[:, p * LANES:(p + 1) * LANES].astype(F32).T
        q_heads += [jnp.where(first, q_t, 0.0).astype(BF16), jnp.where(first, 0.0, q_t).astype(BF16)]

    def block(j, r, strict):
        k_blk = k_ref[j]
        heads = range(SB_HEADS)
        z = [_dot(k_blk[:, (hd // 2) * LANES:(hd // 2 + 1) * LANES], q_heads[hd]) for hd in heads]
        log_beta, later, first_term = [], [], []
        for hd in heads:
            neg_log_1m = jnp.maximum(z[hd], 0.0) + jnp.log(1.0 + jnp.exp2(-jnp.abs(z[hd]))) * LOG2_E
            log_beta.append(z[hd] - neg_log_1m)
            if strict is not None:
                neg_log_1m = jnp.where(strict, neg_log_1m, 0.0)
            neg_log_1m = neg_log_1m.astype(BF16)
            later.append(_dot(u_ref[...], neg_log_1m))
            first_term.append(neg_log_1m[0:1, :].astype(F32))
        for hd in heads:
            w = jnp.exp2(log_beta[hd] - later[hd] + r[hd])
            if strict is not None:
                w = jnp.where(strict, w, 0.0)
            vt_blk = vt_ref[j, hd * SB_HEAD_DIM:(hd + 1) * SB_HEAD_DIM, :]
            acc_ref[hd] += _dot(vt_blk, w.astype(BF16))
        return tuple(r[hd] - later[hd][0:1, :] - first_term[hd] for hd in heads)

    def r_max(r):
        return functools.reduce(jnp.maximum, [jnp.max(x) for x in r])

    acc_ref[...] = jnp.zeros_like(acc_ref)
    zero = jnp.zeros((1, t), F32)
    key_pos = lax.broadcasted_iota(jnp.int32, (t, t), 0)
    qry_pos = lax.broadcasted_iota(jnp.int32, (t, t), 1)
    r = block(qi, (zero,) * SB_HEADS, key_pos < qry_pos)

    def cond(c):
        j, rmax, _ = c
        return jnp.logical_and(j >= 0, rmax > -SB_DEAD_LOG2)

    def body(c):
        j, _, r = c
        r = block(j, r, None)
        return j - 1, r_max(r), r

    lax.while_loop(cond, body, (qi - 1, r_max(r), r))

    o_ref[...] = jnp.concatenate([acc_ref[hd] for hd in range(SB_HEADS)], axis=0).T


def _sb_call(q, k, vt, u):
    b, s, w = q.shape
    t = SB_BLOCK
    nblk = s // t
    assert vt.shape == (b, nblk, w, t)
    return pl.pallas_call(
        _sb_kernel,
        grid=(b, nblk),
        in_specs=[
            pl.BlockSpec((None, t, w), lambda bi, qi: (bi, qi, 0)),
            pl.BlockSpec((None, t, w), lambda bi, qi: (bi, qi, 0)),
            pl.BlockSpec((None, None, w, t), lambda bi, qi: (bi, qi, 0, 0)),
            pl.BlockSpec(u.shape, lambda bi, qi: (0, 0)),
        ],
        out_specs=pl.BlockSpec((None, t, w), lambda bi, qi: (bi, qi, 0)),
        out_shape=jax.ShapeDtypeStruct((b, s, w), F32),
        scratch_shapes=[
            pltpu.VMEM((nblk, t, w), BF16),
            pltpu.VMEM((nblk, w, t), BF16),
            pltpu.VMEM((SB_HEADS, SB_HEAD_DIM, t), F32),
        ],
        compiler_params=pltpu.CompilerParams(
            dimension_semantics=("arbitrary", "arbitrary"), vmem_limit_bytes=VMEM_LIMIT_BYTES),
        name="sb_attn",
    )(q, k, vt, u)


def _mla_kernel(q_ref, kin_ref, vtin_ref, o_ref, k_ref, vt_ref, acc_ref, m_ref, s0_ref, s1_ref, smax0_ref,
                smax1_ref):
    t = ATT_BLOCK
    qi = pl.program_id(1)
    k_ref[qi] = kin_ref[...]
    vt_ref[qi] = vtin_ref[...]

    q_heads = [q_ref[:, hd * LANES:(hd + 1) * LANES].astype(F32).T.astype(BF16) for hd in range(HEADS_PER_STEP)]

    halves = (slice(0, t // 2), slice(t // 2, t))

    def score_rows(hd, j, buf, rows, allowed):
        s = _dot(k_ref[j, rows, hd * LANES:(hd + 1) * LANES], q_heads[hd])
        if allowed is not None:
            s = jnp.where(allowed[rows], s, -jnp.inf)
        buf[0][hd, rows, :] = s
        return jnp.max(s, axis=0, keepdims=True)

    def scores(hd, j, buf, allowed):
        buf[1][hd] = jnp.maximum(*[score_rows(hd, j, buf, rows, allowed) for rows in halves])

    def rescale(hd, buf):
        m = m_ref[hd]
        m_new = jnp.maximum(m, buf[1][hd])
        m_ref[hd] = m_new
        acc_ref[hd] = jnp.exp2(m - m_new) * acc_ref[hd]
        return m_new

    def accumulate(hd, j, buf, rows, m_new):
        p = jnp.exp2(buf[0][hd, rows, :] - m_new).astype(BF16)
        acc_ref[hd] += _dot(vt_ref[j, hd, :, rows], p)

    def update(hd, j, buf):
        m_new = rescale(hd, buf)
        for rows in halves:
            accumulate(hd, j, buf, rows, m_new)

    def step(hd, j_next, buf_next, j_cur, buf_cur):
        m_new = rescale(hd, buf_cur)
        maxes = []
        for rows in halves:
            maxes.append(score_rows(hd, j_next, buf_next, rows, None))
            accumulate(hd, j_cur, buf_cur, rows, m_new)
        buf_next[1][hd] = jnp.maximum(*maxes)

    acc_ref[...] = jnp.zeros_like(acc_ref)
    m_ref[...] = jnp.full_like(m_ref, -jnp.inf)
    key_chunk = lax.broadcasted_iota(jnp.int32, (t, t), 0) // CHUNK
    qry_chunk = lax.broadcasted_iota(jnp.int32, (t, t), 1) // CHUNK
    bufs = ((s0_ref, smax0_ref), (s1_ref, smax1_ref))

    block_of = lambda k: jnp.where(k == 0, qi, k - 1)
    allowed = key_chunk <= qry_chunk
    for hd in range(HEADS_PER_STEP):
        scores(hd, qi, bufs[0], allowed)

    def body(k, carry):
        for parity in range(2):
            @pl.when(k % 2 == parity)
            def _():
                for hd in range(HEADS_PER_STEP):
                    step(hd, block_of(k + 1), bufs[1 - parity], block_of(k), bufs[parity])
        return carry

    lax.fori_loop(0, qi, body, 0)

    for parity in range(2):
        @pl.when(qi % 2 == parity)
        def _():
            for hd in range(HEADS_PER_STEP):
                update(hd, block_of(qi), bufs[parity])

    o_ref[...] = jnp.concatenate(
        [acc_ref[hd, :MLA_V_DIM, :] / acc_ref[hd, MLA_V_DIM:MLA_V_DIM + 1, :] for hd in range(HEADS_PER_STEP)],
        axis=0).T


def _mla_call(qf, kf, vt):
    b, s, w = qf.shape
    t = ATT_BLOCK
    nblk = s // t
    assert vt.shape == (b, nblk, MLA_HEADS, V_ROWS, t) and HEADS_PER_STEP == MLA_HEADS
    return pl.pallas_call(
        _mla_kernel,
        grid=(b, nblk),
        in_specs=[
            pl.BlockSpec((None, t, w), lambda bi, qi: (bi, qi, 0)),
            pl.BlockSpec((None, t, w), lambda bi, qi: (bi, qi, 0)),
            pl.BlockSpec((None, None, MLA_HEADS, V_ROWS, t), lambda bi, qi: (bi, qi, 0, 0, 0)),
        ],
        out_specs=pl.BlockSpec((None, t, MLA_WIDTH), lambda bi, qi: (bi, qi, 0)),
        out_shape=jax.ShapeDtypeStruct((b, s, MLA_WIDTH), F32),
        scratch_shapes=[
            pltpu.VMEM((nblk, t, w), BF16),
            pltpu.VMEM((nblk, MLA_HEADS, V_ROWS, t), BF16),
            pltpu.VMEM((HEADS_PER_STEP, V_ROWS, t), F32),
            pltpu.VMEM((HEADS_PER_STEP, 1, t), F32),
            pltpu.VMEM((HEADS_PER_STEP, t, t), F32),
            pltpu.VMEM((HEADS_PER_STEP, t, t), F32),
            pltpu.VMEM((HEADS_PER_STEP, 1, t), F32),
            pltpu.VMEM((HEADS_PER_STEP, 1, t), F32),
        ],
        compiler_params=pltpu.CompilerParams(
            dimension_semantics=("arbitrary", "arbitrary"), vmem_limit_bytes=VMEM_LIMIT_BYTES),
        name="mla_attn",
    )(qf, kf, vt)


def _out_kernel(x_ref, osb_ref, omla_ref, gin_ref, wgsb_ref, wg_ref, bg_ref, wosb_ref, womla_ref, wout_ref,
                gf_ref, o_ref, *, final_norm):
    x = x_ref[...]
    d = x.shape[-1]
    h = _rms(x, gin_ref[...]).astype(BF16)
    a_sb = (osb_ref[...] * jax.nn.silu(_dot(h, wgsb_ref[...]))).astype(BF16)
    a_mla = (omla_ref[...] * jax.nn.silu(_dot(h, wg_ref[:, :MLA_WIDTH]))).astype(BF16)
    g0 = MLA_WIDTH
    g_sb = jax.nn.sigmoid(_dot(h, wg_ref[:, g0:g0 + d]) + bg_ref[:, :d])
    merged = g_sb * _dot(a_sb, wosb_ref[...])
    g_mla = jax.nn.sigmoid(_dot(h, wg_ref[:, g0 + d:g0 + 2 * d]) + bg_ref[:, d:])
    merged = merged + g_mla * _dot(a_mla, womla_ref[...])
    y = x + _dot(merged.astype(BF16), wout_ref[...])
    o_ref[...] = _rms(y, gf_ref[...]) if final_norm else y


def _out_call(x2, osb, omla, gin, wgsb, wg, bg, wosb, womla, wout, gf, final_norm):
    n, d = x2.shape
    tm = min(OUT_TILE, n)
    row = lambda i: (i, 0)
    fix = lambda i: (0, 0)
    consts = (gin, wgsb, wg, bg, wosb, womla, wout, gf)
    return pl.pallas_call(
        functools.partial(_out_kernel, final_norm=final_norm),
        grid=(n // tm,),
        in_specs=[pl.BlockSpec((tm, d), row), pl.BlockSpec((tm, SB_WIDTH), row), pl.BlockSpec((tm, MLA_WIDTH), row)]
                 + [pl.BlockSpec(c.shape, fix, pipeline_mode=pl.Buffered(1)) for c in consts],
        out_specs=pl.BlockSpec((tm, d), row),
        out_shape=jax.ShapeDtypeStruct((n, d), F32),
        compiler_params=pltpu.CompilerParams(
            dimension_semantics=("arbitrary",), vmem_limit_bytes=VMEM_LIMIT_BYTES),
        name="out",
    )(x2, osb, omla, *consts)


def _rope_tables(s):
    half = MLA_ROPE_DIM // 2
    inv_freq = ROPE_THETA ** (-np.arange(half, dtype=np.float64) / half)
    ang = np.arange(s, dtype=np.float64)[:, None] * inv_freq[None, :]
    cos, sin = np.cos(ang), np.sin(ang)
    lanes = lambda lo, hi: np.concatenate(
        [np.zeros((s, MLA_NOPE_DIM)), lo, hi, np.zeros((s, LANES - MLA_QK_DIM))], axis=1).astype(np.float32)
    return jnp.asarray(lanes(cos, cos)), jnp.asarray(lanes(-sin, sin))


def _up_weights(w_q_up, w_kv_up):
    rq = w_q_up.shape[0]
    wq = w_q_up.astype(BF16).reshape(rq, MLA_HEADS, MLA_QK_DIM)
    wq = jnp.concatenate([wq, jnp.zeros((rq, MLA_HEADS, LANES - MLA_QK_DIM), BF16)], axis=-1)
    rkv = w_kv_up.shape[0]
    wkv = w_kv_up.astype(BF16).reshape(rkv, MLA_HEADS, MLA_NOPE_DIM + MLA_V_DIM)
    wkn = jnp.concatenate([wkv[..., :MLA_NOPE_DIM], jnp.zeros((rkv, MLA_HEADS, LANES - MLA_NOPE_DIM), BF16)], axis=-1)
    wv_t = wkv[..., MLA_NOPE_DIM:].reshape(rkv, MLA_WIDTH).T
    return wq.reshape(rq, MLA_HEADS * LANES), wkn.reshape(rkv, MLA_HEADS * LANES), wv_t


def kernel(x, norm_in_g, w_in, b_gate, q_norm_g, w_q_up, kv_norm_g, w_kv_up, w_o_sb, w_o_mla, w_out, norm_f_g):
    b, s, d = x.shape
    depth = w_in.shape[0]
    assert s % ATT_BLOCK == 0 and s % SB_BLOCK == 0 and TOKEN_TILE == ATT_BLOCK
    tabs = _rope_tables(s)
    idx = jnp.arange(SB_BLOCK)
    after_mat = (idx[None, :] > idx[:, None]).astype(BF16)

    w_in_t = jnp.swapaxes(w_in, 1, 2)
    x2 = x.reshape(b * s, d)
    for l in range(depth):
        wqk, wvsb_t, wgsb, wc, wckv, wg = _wprep_call(w_in_t, l)
        wq, wkn, wv_t = _up_weights(w_q_up[l], w_kv_up[l])
        gin = norm_in_g[l][None, :]
        qsb, ksb, vsbt, qf, kf, vt = _proj_call(
            x2, gin,
            (wqk, wvsb_t, wc, wckv, q_norm_g[l][None, :], kv_norm_g[l][None, :], wq, wkn, wv_t),
            tabs)
        r3 = lambda a: a.reshape(b, s, a.shape[-1])
        vsbt = vsbt.reshape(b, s // SB_BLOCK, SB_WIDTH, SB_BLOCK)
        vt = vt.reshape((b, s // ATT_BLOCK) + vt.shape[1:])
        o_sb = _sb_call(r3(qsb), r3(ksb), vsbt, after_mat).reshape(b * s, SB_WIDTH)
        o_mla = _mla_call(r3(qf), r3(kf), vt).reshape(b * s, MLA_WIDTH)
        x2 = _out_call(x2, o_sb, o_mla, gin, wgsb, wg, b_gate[l][None, :], w_o_sb[l].astype(BF16),
                       w_o_mla[l].astype(BF16), w_out[l].astype(BF16), norm_f_g[None, :],
                       final_norm=(l == depth - 1))
    return x2.reshape(b, s, d)
```

```python
import functools

import jax
import jax.numpy as jnp
import numpy as np
from jax import lax
from jax.experimental import pallas as pl
from jax.experimental.pallas import tpu as pltpu

EPS = 1e-6
CHUNK = 64

SB_HEADS = 8
SB_HEAD_DIM = 64
SB_WIDTH = SB_HEADS * SB_HEAD_DIM

MLA_HEADS = 8
MLA_NOPE_DIM = 64
MLA_ROPE_DIM = 32
MLA_V_DIM = 64
MLA_Q_LORA = 384
MLA_KV_LORA = 256
MLA_QK_DIM = MLA_NOPE_DIM + MLA_ROPE_DIM
MLA_WIDTH = MLA_HEADS * MLA_V_DIM
ROPE_THETA = 10000.0

LANES = 128
ATT_BLOCK = 512
SB_BLOCK = 256
SB_DEAD_LOG2 = 160.0
LOG2_E = float(np.log2(np.e))
HEADS_PER_STEP = 8
V_ROWS = 80
TOKEN_TILE = 512
OUT_TILE = 1024
WPREP_ROWS = 256
VMEM_LIMIT_BYTES = 56 * 1024 * 1024

F32 = jnp.float32
BF16 = jnp.bfloat16
NT_DIMS = (((1,), (1,)), ((), ()))


def _dot(a, b):
    return jnp.dot(a, b, preferred_element_type=F32)


def _rms(x, g):
    inv = lax.rsqrt(jnp.mean(x * x, axis=-1, keepdims=True) + EPS)
    return (x * inv) * g


def _wprep_kernel(wt_ref, wqk_ref, wvsbt_ref, wgsb_ref, wc_ref, wckv_ref, wg_ref):
    o_v, o_gsb, o_cq = 2 * SB_WIDTH, 3 * SB_WIDTH, 4 * SB_WIDTH
    o_ckv = o_cq + MLA_Q_LORA
    o_kr = o_ckv + MLA_KV_LORA
    o_gm = o_kr + MLA_ROPE_DIM
    rows = wt_ref.shape[1]
    piece = lambda lo, hi: wt_ref[lo:hi, :]
    wqk_ref[...] = piece(0, o_v).T.astype(BF16)
    wvsbt_ref[...] = piece(o_v, o_gsb).astype(BF16)
    wgsb_ref[...] = piece(o_gsb, o_cq).T.astype(BF16)
    wc_t = jnp.concatenate([piece(o_cq, o_ckv), jnp.zeros((MLA_NOPE_DIM, rows), F32), piece(o_kr, o_gm),
                            jnp.zeros((LANES - MLA_QK_DIM, rows), F32)], axis=0)
    wc_ref[...] = wc_t.T.astype(BF16)
    wckv_ref[...] = piece(o_ckv, o_kr).T.astype(BF16)
    wg_ref[...] = piece(o_gm, wt_ref.shape[0]).T.astype(BF16)


def _wprep_call(w_in_t, layer):
    _, cols, d = w_in_t.shape
    rows = WPREP_ROWS
    n_g = cols - (4 * SB_WIDTH + MLA_Q_LORA + MLA_KV_LORA + MLA_ROPE_DIM)
    row = lambda i: (i, 0)
    widths = (2 * SB_WIDTH, None, SB_WIDTH, MLA_Q_LORA + LANES, MLA_KV_LORA, n_g)
    out_specs = [pl.BlockSpec((SB_WIDTH, rows), lambda i: (0, i)) if wd is None else pl.BlockSpec((rows, wd), row)
                 for wd in widths]
    out_shape = [jax.ShapeDtypeStruct((SB_WIDTH, d) if wd is None else (d, wd), BF16) for wd in widths]
    return pl.pallas_call(
        _wprep_kernel,
        grid=(d // rows,),
        in_specs=[pl.BlockSpec((None, cols, rows), lambda i: (layer, 0, i))],
        out_specs=out_specs,
        out_shape=out_shape,
        compiler_params=pltpu.CompilerParams(
            dimension_semantics=("arbitrary",), vmem_limit_bytes=VMEM_LIMIT_BYTES),
        name="wprep",
    )(w_in_t)


def _proj_kernel(x_ref, gin_ref, wqk_ref, wvsb_ref, wc_ref, wckv_ref, qng_ref, kvng_ref, wq_ref, wkn_ref, wv_ref,
                 cos_ref, sin_ref, qsb_ref, ksb_ref, vsbt_ref, qf_ref, kf_ref, vt_ref):
    h = _rms(x_ref[...], gin_ref[...]).astype(BF16)

    c = _dot(h, wc_ref[...])
    ckv = _dot(h, wckv_ref[...])
    qsb_ref[...] = (_dot(h, wqk_ref[:, :SB_WIDTH]) * (SB_HEAD_DIM ** -0.5 * LOG2_E)).astype(BF16)
    cq = _rms(c[:, :MLA_Q_LORA], qng_ref[...]).astype(BF16)
    ckv = _rms(ckv, kvng_ref[...]).astype(BF16)

    def rope(v, cos, sin_lo, sin_hi):
        return (v * cos + pltpu.roll(v, LANES - MLA_ROPE_DIM // 2, 1) * sin_lo
                + pltpu.roll(v, MLA_ROPE_DIM // 2, 1) * sin_hi)

    lane = lax.broadcasted_iota(jnp.int32, cos_ref.shape, 1)
    ck_t, sin_t = cos_ref[...], sin_ref[...]
    sk_lo = jnp.where(lane < MLA_NOPE_DIM + MLA_ROPE_DIM // 2, sin_t, 0.0)
    sk_hi = sin_t - sk_lo
    scale = MLA_QK_DIM ** -0.5 * LOG2_E
    cq_t = jnp.where(lane < MLA_NOPE_DIM, scale, ck_t * scale)
    sq_lo, sq_hi = sk_lo * scale, sk_hi * scale

    k_rope = rope(c[:, MLA_Q_LORA:], ck_t, sk_lo, sk_hi)
    q = _dot(cq, wq_ref[...])
    kn = _dot(ckv, wkn_ref[...])
    v_t = lax.dot_general(wv_ref[...], ckv, NT_DIMS, preferred_element_type=F32)
    pad_rows = (lax.broadcasted_iota(jnp.int32, (V_ROWS - MLA_V_DIM, v_t.shape[1]), 0) == 0).astype(BF16)
    for hd in range(MLA_HEADS):
        vt_ref[hd, :MLA_V_DIM, :] = v_t[hd * MLA_V_DIM:(hd + 1) * MLA_V_DIM, :].astype(BF16)
        vt_ref[hd, MLA_V_DIM:, :] = pad_rows
    ksb_ref[...] = _dot(h, wqk_ref[:, SB_WIDTH:]).astype(BF16)
    v_sb_t = lax.dot_general(wvsb_ref[...], h, NT_DIMS, preferred_element_type=F32)
    for half in range(v_sb_t.shape[1] // SB_BLOCK):
        vsbt_ref[half] = v_sb_t[:, half * SB_BLOCK:(half + 1) * SB_BLOCK].astype(BF16)
    for hd in range(MLA_HEADS):
        sl = slice(hd * LANES, (hd + 1) * LANES)
        qf_ref[:, sl] = rope(q[:, sl], cq_t, sq_lo, sq_hi).astype(BF16)
        kf_ref[:, sl] = (kn[:, sl] + k_rope).astype(BF16)


def _proj_call(x2, gin, wts, tabs):
    n, d = x2.shape
    s = tabs[0].shape[0]
    tm = min(TOKEN_TILE, s)
    row = lambda i: (i, 0)
    fix = lambda i: (0, 0)
    w_specs = [pl.BlockSpec(w.shape, fix) for w in wts]
    t_specs = [pl.BlockSpec((tm, LANES), lambda i: (i % (s // tm), 0)) for _ in tabs]
    vsbt_spec = pl.BlockSpec((None, tm // SB_BLOCK, SB_WIDTH, SB_BLOCK), lambda i: (i, 0, 0, 0))
    vsbt_shape = jax.ShapeDtypeStruct((n // tm, tm // SB_BLOCK, SB_WIDTH, SB_BLOCK), BF16)
    vt_spec = pl.BlockSpec((None, MLA_HEADS, V_ROWS, tm), lambda i: (i, 0, 0, 0))
    vt_shape = jax.ShapeDtypeStruct((n // tm, MLA_HEADS, V_ROWS, tm), BF16)
    tile = lambda w: pl.BlockSpec((tm, w), row)
    sds = lambda w: jax.ShapeDtypeStruct((n, w), BF16)
    return pl.pallas_call(
        _proj_kernel,
        grid=(n // tm,),
        in_specs=[pl.BlockSpec((tm, d), row), pl.BlockSpec(gin.shape, fix)] + w_specs + t_specs,
        out_specs=[tile(SB_WIDTH), tile(SB_WIDTH), vsbt_spec, tile(MLA_HEADS * LANES), tile(MLA_HEADS * LANES),
                   vt_spec],
        out_shape=[sds(SB_WIDTH), sds(SB_WIDTH), vsbt_shape, sds(MLA_HEADS * LANES), sds(MLA_HEADS * LANES),
                   vt_shape],
        compiler_params=pltpu.CompilerParams(
            dimension_semantics=("arbitrary",), vmem_limit_bytes=VMEM_LIMIT_BYTES),
        name="proj",
    )(x2, gin, *wts, *tabs)


def _sb_kernel(q_ref, kin_ref, vtin_ref, u_ref, o_ref, k_ref, vt_ref, acc_ref):
    t = SB_BLOCK
    qi = pl.program_id(1)
    k_ref[qi] = kin_ref[...]
    vt_ref[qi] = vtin_ref[...]

    first = lax.broadcasted_iota(jnp.int32, (LANES, t), 0) < SB_HEAD_DIM
    q_heads = []
    for p in range(SB_HEADS // 2):
        q_t = q_ref[:, p * LANES:(p + 1) * LANES].astype(F32).T
        q_heads += [jnp.where(first, q_t, 0.0).astype(BF16), jnp.where(first, 0.0, q_t).astype(BF16)]

    def block(j, r, strict):
        k_blk = k_ref[j]
        heads = range(SB_HEADS)
        z = [_dot(k_blk[:, (hd // 2) * LANES:(hd // 2 + 1) * LANES], q_heads[hd]) for hd in heads]
        log_beta, later, first_term = [], [], []
        for hd in heads:
            neg_log_1m = jnp.maximum(z[hd], 0.0) + jnp.log(1.0 + jnp.exp2(-jnp.abs(z[hd]))) * LOG2_E
            log_beta.append(z[hd] - neg_log_1m)
            if strict is not None:
                neg_log_1m = jnp.where(strict, neg_log_1m, 0.0)
            neg_log_1m = neg_log_1m.astype(BF16)
            later.append(_dot(u_ref[...], neg_log_1m))
            first_term.append(neg_log_1m[0:1, :].astype(F32))
        for hd in heads:
            w = jnp.exp2(log_beta[hd] - later[hd] + r[hd])
            if strict is not None:
                w = jnp.where(strict, w, 0.0)
            vt_blk = vt_ref[j, hd * SB_HEAD_DIM:(hd + 1) * SB_HEAD_DIM, :]
            acc_ref[hd] += _dot(vt_blk, w.astype(BF16))
        return tuple(r[hd] - later[hd][0:1, :] - first_term[hd] for hd in heads)

    def r_max(r):
        return functools.reduce(jnp.maximum, [jnp.max(x) for x in r])

    acc_ref[...] = jnp.zeros_like(acc_ref)
    zero = jnp.zeros((1, t), F32)
    key_pos = lax.broadcasted_iota(jnp.int32, (t, t), 0)
    qry_pos = lax.broadcasted_iota(jnp.int32, (t, t), 1)
    r = block(qi, (zero,) * SB_HEADS, key_pos < qry_pos)

    def cond(c):
        j, rmax, _ = c
        return jnp.logical_and(j >= 0, rmax > -SB_DEAD_LOG2)

    def body(c):
        j, _, r = c
        r = block(j, r, None)
        return j - 1, r_max(r), r

    lax.while_loop(cond, body, (qi - 1, r_max(r), r))

    o_ref[...] = jnp.concatenate([acc_ref[hd] for hd in range(SB_HEADS)], axis=0).T


def _sb_call(q, k, vt, u):
    b, s, w = q.shape
    t = SB_BLOCK
    nblk = s // t
    assert vt.shape == (b, nblk, w, t)
    return pl.pallas_call(
        _sb_kernel,
        grid=(b, nblk),
        in_specs=[
            pl.BlockSpec((None, t, w), lambda bi, qi: (bi, qi, 0)),
            pl.BlockSpec((None, t, w), lambda bi, qi: (bi, qi, 0)),
            pl.BlockSpec((None, None, w, t), lambda bi, qi: (bi, qi, 0, 0)),
            pl.BlockSpec(u.shape, lambda bi, qi: (0, 0)),
        ],
        out_specs=pl.BlockSpec((None, t, w), lambda bi, qi: (bi, qi, 0)),
        out_shape=jax.ShapeDtypeStruct((b, s, w), F32),
        scratch_shapes=[
            pltpu.VMEM((nblk, t, w), BF16),
            pltpu.VMEM((nblk, w, t), BF16),
            pltpu.VMEM((SB_HEADS, SB_HEAD_DIM, t), F32),
        ],
        compiler_params=pltpu.CompilerParams(
            dimension_semantics=("arbitrary", "arbitrary"), vmem_limit_bytes=VMEM_LIMIT_BYTES),
        name="sb_attn",
    )(q, k, vt, u)


def _mla_kernel(q_ref, qn_ref, kin_ref, vtin_ref, o_ref, k_ref, vt_ref, acc_ref, m_ref, s0_ref, s1_ref, smax0_ref,
                smax1_ref, *, nblk):
    t = ATT_BLOCK
    i = pl.program_id(1)
    qi = i - 1
    jn = jnp.minimum(i, nblk - 1)
    k_ref[jn] = kin_ref[...]
    vt_ref[jn] = vtin_ref[...]

    heads = range(HEADS_PER_STEP)
    lanes = lambda hd: slice(hd * LANES, (hd + 1) * LANES)
    halves = (slice(0, t // 2), slice(t // 2, t))
    bufs = ((s0_ref, smax0_ref), (s1_ref, smax1_ref))
    key_chunk = lax.broadcasted_iota(jnp.int32, (t, t), 0) // CHUNK
    qry_chunk = lax.broadcasted_iota(jnp.int32, (t, t), 1) // CHUNK
    allowed = key_chunk <= qry_chunk

    def transposed(src_ref, hd):
        return src_ref[:, lanes(hd)].astype(F32).T.astype(BF16)

    def score_rows(hd, keys, q_t, buf, rows, masked):
        s = _dot(keys, q_t)
        if masked:
            s = jnp.where(allowed[rows], s, -jnp.inf)
        buf[0][hd, rows, :] = s
        return jnp.max(s, axis=0, keepdims=True)

    def rescale(hd, buf):
        m = m_ref[hd]
        m_new = jnp.maximum(m, buf[1][hd])
        m_ref[hd] = m_new
        acc_ref[hd] = jnp.exp2(m - m_new) * acc_ref[hd]
        return m_new

    def accumulate(hd, j, buf, rows, m_new):
        p = jnp.exp2(buf[0][hd, rows, :] - m_new).astype(BF16)
        acc_ref[hd] += _dot(vt_ref[j, hd, :, rows], p)

    def step(hd, next_keys, q_t, masked, buf_next, j_cur, buf_cur):
        m_new = rescale(hd, buf_cur)
        maxes = []
        for rows in halves:
            maxes.append(score_rows(hd, next_keys(rows), q_t, buf_next, rows, masked))
            accumulate(hd, j_cur, buf_cur, rows, m_new)
        buf_next[1][hd] = jnp.maximum(*maxes)

    @pl.when(i == 0)
    def _():
        for hd in heads:
            q_t = transposed(qn_ref, hd)
            maxes = [score_rows(hd, kin_ref[rows, lanes(hd)], q_t, bufs[0], rows, True) for rows in halves]
            bufs[0][1][hd] = jnp.maximum(*maxes)

    @pl.when(i > 0)
    def _():
        q_heads = [transposed(q_ref, hd) for hd in heads]
        acc_ref[...] = jnp.zeros_like(acc_ref)
        m_ref[...] = jnp.full_like(m_ref, -jnp.inf)
        first_visit = (qi * (qi + 1)) // 2
        block_of = lambda k: jnp.where(k == 0, qi, k - 1)

        def body(k, carry):
            for parity in range(2):
                @pl.when((first_visit + k) % 2 == parity)
                def _():
                    j_next = block_of(k + 1)
                    for hd in heads:
                        step(hd, lambda rows: k_ref[j_next, rows, lanes(hd)], q_heads[hd], False,
                             bufs[1 - parity], block_of(k), bufs[parity])
            return carry

        lax.fori_loop(0, qi, body, 0)

        for parity in range(2):
            @pl.when((first_visit + qi) % 2 == parity)
            def _():
                for hd in heads:
                    step(hd, lambda rows: kin_ref[rows, lanes(hd)], transposed(qn_ref, hd), True,
                         bufs[1 - parity], block_of(qi), bufs[parity])

        o_ref[...] = jnp.concatenate(
            [acc_ref[hd, :MLA_V_DIM, :] / acc_ref[hd, MLA_V_DIM:MLA_V_DIM + 1, :] for hd in heads], axis=0).T


def _mla_call(qf, kf, vt):
    b, s, w = qf.shape
    t = ATT_BLOCK
    nblk = s // t
    assert vt.shape == (b, nblk, MLA_HEADS, V_ROWS, t) and HEADS_PER_STEP == MLA_HEADS
    finished = lambda bi, i: (bi, jnp.maximum(i - 1, 0), 0)
    started = lambda bi, i: (bi, jnp.minimum(i, nblk - 1), 0)
    return pl.pallas_call(
        functools.partial(_mla_kernel, nblk=nblk),
        grid=(b, nblk + 1),
        in_specs=[
            pl.BlockSpec((None, t, w), finished),
            pl.BlockSpec((None, t, w), started),
            pl.BlockSpec((None, t, w), started),
            pl.BlockSpec((None, None, MLA_HEADS, V_ROWS, t), lambda bi, i: (bi, jnp.minimum(i, nblk - 1), 0, 0, 0)),
        ],
        out_specs=pl.BlockSpec((None, t, MLA_WIDTH), finished),
        out_shape=jax.ShapeDtypeStruct((b, s, MLA_WIDTH), F32),
        scratch_shapes=[
            pltpu.VMEM((nblk, t, w), BF16),
            pltpu.VMEM((nblk, MLA_HEADS, V_ROWS, t), BF16),
            pltpu.VMEM((HEADS_PER_STEP, V_ROWS, t), F32),
            pltpu.VMEM((HEADS_PER_STEP, 1, t), F32),
            pltpu.VMEM((HEADS_PER_STEP, t, t), F32),
            pltpu.VMEM((HEADS_PER_STEP, t, t), F32),
            pltpu.VMEM((HEADS_PER_STEP, 1, t), F32),
            pltpu.VMEM((HEADS_PER_STEP, 1, t), F32),
        ],
        compiler_params=pltpu.CompilerParams(
            dimension_semantics=("arbitrary", "arbitrary"), vmem_limit_bytes=VMEM_LIMIT_BYTES),
        name="mla_attn",
    )(qf, qf, kf, vt)


def _out_kernel(x_ref, osb_ref, omla_ref, gin_ref, wgsb_ref, wg_ref, bg_ref, wosb_ref, womla_ref, wout_ref,
                gf_ref, o_ref, *, final_norm):
    x = x_ref[...]
    d = x.shape[-1]
    h = _rms(x, gin_ref[...]).astype(BF16)
    a_sb = (osb_ref[...] * jax.nn.silu(_dot(h, wgsb_ref[...]))).astype(BF16)
    a_mla = (omla_ref[...] * jax.nn.silu(_dot(h, wg_ref[:, :MLA_WIDTH]))).astype(BF16)
    g0 = MLA_WIDTH
    g_sb = jax.nn.sigmoid(_dot(h, wg_ref[:, g0:g0 + d]) + bg_ref[:, :d])
    merged = g_sb * _dot(a_sb, wosb_ref[...])
    g_mla = jax.nn.sigmoid(_dot(h, wg_ref[:, g0 + d:g0 + 2 * d]) + bg_ref[:, d:])
    merged = merged + g_mla * _dot(a_mla, womla_ref[...])
    y = x + _dot(merged.astype(BF16), wout_ref[...])
    o_ref[...] = _rms(y, gf_ref[...]) if final_norm else y


def _out_call(x2, osb, omla, gin, wgsb, wg, bg, wosb, womla, wout, gf, final_norm):
    n, d = x2.shape
    tm = min(OUT_TILE, n)
    row = lambda i: (i, 0)
    fix = lambda i: (0, 0)
    consts = (gin, wgsb, wg, bg, wosb, womla, wout, gf)
    return pl.pallas_call(
        functools.partial(_out_kernel, final_norm=final_norm),
        grid=(n // tm,),
        in_specs=[pl.BlockSpec((tm, d), row), pl.BlockSpec((tm, SB_WIDTH), row), pl.BlockSpec((tm, MLA_WIDTH), row)]
                 + [pl.BlockSpec(c.shape, fix, pipeline_mode=pl.Buffered(1)) for c in consts],
        out_specs=pl.BlockSpec((tm, d), row),
        out_shape=jax.ShapeDtypeStruct((n, d), F32),
        compiler_params=pltpu.CompilerParams(
            dimension_semantics=("arbitrary",), vmem_limit_bytes=VMEM_LIMIT_BYTES),
        name="out",
    )(x2, osb, omla, *consts)


def _rope_tables(s):
    half = MLA_ROPE_DIM // 2
    inv_freq = ROPE_THETA ** (-np.arange(half, dtype=np.float64) / half)
    ang = np.arange(s, dtype=np.float64)[:, None] * inv_freq[None, :]
    cos, sin = np.cos(ang), np.sin(ang)
    lanes = lambda lo, hi: np.concatenate(
        [np.zeros((s, MLA_NOPE_DIM)), lo, hi, np.zeros((s, LANES - MLA_QK_DIM))], axis=1).astype(np.float32)
    return jnp.asarray(lanes(cos, cos)), jnp.asarray(lanes(-sin, sin))


def _up_weights(w_q_up, w_kv_up):
    rq = w_q_up.shape[0]
    wq = w_q_up.astype(BF16).reshape(rq, MLA_HEADS, MLA_QK_DIM)
    wq = jnp.concatenate([wq, jnp.zeros((rq, MLA_HEADS, LANES - MLA_QK_DIM), BF16)], axis=-1)
    rkv = w_kv_up.shape[0]
    wkv = w_kv_up.astype(BF16).reshape(rkv, MLA_HEADS, MLA_NOPE_DIM + MLA_V_DIM)
    wkn = jnp.concatenate([wkv[..., :MLA_NOPE_DIM], jnp.zeros((rkv, MLA_HEADS, LANES - MLA_NOPE_DIM), BF16)], axis=-1)
    wv_t = wkv[..., MLA_NOPE_DIM:].reshape(rkv, MLA_WIDTH).T
    return wq.reshape(rq, MLA_HEADS * LANES), wkn.reshape(rkv, MLA_HEADS * LANES), wv_t


def kernel(x, norm_in_g, w_in, b_gate, q_norm_g, w_q_up, kv_norm_g, w_kv_up, w_o_sb, w_o_mla, w_out, norm_f_g):
    b, s, d = x.shape
    depth = w_in.shape[0]
    assert s % ATT_BLOCK == 0 and s % SB_BLOCK == 0 and TOKEN_TILE == ATT_BLOCK
    tabs = _rope_tables(s)
    idx = jnp.arange(SB_BLOCK)
    after_mat = (idx[None, :] > idx[:, None]).astype(BF16)

    w_in_t = jnp.swapaxes(w_in, 1, 2)
    x2 = x.reshape(b * s, d)
    for l in range(depth):
        wqk, wvsb_t, wgsb, wc, wckv, wg = _wprep_call(w_in_t, l)
        wq, wkn, wv_t = _up_weights(w_q_up[l], w_kv_up[l])
        gin = norm_in_g[l][None, :]
        qsb, ksb, vsbt, qf, kf, vt = _proj_call(
            x2, gin,
            (wqk, wvsb_t, wc, wckv, q_norm_g[l][None, :], kv_norm_g[l][None, :], wq, wkn, wv_t),
            tabs)
        r3 = lambda a: a.reshape(b, s, a.shape[-1])
        vsbt = vsbt.reshape(b, s // SB_BLOCK, SB_WIDTH, SB_BLOCK)
        vt = vt.reshape((b, s // ATT_BLOCK) + vt.shape[1:])
        o_sb = _sb_call(r3(qsb), r3(ksb), vsbt, after_mat).reshape(b * s, SB_WIDTH)
        o_mla = _mla_call(r3(qf), r3(kf), vt).reshape(b * s, MLA_WIDTH)
        x2 = _out_call(x2, o_sb, o_mla, gin, wgsb, wg, b_gate[l][None, :], w_o_sb[l].astype(BF16),
                       w_o_mla[l].astype(BF16), w_out[l].astype(BF16), norm_f_g[None, :],
                       final_norm=(l == depth - 1))
    return x2.reshape(b, s, d)
```

```python
import functools

import jax
import jax.numpy as jnp
import numpy as np
from jax import lax
from jax.experimental import pallas as pl
from jax.experimental.pallas import tpu as pltpu

EPS = 1e-6
CHUNK = 64

SB_HEADS = 8
SB_HEAD_DIM = 64
SB_WIDTH = SB_HEADS * SB_HEAD_DIM

MLA_HEADS = 8
MLA_NOPE_DIM = 64
MLA_ROPE_DIM = 32
MLA_V_DIM = 64
MLA_Q_LORA = 384
MLA_KV_LORA = 256
MLA_QK_DIM = MLA_NOPE_DIM + MLA_ROPE_DIM
MLA_WIDTH = MLA_HEADS * MLA_V_DIM
ROPE_THETA = 10000.0

LANES = 128
ATT_BLOCK = 512
SB_BLOCK = 256
SB_DEAD_LOG2 = 160.0
LOG2_E = float(np.log2(np.e))
HEADS_PER_STEP = 8
V_ROWS = 80
TOKEN_TILE = 512
OUT_TILE = 1024
WPREP_ROWS = 256
VMEM_LIMIT_BYTES = 56 * 1024 * 1024

F32 = jnp.float32
BF16 = jnp.bfloat16
NT_DIMS = (((1,), (1,)), ((), ()))


def _dot(a, b):
    return jnp.dot(a, b, preferred_element_type=F32)


def _rms(x, g):
    inv = lax.rsqrt(jnp.mean(x * x, axis=-1, keepdims=True) + EPS)
    return (x * inv) * g


def _wprep_kernel(wt_ref, wqk_ref, wvsbt_ref, wgsb_ref, wc_ref, wckv_ref, wg_ref):
    o_v, o_gsb, o_cq = 2 * SB_WIDTH, 3 * SB_WIDTH, 4 * SB_WIDTH
    o_ckv = o_cq + MLA_Q_LORA
    o_kr = o_ckv + MLA_KV_LORA
    o_gm = o_kr + MLA_ROPE_DIM
    rows = wt_ref.shape[1]
    piece = lambda lo, hi: wt_ref[lo:hi, :]
    wqk_ref[...] = piece(0, o_v).T.astype(BF16)
    wvsbt_ref[...] = piece(o_v, o_gsb).astype(BF16)
    wgsb_ref[...] = piece(o_gsb, o_cq).T.astype(BF16)
    wc_t = jnp.concatenate([piece(o_cq, o_ckv), jnp.zeros((MLA_NOPE_DIM, rows), F32), piece(o_kr, o_gm),
                            jnp.zeros((LANES - MLA_QK_DIM, rows), F32)], axis=0)
    wc_ref[...] = wc_t.T.astype(BF16)
    wckv_ref[...] = piece(o_ckv, o_kr).T.astype(BF16)
    wg_ref[...] = piece(o_gm, wt_ref.shape[0]).T.astype(BF16)


def _wprep_call(w_in_t, layer):
    _, cols, d = w_in_t.shape
    rows = WPREP_ROWS
    n_g = cols - (4 * SB_WIDTH + MLA_Q_LORA + MLA_KV_LORA + MLA_ROPE_DIM)
    row = lambda i: (i, 0)
    widths = (2 * SB_WIDTH, None, SB_WIDTH, MLA_Q_LORA + LANES, MLA_KV_LORA, n_g)
    out_specs = [pl.BlockSpec((SB_WIDTH, rows), lambda i: (0, i)) if wd is None else pl.BlockSpec((rows, wd), row)
                 for wd in widths]
    out_shape = [jax.ShapeDtypeStruct((SB_WIDTH, d) if wd is None else (d, wd), BF16) for wd in widths]
    return pl.pallas_call(
        _wprep_kernel,
        grid=(d // rows,),
        in_specs=[pl.BlockSpec((None, cols, rows), lambda i: (layer, 0, i))],
        out_specs=out_specs,
        out_shape=out_shape,
        compiler_params=pltpu.CompilerParams(
            dimension_semantics=("arbitrary",), vmem_limit_bytes=VMEM_LIMIT_BYTES),
        name="wprep",
    )(w_in_t)


def _proj_kernel(x_ref, gin_ref, wqk_ref, wvsb_ref, wc_ref, wckv_ref, qng_ref, kvng_ref, wq_ref, wkn_ref, wv_ref,
                 cos_ref, sin_ref, qsb_ref, ksb_ref, vsbt_ref, qf_ref, kf_ref, vt_ref):
    h = _rms(x_ref[...], gin_ref[...]).astype(BF16)

    c = _dot(h, wc_ref[...])
    ckv = _dot(h, wckv_ref[...])
    qsb_ref[...] = (_dot(h, wqk_ref[:, :SB_WIDTH]) * (SB_HEAD_DIM ** -0.5 * LOG2_E)).astype(BF16)
    cq = _rms(c[:, :MLA_Q_LORA], qng_ref[...]).astype(BF16)
    ckv = _rms(ckv, kvng_ref[...]).astype(BF16)

    def rope(v, cos, sin_lo, sin_hi):
        return (v * cos + pltpu.roll(v, LANES - MLA_ROPE_DIM // 2, 1) * sin_lo
                + pltpu.roll(v, MLA_ROPE_DIM // 2, 1) * sin_hi)

    lane = lax.broadcasted_iota(jnp.int32, cos_ref.shape, 1)
    ck_t, sin_t = cos_ref[...], sin_ref[...]
    sk_lo = jnp.where(lane < MLA_NOPE_DIM + MLA_ROPE_DIM // 2, sin_t, 0.0)
    sk_hi = sin_t - sk_lo
    scale = MLA_QK_DIM ** -0.5 * LOG2_E
    cq_t = jnp.where(lane < MLA_NOPE_DIM, scale, ck_t * scale)
    sq_lo, sq_hi = sk_lo * scale, sk_hi * scale

    k_rope = rope(c[:, MLA_Q_LORA:], ck_t, sk_lo, sk_hi)
    q = _dot(cq, wq_ref[...])
    kn = _dot(ckv, wkn_ref[...])
    v_t = lax.dot_general(wv_ref[...], ckv, NT_DIMS, preferred_element_type=F32)
    pad_rows = (lax.broadcasted_iota(jnp.int32, (V_ROWS - MLA_V_DIM, v_t.shape[1]), 0) == 0).astype(BF16)
    for hd in range(MLA_HEADS):
        vt_ref[hd, :MLA_V_DIM, :] = v_t[hd * MLA_V_DIM:(hd + 1) * MLA_V_DIM, :].astype(BF16)
        vt_ref[hd, MLA_V_DIM:, :] = pad_rows
    ksb_ref[...] = _dot(h, wqk_ref[:, SB_WIDTH:]).astype(BF16)
    v_sb_t = lax.dot_general(wvsb_ref[...], h, NT_DIMS, preferred_element_type=F32)
    for half in range(v_sb_t.shape[1] // SB_BLOCK):
        vsbt_ref[half] = v_sb_t[:, half * SB_BLOCK:(half + 1) * SB_BLOCK].astype(BF16)
    for hd in range(MLA_HEADS):
        sl = slice(hd * LANES, (hd + 1) * LANES)
        qf_ref[:, sl] = rope(q[:, sl], cq_t, sq_lo, sq_hi).astype(BF16)
        kf_ref[:, sl] = (kn[:, sl] + k_rope).astype(BF16)


def _proj_call(x2, gin, wts, tabs):
    n, d = x2.shape
    s = tabs[0].shape[0]
    tm = min(TOKEN_TILE, s)
    row = lambda i: (i, 0)
    fix = lambda i: (0, 0)
    w_specs = [pl.BlockSpec(w.shape, fix) for w in wts]
    t_specs = [pl.BlockSpec((tm, LANES), lambda i: (i % (s // tm), 0)) for _ in tabs]
    vsbt_spec = pl.BlockSpec((None, tm // SB_BLOCK, SB_WIDTH, SB_BLOCK), lambda i: (i, 0, 0, 0))
    vsbt_shape = jax.ShapeDtypeStruct((n // tm, tm // SB_BLOCK, SB_WIDTH, SB_BLOCK), BF16)
    vt_spec = pl.BlockSpec((None, MLA_HEADS, V_ROWS, tm), lambda i: (i, 0, 0, 0))
    vt_shape = jax.ShapeDtypeStruct((n // tm, MLA_HEADS, V_ROWS, tm), BF16)
    tile = lambda w: pl.BlockSpec((tm, w), row)
    sds = lambda w: jax.ShapeDtypeStruct((n, w), BF16)
    return pl.pallas_call(
        _proj_kernel,
        grid=(n // tm,),
        in_specs=[pl.BlockSpec((tm, d), row), pl.BlockSpec(gin.shape, fix)] + w_specs + t_specs,
        out_specs=[tile(SB_WIDTH), tile(SB_WIDTH), vsbt_spec, tile(MLA_HEADS * LANES), tile(MLA_HEADS * LANES),
                   vt_spec],
        out_shape=[sds(SB_WIDTH), sds(SB_WIDTH), vsbt_shape, sds(MLA_HEADS * LANES), sds(MLA_HEADS * LANES),
                   vt_shape],
        compiler_params=pltpu.CompilerParams(
            dimension_semantics=("arbitrary",), vmem_limit_bytes=VMEM_LIMIT_BYTES),
        name="proj",
    )(x2, gin, *wts, *tabs)


def _sb_kernel(q_ref, kin_ref, vtin_ref, u_ref, o_ref, k_ref, vt_ref, acc_ref):
    t = SB_BLOCK
    qi = pl.program_id(1)
    k_ref[qi] = kin_ref[...]
    vt_ref[qi] = vtin_ref[...]

    first = lax.broadcasted_iota(jnp.int32, (LANES, t), 0) < SB_HEAD_DIM
    q_heads = []
    for p in range(SB_HEADS // 2):
        q_t = q_ref[:, p * LANES:(p + 1) * LANES].astype(F32).T
        q_heads += [jnp.where(first, q_t, 0.0).astype(BF16), jnp.where(first, 0.0, q_t).astype(BF16)]

    half = t // 2
    key_pos = lax.broadcasted_iota(jnp.int32, (half, half), 0)
    qry_pos = lax.broadcasted_iota(jnp.int32, (half, half), 1)
    before = key_pos < qry_pos

    def visit(blocks, r):
        heads = range(SB_HEADS)
        keep = lambda mask, a: a if mask is None else jnp.where(mask, a, 0.0)

        def layout(diagonal):
            if not diagonal:
                return (lambda a: (a,)), (lambda a: a), (None,), (slice(0, t),)
            split = lambda a: (a[:half, :half], a[:half, half:], a[half:, half:])
            join = lambda tl, tr, br: jnp.concatenate(
                [jnp.concatenate([tl, tr], axis=1), jnp.concatenate([jnp.zeros_like(tl), br], axis=1)], axis=0)
            return split, join, (before, None, before), (slice(0, half), slice(half, t), slice(half, t))

        z, log_beta, later, r_in = [], [], [], [r]

        def scores(blk):
            k_blk = k_ref[blocks[blk][0]]
            z.append([_dot(k_blk[:, (hd // 2) * LANES:(hd // 2 + 1) * LANES], q_heads[hd]) for hd in heads])

        def sums(blk):
            _, diagonal, live = blocks[blk]
            split, join, masks, _ = layout(diagonal)
            log_beta.append([])
            later.append([])
            if live is not None:
                r_in[-1] = tuple(jnp.where(live, x, -jnp.inf) for x in r_in[-1])
            r_next = []
            for hd in heads:
                parts = []
                for z_part in split(z[blk][hd]):
                    neg_log_1m = jnp.maximum(z_part, 0.0) + jnp.log(1.0 + jnp.exp2(-jnp.abs(z_part))) * LOG2_E
                    parts.append((neg_log_1m, z_part - neg_log_1m))
                log_beta[-1].append([lb for _, lb in parts])
                neg_log_1m = join(*[keep(mask, sp) for mask, (sp, _) in zip(masks, parts)]).astype(BF16)
                later[-1].append(_dot(u_ref[...], neg_log_1m))
                r_next.append(r_in[-1][hd] - later[-1][hd][0:1, :] - neg_log_1m[0:1, :].astype(F32))
            r_in.append(tuple(r_next))

        def values(blk):
            j, diagonal, _ = blocks[blk]
            split, join, masks, queries = layout(diagonal)
            for hd in heads:
                w = [keep(mask, jnp.exp2(lb - lt + r_in[blk][hd][:, qs]))
                     for mask, qs, lb, lt in zip(masks, queries, log_beta[blk][hd], split(later[blk][hd]))]
                vt_blk = vt_ref[j, hd * SB_HEAD_DIM:(hd + 1) * SB_HEAD_DIM, :]
                acc_ref[hd] += _dot(vt_blk, join(*w).astype(BF16))

        scores(0)
        sums(0)
        for blk in range(1, len(blocks)):
            scores(blk)
            values(blk - 1)
            sums(blk)
        values(len(blocks) - 1)
        return r_in[-1]

    def r_max(r):
        return functools.reduce(jnp.maximum, [jnp.max(x) for x in r])

    acc_ref[...] = jnp.zeros_like(acc_ref)
    zero = jnp.zeros((1, t), F32)
    r = visit([(qi, True, None), (jnp.maximum(qi - 1, 0), False, qi > 0)], (zero,) * SB_HEADS)

    def cond(c):
        j, rmax, _ = c
        return jnp.logical_and(j >= 0, rmax > -SB_DEAD_LOG2)

    def body(c):
        j, _, r = c
        r = visit([(j, False, None)], r)
        return j - 1, r_max(r), r

    lax.while_loop(cond, body, (qi - 2, r_max(r), r))

    o_ref[...] = jnp.concatenate([acc_ref[hd] for hd in range(SB_HEADS)], axis=0).T


def _sb_call(q, k, vt, u):
    b, s, w = q.shape
    t = SB_BLOCK
    nblk = s // t
    assert vt.shape == (b, nblk, w, t)
    return pl.pallas_call(
        _sb_kernel,
        grid=(b, nblk),
        in_specs=[
            pl.BlockSpec((None, t, w), lambda bi, qi: (bi, qi, 0)),
            pl.BlockSpec((None, t, w), lambda bi, qi: (bi, qi, 0)),
            pl.BlockSpec((None, None, w, t), lambda bi, qi: (bi, qi, 0, 0)),
            pl.BlockSpec(u.shape, lambda bi, qi: (0, 0)),
        ],
        out_specs=pl.BlockSpec((None, t, w), lambda bi, qi: (bi, qi, 0)),
        out_shape=jax.ShapeDtypeStruct((b, s, w), F32),
        scratch_shapes=[
            pltpu.VMEM((nblk, t, w), BF16),
            pltpu.VMEM((nblk, w, t), BF16),
            pltpu.VMEM((SB_HEADS, SB_HEAD_DIM, t), F32),
        ],
        compiler_params=pltpu.CompilerParams(
            dimension_semantics=("arbitrary", "arbitrary"), vmem_limit_bytes=VMEM_LIMIT_BYTES),
        name="sb_attn",
    )(q, k, vt, u)


def _mla_kernel(q_ref, kin_ref, vtin_ref, o_ref, k_ref, vt_ref, acc_ref, m_ref, s0_ref, s1_ref, smax0_ref,
                smax1_ref):
    t = ATT_BLOCK
    qi = pl.program_id(1)
    k_ref[qi] = kin_ref[...]
    vt_ref[qi] = vtin_ref[...]

    q_heads = [q_ref[:, hd * LANES:(hd + 1) * LANES].astype(F32).T.astype(BF16) for hd in range(HEADS_PER_STEP)]

    halves = (slice(0, t // 2), slice(t // 2, t))

    def score_rows(hd, j, buf, rows, allowed):
        s = _dot(k_ref[j, rows, hd * LANES:(hd + 1) * LANES], q_heads[hd])
        if allowed is not None:
            s = jnp.where(allowed[rows], s, -jnp.inf)
        buf[0][hd, rows, :] = s
        return jnp.max(s, axis=0, keepdims=True)

    def scores(hd, j, buf, allowed):
        buf[1][hd] = jnp.maximum(*[score_rows(hd, j, buf, rows, allowed) for rows in halves])

    def rescale(hd, buf):
        m = m_ref[hd]
        m_new = jnp.maximum(m, buf[1][hd])
        m_ref[hd] = m_new
        acc_ref[hd] = jnp.exp2(m - m_new) * acc_ref[hd]
        return m_new

    def accumulate(hd, j, buf, rows, m_new):
        p = jnp.exp2(buf[0][hd, rows, :] - m_new).astype(BF16)
        acc_ref[hd] += _dot(vt_ref[j, hd, :, rows], p)

    def update(hd, j, buf):
        m_new = rescale(hd, buf)
        for rows in halves:
            accumulate(hd, j, buf, rows, m_new)

    def step(hd, j_next, buf_next, j_cur, buf_cur):
        m_new = rescale(hd, buf_cur)
        maxes = []
        for rows in halves:
            maxes.append(score_rows(hd, j_next, buf_next, rows, None))
            accumulate(hd, j_cur, buf_cur, rows, m_new)
        buf_next[1][hd] = jnp.maximum(*maxes)

    acc_ref[...] = jnp.zeros_like(acc_ref)
    m_ref[...] = jnp.full_like(m_ref, -jnp.inf)
    key_chunk = lax.broadcasted_iota(jnp.int32, (t, t), 0) // CHUNK
    qry_chunk = lax.broadcasted_iota(jnp.int32, (t, t), 1) // CHUNK
    bufs = ((s0_ref, smax0_ref), (s1_ref, smax1_ref))

    block_of = lambda k: jnp.where(k == 0, qi, k - 1)
    allowed = key_chunk <= qry_chunk
    for hd in range(HEADS_PER_STEP):
        scores(hd, qi, bufs[0], allowed)

    def body(k, carry):
        for parity in range(2):
            @pl.when(k % 2 == parity)
            def _():
                for hd in range(HEADS_PER_STEP):
                    step(hd, block_of(k + 1), bufs[1 - parity], block_of(k), bufs[parity])
        return carry

    lax.fori_loop(0, qi, body, 0)

    for parity in range(2):
        @pl.when(qi % 2 == parity)
        def _():
            for hd in range(HEADS_PER_STEP):
                update(hd, block_of(qi), bufs[parity])

    o_ref[...] = jnp.concatenate(
        [acc_ref[hd, :MLA_V_DIM, :] / acc_ref[hd, MLA_V_DIM:MLA_V_DIM + 1, :] for hd in range(HEADS_PER_STEP)],
        axis=0).T


def _mla_call(qf, kf, vt):
    b, s, w = qf.shape
    t = ATT_BLOCK
    nblk = s // t
    assert vt.shape == (b, nblk, MLA_HEADS, V_ROWS, t) and HEADS_PER_STEP == MLA_HEADS
    return pl.pallas_call(
        _mla_kernel,
        grid=(b, nblk),
        in_specs=[
            pl.BlockSpec((None, t, w), lambda bi, qi: (bi, qi, 0)),
            pl.BlockSpec((None, t, w), lambda bi, qi: (bi, qi, 0)),
            pl.BlockSpec((None, None, MLA_HEADS, V_ROWS, t), lambda bi, qi: (bi, qi, 0, 0, 0)),
        ],
        out_specs=pl.BlockSpec((None, t, MLA_WIDTH), lambda bi, qi: (bi, qi, 0)),
        out_shape=jax.ShapeDtypeStruct((b, s, MLA_WIDTH), F32),
        scratch_shapes=[
            pltpu.VMEM((nblk, t, w), BF16),
            pltpu.VMEM((nblk, MLA_HEADS, V_ROWS, t), BF16),
            pltpu.VMEM((HEADS_PER_STEP, V_ROWS, t), F32),
            pltpu.VMEM((HEADS_PER_STEP, 1, t), F32),
            pltpu.VMEM((HEADS_PER_STEP, t, t), F32),
            pltpu.VMEM((HEADS_PER_STEP, t, t), F32),
            pltpu.VMEM((HEADS_PER_STEP, 1, t), F32),
            pltpu.VMEM((HEADS_PER_STEP, 1, t), F32),
        ],
        compiler_params=pltpu.CompilerParams(
            dimension_semantics=("arbitrary", "arbitrary"), vmem_limit_bytes=VMEM_LIMIT_BYTES),
        name="mla_attn",
    )(qf, kf, vt)


def _out_kernel(x_ref, osb_ref, omla_ref, gin_ref, wgsb_ref, wg_ref, bg_ref, wosb_ref, womla_ref, wout_ref,
                gf_ref, o_ref, *, final_norm):
    x = x_ref[...]
    d = x.shape[-1]
    h = _rms(x, gin_ref[...]).astype(BF16)
    a_sb = (osb_ref[...] * jax.nn.silu(_dot(h, wgsb_ref[...]))).astype(BF16)
    a_mla = (omla_ref[...] * jax.nn.silu(_dot(h, wg_ref[:, :MLA_WIDTH]))).astype(BF16)
    g0 = MLA_WIDTH
    g_sb = jax.nn.sigmoid(_dot(h, wg_ref[:, g0:g0 + d]) + bg_ref[:, :d])
    merged = g_sb * _dot(a_sb, wosb_ref[...])
    g_mla = jax.nn.sigmoid(_dot(h, wg_ref[:, g0 + d:g0 + 2 * d]) + bg_ref[:, d:])
    merged = merged + g_mla * _dot(a_mla, womla_ref[...])
    y = x + _dot(merged.astype(BF16), wout_ref[...])
    o_ref[...] = _rms(y, gf_ref[...]) if final_norm else y


def _out_call(x2, osb, omla, gin, wgsb, wg, bg, wosb, womla, wout, gf, final_norm):
    n, d = x2.shape
    tm = min(OUT_TILE, n)
    row = lambda i: (i, 0)
    fix = lambda i: (0, 0)
    consts = (gin, wgsb, wg, bg, wosb, womla, wout, gf)
    return pl.pallas_call(
        functools.partial(_out_kernel, final_norm=final_norm),
        grid=(n // tm,),
        in_specs=[pl.BlockSpec((tm, d), row), pl.BlockSpec((tm, SB_WIDTH), row), pl.BlockSpec((tm, MLA_WIDTH), row)]
                 + [pl.BlockSpec(c.shape, fix, pipeline_mode=pl.Buffered(1)) for c in consts],
        out_specs=pl.BlockSpec((tm, d), row),
        out_shape=jax.ShapeDtypeStruct((n, d), F32),
        compiler_params=pltpu.CompilerParams(
            dimension_semantics=("arbitrary",), vmem_limit_bytes=VMEM_LIMIT_BYTES),
        name="out",
    )(x2, osb, omla, *consts)


def _rope_tables(s):
    half = MLA_ROPE_DIM // 2
    inv_freq = ROPE_THETA ** (-np.arange(half, dtype=np.float64) / half)
    ang = np.arange(s, dtype=np.float64)[:, None] * inv_freq[None, :]
    cos, sin = np.cos(ang), np.sin(ang)
    lanes = lambda lo, hi: np.concatenate(
        [np.zeros((s, MLA_NOPE_DIM)), lo, hi, np.zeros((s, LANES - MLA_QK_DIM))], axis=1).astype(np.float32)
    return jnp.asarray(lanes(cos, cos)), jnp.asarray(lanes(-sin, sin))


def _up_weights(w_q_up, w_kv_up):
    rq = w_q_up.shape[0]
    wq = w_q_up.astype(BF16).reshape(rq, MLA_HEADS, MLA_QK_DIM)
    wq = jnp.concatenate([wq, jnp.zeros((rq, MLA_HEADS, LANES - MLA_QK_DIM), BF16)], axis=-1)
    rkv = w_kv_up.shape[0]
    wkv = w_kv_up.astype(BF16).reshape(rkv, MLA_HEADS, MLA_NOPE_DIM + MLA_V_DIM)
    wkn = jnp.concatenate([wkv[..., :MLA_NOPE_DIM], jnp.zeros((rkv, MLA_HEADS, LANES - MLA_NOPE_DIM), BF16)], axis=-1)
    wv_t = wkv[..., MLA_NOPE_DIM:].reshape(rkv, MLA_WIDTH).T
    return wq.reshape(rq, MLA_HEADS * LANES), wkn.reshape(rkv, MLA_HEADS * LANES), wv_t


def kernel(x, norm_in_g, w_in, b_gate, q_norm_g, w_q_up, kv_norm_g, w_kv_up, w_o_sb, w_o_mla, w_out, norm_f_g):
    b, s, d = x.shape
    depth = w_in.shape[0]
    assert s % ATT_BLOCK == 0 and s % SB_BLOCK == 0 and TOKEN_TILE == ATT_BLOCK
    tabs = _rope_tables(s)
    idx = jnp.arange(SB_BLOCK)
    after_mat = (idx[None, :] > idx[:, None]).astype(BF16)

    w_in_t = jnp.swapaxes(w_in, 1, 2)
    x2 = x.reshape(b * s, d)
    for l in range(depth):
        wqk, wvsb_t, wgsb, wc, wckv, wg = _wprep_call(w_in_t, l)
        wq, wkn, wv_t = _up_weights(w_q_up[l], w_kv_up[l])
        gin = norm_in_g[l][None, :]
        qsb, ksb, vsbt, qf, kf, vt = _proj_call(
            x2, gin,
            (wqk, wvsb_t, wc, wckv, q_norm_g[l][None, :], kv_norm_g[l][None, :], wq, wkn, wv_t),
            tabs)
        r3 = lambda a: a.reshape(b, s, a.shape[-1])
        vsbt = vsbt.reshape(b, s // SB_BLOCK, SB_WIDTH, SB_BLOCK)
        vt = vt.reshape((b, s // ATT_BLOCK) + vt.shape[1:])
        o_sb = _sb_call(r3(qsb), r3(ksb), vsbt, after_mat).reshape(b * s, SB_WIDTH)
        o_mla = _mla_call(r3(qf), r3(kf), vt).reshape(b * s, MLA_WIDTH)
        x2 = _out_call(x2, o_sb, o_mla, gin, wgsb, wg, b_gate[l][None, :], w_o_sb[l].astype(BF16),
                       w_o_mla[l].astype(BF16), w_out[l].astype(BF16), norm_f_g[None, :],
                       final_norm=(l == depth - 1))
    return x2.reshape(b, s, d)
```

```python
import functools

import jax
import jax.numpy as jnp
import numpy as np
from jax import lax
from jax.experimental import pallas as pl
from jax.experimental.pallas import tpu as pltpu

EPS = 1e-6
CHUNK = 64

SB_HEADS = 8
SB_HEAD_DIM = 64
SB_WIDTH = SB_HEADS * SB_HEAD_DIM

MLA_HEADS = 8
MLA_NOPE_DIM = 64
MLA_ROPE_DIM = 32
MLA_V_DIM = 64
MLA_Q_LORA = 384
MLA_KV_LORA = 256
MLA_QK_DIM = MLA_NOPE_DIM + MLA_ROPE_DIM
MLA_WIDTH = MLA_HEADS * MLA_V_DIM
ROPE_THETA = 10000.0

LANES = 128
ATT_BLOCK = 512
SB_BLOCK = 256
SB_DEAD_LOG2 = 160.0
LOG2_E = float(np.log2(np.e))
HEADS_PER_STEP = 8
V_ROWS = 80
TOKEN_TILE = 512
OUT_TILE = 1024
WPREP_ROWS = 256
VMEM_LIMIT_BYTES = 56 * 1024 * 1024

F32 = jnp.float32
BF16 = jnp.bfloat16
NT_DIMS = (((1,), (1,)), ((), ()))


def _dot(a, b):
    return jnp.dot(a, b, preferred_element_type=F32)


def _rms(x, g):
    inv = lax.rsqrt(jnp.mean(x * x, axis=-1, keepdims=True) + EPS)
    return (x * inv) * g


def _wprep_kernel(wt_ref, wqk_ref, wvsbt_ref, wgsb_ref, wc_ref, wckv_ref, wg_ref):
    o_v, o_gsb, o_cq = 2 * SB_WIDTH, 3 * SB_WIDTH, 4 * SB_WIDTH
    o_ckv = o_cq + MLA_Q_LORA
    o_kr = o_ckv + MLA_KV_LORA
    o_gm = o_kr + MLA_ROPE_DIM
    rows = wt_ref.shape[1]
    piece = lambda lo, hi: wt_ref[lo:hi, :]
    wqk_ref[...] = piece(0, o_v).T.astype(BF16)
    wvsbt_ref[...] = piece(o_v, o_gsb).astype(BF16)
    wgsb_ref[...] = piece(o_gsb, o_cq).T.astype(BF16)
    wc_t = jnp.concatenate([piece(o_cq, o_ckv), jnp.zeros((MLA_NOPE_DIM, rows), F32), piece(o_kr, o_gm),
                            jnp.zeros((LANES - MLA_QK_DIM, rows), F32)], axis=0)
    wc_ref[...] = wc_t.T.astype(BF16)
    wckv_ref[...] = piece(o_ckv, o_kr).T.astype(BF16)
    wg_ref[...] = piece(o_gm, wt_ref.shape[0]).T.astype(BF16)


def _wprep_call(w_in_t, layer):
    _, cols, d = w_in_t.shape
    rows = WPREP_ROWS
    n_g = cols - (4 * SB_WIDTH + MLA_Q_LORA + MLA_KV_LORA + MLA_ROPE_DIM)
    row = lambda i: (i, 0)
    widths = (2 * SB_WIDTH, None, SB_WIDTH, MLA_Q_LORA + LANES, MLA_KV_LORA, n_g)
    out_specs = [pl.BlockSpec((SB_WIDTH, rows), lambda i: (0, i)) if wd is None else pl.BlockSpec((rows, wd), row)
                 for wd in widths]
    out_shape = [jax.ShapeDtypeStruct((SB_WIDTH, d) if wd is None else (d, wd), BF16) for wd in widths]
    return pl.pallas_call(
        _wprep_kernel,
        grid=(d // rows,),
        in_specs=[pl.BlockSpec((None, cols, rows), lambda i: (layer, 0, i))],
        out_specs=out_specs,
        out_shape=out_shape,
        compiler_params=pltpu.CompilerParams(
            dimension_semantics=("arbitrary",), vmem_limit_bytes=VMEM_LIMIT_BYTES),
        name="wprep",
    )(w_in_t)


def _proj_kernel(x_ref, gin_ref, wqk_ref, wvsb_ref, wc_ref, wckv_ref, qng_ref, kvng_ref, wq_ref, wkn_ref, wv_ref,
                 cos_ref, sin_ref, qsb_ref, ksb_ref, vsbt_ref, qf_ref, kf_ref, vt_ref):
    h = _rms(x_ref[...], gin_ref[...]).astype(BF16)

    c = _dot(h, wc_ref[...])
    ckv = _dot(h, wckv_ref[...])
    qsb_ref[...] = (_dot(h, wqk_ref[:, :SB_WIDTH]) * (SB_HEAD_DIM ** -0.5 * LOG2_E)).astype(BF16)
    cq = _rms(c[:, :MLA_Q_LORA], qng_ref[...]).astype(BF16)
    ckv = _rms(ckv, kvng_ref[...]).astype(BF16)

    def rope(v, cos, sin_lo, sin_hi):
        return (v * cos + pltpu.roll(v, LANES - MLA_ROPE_DIM // 2, 1) * sin_lo
                + pltpu.roll(v, MLA_ROPE_DIM // 2, 1) * sin_hi)

    lane = lax.broadcasted_iota(jnp.int32, cos_ref.shape, 1)
    ck_t, sin_t = cos_ref[...], sin_ref[...]
    sk_lo = jnp.where(lane < MLA_NOPE_DIM + MLA_ROPE_DIM // 2, sin_t, 0.0)
    sk_hi = sin_t - sk_lo
    scale = MLA_QK_DIM ** -0.5 * LOG2_E
    cq_t = jnp.where(lane < MLA_NOPE_DIM, scale, ck_t * scale)
    sq_lo, sq_hi = sk_lo * scale, sk_hi * scale

    k_rope = rope(c[:, MLA_Q_LORA:], ck_t, sk_lo, sk_hi)
    q = _dot(cq, wq_ref[...])
    kn = _dot(ckv, wkn_ref[...])
    v_t = lax.dot_general(wv_ref[...], ckv, NT_DIMS, preferred_element_type=F32)
    pad_rows = (lax.broadcasted_iota(jnp.int32, (V_ROWS - MLA_V_DIM, v_t.shape[1]), 0) == 0).astype(BF16)
    for hd in range(MLA_HEADS):
        vt_ref[hd, :MLA_V_DIM, :] = v_t[hd * MLA_V_DIM:(hd + 1) * MLA_V_DIM, :].astype(BF16)
        vt_ref[hd, MLA_V_DIM:, :] = pad_rows
    ksb_ref[...] = _dot(h, wqk_ref[:, SB_WIDTH:]).astype(BF16)
    v_sb_t = lax.dot_general(wvsb_ref[...], h, NT_DIMS, preferred_element_type=F32)
    for half in range(v_sb_t.shape[1] // SB_BLOCK):
        vsbt_ref[half] = v_sb_t[:, half * SB_BLOCK:(half + 1) * SB_BLOCK].astype(BF16)
    for hd in range(MLA_HEADS):
        sl = slice(hd * LANES, (hd + 1) * LANES)
        qf_ref[:, sl] = rope(q[:, sl], cq_t, sq_lo, sq_hi).astype(BF16)
        kf_ref[:, sl] = (kn[:, sl] + k_rope).astype(BF16)


def _proj_call(x2, gin, wts, tabs):
    n, d = x2.shape
    s = tabs[0].shape[0]
    tm = min(TOKEN_TILE, s)
    row = lambda i: (i, 0)
    fix = lambda i: (0, 0)
    w_specs = [pl.BlockSpec(w.shape, fix) for w in wts]
    t_specs = [pl.BlockSpec((tm, LANES), lambda i: (i % (s // tm), 0)) for _ in tabs]
    vsbt_spec = pl.BlockSpec((None, tm // SB_BLOCK, SB_WIDTH, SB_BLOCK), lambda i: (i, 0, 0, 0))
    vsbt_shape = jax.ShapeDtypeStruct((n // tm, tm // SB_BLOCK, SB_WIDTH, SB_BLOCK), BF16)
    vt_spec = pl.BlockSpec((None, MLA_HEADS, V_ROWS, tm), lambda i: (i, 0, 0, 0))
    vt_shape = jax.ShapeDtypeStruct((n // tm, MLA_HEADS, V_ROWS, tm), BF16)
    tile = lambda w: pl.BlockSpec((tm, w), row)
    sds = lambda w: jax.ShapeDtypeStruct((n, w), BF16)
    return pl.pallas_call(
        _proj_kernel,
        grid=(n // tm,),
        in_specs=[pl.BlockSpec((tm, d), row), pl.BlockSpec(gin.shape, fix)] + w_specs + t_specs,
        out_specs=[tile(SB_WIDTH), tile(SB_WIDTH), vsbt_spec, tile(MLA_HEADS * LANES), tile(MLA_HEADS * LANES),
                   vt_spec],
        out_shape=[sds(SB_WIDTH), sds(SB_WIDTH), vsbt_shape, sds(MLA_HEADS * LANES), sds(MLA_HEADS * LANES),
                   vt_shape],
        compiler_params=pltpu.CompilerParams(
            dimension_semantics=("arbitrary",), vmem_limit_bytes=VMEM_LIMIT_BYTES),
        name="proj",
    )(x2, gin, *wts, *tabs)


def _sb_kernel(q_ref, kin_ref, vtin_ref, u_ref, o_ref, k_ref, vt_ref, acc_ref):
    t = SB_BLOCK
    qi = pl.program_id(1)
    k_ref[qi] = kin_ref[...]
    vt_ref[qi] = vtin_ref[...]

    first = lax.broadcasted_iota(jnp.int32, (LANES, t), 0) < SB_HEAD_DIM
    q_heads = []
    for p in range(SB_HEADS // 2):
        q_t = q_ref[:, p * LANES:(p + 1) * LANES].astype(F32).T
        q_heads += [jnp.where(first, q_t, 0.0).astype(BF16), jnp.where(first, 0.0, q_t).astype(BF16)]

    half = t // 2
    key_pos = lax.broadcasted_iota(jnp.int32, (half, half), 0)
    qry_pos = lax.broadcasted_iota(jnp.int32, (half, half), 1)
    before = key_pos < qry_pos

    def visit(blocks, r):
        heads = range(SB_HEADS)
        keep = lambda mask, a: a if mask is None else jnp.where(mask, a, 0.0)

        def layout(diagonal):
            if not diagonal:
                return (lambda a: (a,)), (lambda a: a), (None,), (slice(0, t),)
            split = lambda a: (a[:half, :half], a[:half, half:], a[half:, half:])
            join = lambda tl, tr, br: jnp.concatenate(
                [jnp.concatenate([tl, tr], axis=1), jnp.concatenate([jnp.zeros_like(tl), br], axis=1)], axis=0)
            return split, join, (before, None, before), (slice(0, half), slice(half, t), slice(half, t))

        z, log_beta, later, r_in = [], [], [], [r]

        def scores(blk):
            k_blk = k_ref[blocks[blk][0]]
            z.append([_dot(k_blk[:, (hd // 2) * LANES:(hd // 2 + 1) * LANES], q_heads[hd]) for hd in heads])

        def sums(blk):
            _, diagonal, live = blocks[blk]
            split, join, masks, _ = layout(diagonal)
            log_beta.append([])
            later.append([])
            if live is not None:
                r_in[-1] = tuple(jnp.where(live, x, -jnp.inf) for x in r_in[-1])
            r_next = []
            for hd in heads:
                parts = []
                for z_part in split(z[blk][hd]):
                    neg_log_1m = jnp.maximum(z_part, 0.0) + jnp.log(1.0 + jnp.exp2(-jnp.abs(z_part))) * LOG2_E
                    parts.append((neg_log_1m, z_part - neg_log_1m))
                log_beta[-1].append([lb for _, lb in parts])
                neg_log_1m = join(*[keep(mask, sp) for mask, (sp, _) in zip(masks, parts)]).astype(BF16)
                later[-1].append(_dot(u_ref[...], neg_log_1m))
                r_next.append(r_in[-1][hd] - later[-1][hd][0:1, :] - neg_log_1m[0:1, :].astype(F32))
            r_in.append(tuple(r_next))

        def values(blk):
            j, diagonal, _ = blocks[blk]
            split, join, masks, queries = layout(diagonal)
            for hd in heads:
                w = [keep(mask, jnp.exp2(lb - lt + r_in[blk][hd][:, qs]))
                     for mask, qs, lb, lt in zip(masks, queries, log_beta[blk][hd], split(later[blk][hd]))]
                vt_blk = vt_ref[j, hd * SB_HEAD_DIM:(hd + 1) * SB_HEAD_DIM, :]
                acc_ref[hd] += _dot(vt_blk, join(*w).astype(BF16))

        scores(0)
        sums(0)
        for blk in range(1, len(blocks)):
            scores(blk)
            values(blk - 1)
            sums(blk)
        values(len(blocks) - 1)
        return r_in[-1]

    def r_max(r):
        return functools.reduce(jnp.maximum, [jnp.max(x) for x in r])

    acc_ref[...] = jnp.zeros_like(acc_ref)
    zero = jnp.zeros((1, t), F32)
    r = visit([(qi, True, None), (jnp.maximum(qi - 1, 0), False, qi > 0)], (zero,) * SB_HEADS)

    def cond(c):
        j, rmax, _ = c
        return jnp.logical_and(j >= 0, rmax > -SB_DEAD_LOG2)

    def body(c):
        j, _, r = c
        r = visit([(j, False, None)], r)
        return j - 1, r_max(r), r

    lax.while_loop(cond, body, (qi - 2, r_max(r), r))

    o_ref[...] = jnp.concatenate([acc_ref[hd] for hd in range(SB_HEADS)], axis=0).T


def _sb_call(q, k, vt, u):
    b, s, w = q.shape
    t = SB_BLOCK
    nblk = s // t
    assert vt.shape == (b, nblk, w, t)
    return pl.pallas_call(
        _sb_kernel,
        grid=(b, nblk),
        in_specs=[
            pl.BlockSpec((None, t, w), lambda bi, qi: (bi, qi, 0)),
            pl.BlockSpec((None, t, w), lambda bi, qi: (bi, qi, 0)),
            pl.BlockSpec((None, None, w, t), lambda bi, qi: (bi, qi, 0, 0)),
            pl.BlockSpec(u.shape, lambda bi, qi: (0, 0)),
        ],
        out_specs=pl.BlockSpec((None, t, w), lambda bi, qi: (bi, qi, 0)),
        out_shape=jax.ShapeDtypeStruct((b, s, w), F32),
        scratch_shapes=[
            pltpu.VMEM((nblk, t, w), BF16),
            pltpu.VMEM((nblk, w, t), BF16),
            pltpu.VMEM((SB_HEADS, SB_HEAD_DIM, t), F32),
        ],
        compiler_params=pltpu.CompilerParams(
            dimension_semantics=("arbitrary", "arbitrary"), vmem_limit_bytes=VMEM_LIMIT_BYTES),
        name="sb_attn",
    )(q, k, vt, u)


def _mla_kernel(q_ref, kin_ref, vtin_ref, o_ref, k_ref, vt_ref, acc_ref, m_ref, s0_ref, s1_ref, smax0_ref,
                smax1_ref):
    t = ATT_BLOCK
    qi = pl.program_id(1)
    k_ref[qi] = kin_ref[...]
    vt_ref[qi] = vtin_ref[...]

    q_heads = [q_ref[:, hd * LANES:(hd + 1) * LANES].astype(F32).T.astype(BF16) for hd in range(HEADS_PER_STEP)]

    halves = (slice(0, t // 2), slice(t // 2, t))

    def score_rows(hd, j, buf, rows, allowed):
        s = _dot(k_ref[j, rows, hd * LANES:(hd + 1) * LANES], q_heads[hd])
        if allowed is not None:
            s = jnp.where(allowed[rows], s, -jnp.inf)
        buf[0][hd, rows, :] = s
        return jnp.max(s, axis=0, keepdims=True)

    def scores(hd, j, buf, allowed):
        buf[1][hd] = jnp.maximum(*[score_rows(hd, j, buf, rows, allowed) for rows in halves])

    def rescale(hd, buf):
        m = m_ref[hd]
        m_new = jnp.maximum(m, buf[1][hd])
        m_ref[hd] = m_new
        acc_ref[hd] = jnp.exp2(m - m_new) * acc_ref[hd]
        return m_new

    def accumulate(hd, j, buf, rows, m_new):
        p = jnp.exp2(buf[0][hd, rows, :] - m_new).astype(BF16)
        acc_ref[hd] += _dot(vt_ref[j, hd, :, rows], p)

    def update(hd, j, buf):
        m_new = rescale(hd, buf)
        for rows in halves:
            accumulate(hd, j, buf, rows, m_new)

    def step(hd, j_next, buf_next, j_cur, buf_cur):
        m_new = rescale(hd, buf_cur)
        maxes = []
        for rows in halves:
            maxes.append(score_rows(hd, j_next, buf_next, rows, None))
            accumulate(hd, j_cur, buf_cur, rows, m_new)
        buf_next[1][hd] = jnp.maximum(*maxes)

    acc_ref[...] = jnp.zeros_like(acc_ref)
    m_ref[...] = jnp.full_like(m_ref, -jnp.inf)
    key_chunk = lax.broadcasted_iota(jnp.int32, (t, t), 0) // CHUNK
    qry_chunk = lax.broadcasted_iota(jnp.int32, (t, t), 1) // CHUNK
    bufs = ((s0_ref, smax0_ref), (s1_ref, smax1_ref))

    block_of = lambda k: jnp.where(k == 0, qi, k - 1)
    allowed = key_chunk <= qry_chunk
    for hd in range(HEADS_PER_STEP):
        scores(hd, qi, bufs[0], allowed)

    def consume(k, parity):
        for hd in range(HEADS_PER_STEP):
            step(hd, block_of(k + 1), bufs[1 - parity], block_of(k), bufs[parity])

    def body(pair, carry):
        consume(2 * pair, 0)
        consume(2 * pair + 1, 1)
        return carry

    lax.fori_loop(0, qi // 2, body, 0)

    @pl.when(qi % 2 == 1)
    def _():
        consume(qi - 1, 0)

    for parity in range(2):
        @pl.when(qi % 2 == parity)
        def _():
            for hd in range(HEADS_PER_STEP):
                update(hd, block_of(qi), bufs[parity])

    o_ref[...] = jnp.concatenate(
        [acc_ref[hd, :MLA_V_DIM, :] / acc_ref[hd, MLA_V_DIM:MLA_V_DIM + 1, :] for hd in range(HEADS_PER_STEP)],
        axis=0).T


def _mla_call(qf, kf, vt):
    b, s, w = qf.shape
    t = ATT_BLOCK
    nblk = s // t
    assert vt.shape == (b, nblk, MLA_HEADS, V_ROWS, t) and HEADS_PER_STEP == MLA_HEADS
    return pl.pallas_call(
        _mla_kernel,
        grid=(b, nblk),
        in_specs=[
            pl.BlockSpec((None, t, w), lambda bi, qi: (bi, qi, 0)),
            pl.BlockSpec((None, t, w), lambda bi, qi: (bi, qi, 0)),
            pl.BlockSpec((None, None, MLA_HEADS, V_ROWS, t), lambda bi, qi: (bi, qi, 0, 0, 0)),
        ],
        out_specs=pl.BlockSpec((None, t, MLA_WIDTH), lambda bi, qi: (bi, qi, 0)),
        out_shape=jax.ShapeDtypeStruct((b, s, MLA_WIDTH), F32),
        scratch_shapes=[
            pltpu.VMEM((nblk, t, w), BF16),
            pltpu.VMEM((nblk, MLA_HEADS, V_ROWS, t), BF16),
            pltpu.VMEM((HEADS_PER_STEP, V_ROWS, t), F32),
            pltpu.VMEM((HEADS_PER_STEP, 1, t), F32),
            pltpu.VMEM((HEADS_PER_STEP, t, t), F32),
            pltpu.VMEM((HEADS_PER_STEP, t, t), F32),
            pltpu.VMEM((HEADS_PER_STEP, 1, t), F32),
            pltpu.VMEM((HEADS_PER_STEP, 1, t), F32),
        ],
        compiler_params=pltpu.CompilerParams(
            dimension_semantics=("arbitrary", "arbitrary"), vmem_limit_bytes=VMEM_LIMIT_BYTES),
        name="mla_attn",
    )(qf, kf, vt)


def _out_kernel(x_ref, osb_ref, omla_ref, gin_ref, wgsb_ref, wg_ref, bg_ref, wosb_ref, womla_ref, wout_ref,
                gf_ref, o_ref, *, final_norm):
    x = x_ref[...]
    d = x.shape[-1]
    h = _rms(x, gin_ref[...]).astype(BF16)
    a_sb = (osb_ref[...] * jax.nn.silu(_dot(h, wgsb_ref[...]))).astype(BF16)
    a_mla = (omla_ref[...] * jax.nn.silu(_dot(h, wg_ref[:, :MLA_WIDTH]))).astype(BF16)
    g0 = MLA_WIDTH
    g_sb = jax.nn.sigmoid(_dot(h, wg_ref[:, g0:g0 + d]) + bg_ref[:, :d])
    merged = g_sb * _dot(a_sb, wosb_ref[...])
    g_mla = jax.nn.sigmoid(_dot(h, wg_ref[:, g0 + d:g0 + 2 * d]) + bg_ref[:, d:])
    merged = merged + g_mla * _dot(a_mla, womla_ref[...])
    y = x + _dot(merged.astype(BF16), wout_ref[...])
    o_ref[...] = _rms(y, gf_ref[...]) if final_norm else y


def _out_call(x2, osb, omla, gin, wgsb, wg, bg, wosb, womla, wout, gf, final_norm):
    n, d = x2.shape
    tm = min(OUT_TILE, n)
    row = lambda i: (i, 0)
    fix = lambda i: (0, 0)
    consts = (gin, wgsb, wg, bg, wosb, womla, wout, gf)
    return pl.pallas_call(
        functools.partial(_out_kernel, final_norm=final_norm),
        grid=(n // tm,),
        in_specs=[pl.BlockSpec((tm, d), row), pl.BlockSpec((tm, SB_WIDTH), row), pl.BlockSpec((tm, MLA_WIDTH), row)]
                 + [pl.BlockSpec(c.shape, fix, pipeline_mode=pl.Buffered(1)) for c in consts],
        out_specs=pl.BlockSpec((tm, d), row),
        out_shape=jax.ShapeDtypeStruct((n, d), F32),
        compiler_params=pltpu.CompilerParams(
            dimension_semantics=("arbitrary",), vmem_limit_bytes=VMEM_LIMIT_BYTES),
        name="out",
    )(x2, osb, omla, *consts)


def _rope_tables(s):
    half = MLA_ROPE_DIM // 2
    inv_freq = ROPE_THETA ** (-np.arange(half, dtype=np.float64) / half)
    ang = np.arange(s, dtype=np.float64)[:, None] * inv_freq[None, :]
    cos, sin = np.cos(ang), np.sin(ang)
    lanes = lambda lo, hi: np.concatenate(
        [np.zeros((s, MLA_NOPE_DIM)), lo, hi, np.zeros((s, LANES - MLA_QK_DIM))], axis=1).astype(np.float32)
    return jnp.asarray(lanes(cos, cos)), jnp.asarray(lanes(-sin, sin))


def _up_weights(w_q_up, w_kv_up):
    rq = w_q_up.shape[0]
    wq = w_q_up.astype(BF16).reshape(rq, MLA_HEADS, MLA_QK_DIM)
    wq = jnp.concatenate([wq, jnp.zeros((rq, MLA_HEADS, LANES - MLA_QK_DIM), BF16)], axis=-1)
    rkv = w_kv_up.shape[0]
    wkv = w_kv_up.astype(BF16).reshape(rkv, MLA_HEADS, MLA_NOPE_DIM + MLA_V_DIM)
    wkn = jnp.concatenate([wkv[..., :MLA_NOPE_DIM], jnp.zeros((rkv, MLA_HEADS, LANES - MLA_NOPE_DIM), BF16)], axis=-1)
    wv_t = wkv[..., MLA_NOPE_DIM:].reshape(rkv, MLA_WIDTH).T
    return wq.reshape(rq, MLA_HEADS * LANES), wkn.reshape(rkv, MLA_HEADS * LANES), wv_t


def kernel(x, norm_in_g, w_in, b_gate, q_norm_g, w_q_up, kv_norm_g, w_kv_up, w_o_sb, w_o_mla, w_out, norm_f_g):
    b, s, d = x.shape
    depth = w_in.shape[0]
    assert s % ATT_BLOCK == 0 and s % SB_BLOCK == 0 and TOKEN_TILE == ATT_BLOCK
    tabs = _rope_tables(s)
    idx = jnp.arange(SB_BLOCK)
    after_mat = (idx[None, :] > idx[:, None]).astype(BF16)

    w_in_t = jnp.swapaxes(w_in, 1, 2)
    x2 = x.reshape(b * s, d)
    for l in range(depth):
        wqk, wvsb_t, wgsb, wc, wckv, wg = _wprep_call(w_in_t, l)
        wq, wkn, wv_t = _up_weights(w_q_up[l], w_kv_up[l])
        gin = norm_in_g[l][None, :]
        qsb, ksb, vsbt, qf, kf, vt = _proj_call(
            x2, gin,
            (wqk, wvsb_t, wc, wckv, q_norm_g[l][None, :], kv_norm_g[l][None, :], wq, wkn, wv_t),
            tabs)
        r3 = lambda a: a.reshape(b, s, a.shape[-1])
        vsbt = vsbt.reshape(b, s // SB_BLOCK, SB_WIDTH, SB_BLOCK)
        vt = vt.reshape((b, s // ATT_BLOCK) + vt.shape[1:])
        o_sb = _sb_call(r3(qsb), r3(ksb), vsbt, after_mat).reshape(b * s, SB_WIDTH)
        o_mla = _mla_call(r3(qf), r3(kf), vt).reshape(b * s, MLA_WIDTH)
        x2 = _out_call(x2, o_sb, o_mla, gin, wgsb, wg, b_gate[l][None, :], w_o_sb[l].astype(BF16),
                       w_o_mla[l].astype(BF16), w_out[l].astype(BF16), norm_f_g[None, :],
                       final_norm=(l == depth - 1))
    return x2.reshape(b, s, d)
```

```python
import functools

import jax
import jax.numpy as jnp
import numpy as np
from jax import lax
from jax.experimental import pallas as pl
from jax.experimental.pallas import tpu as pltpu

EPS = 1e-6
CHUNK = 64

SB_HEADS = 8
SB_HEAD_DIM = 64
SB_WIDTH = SB_HEADS * SB_HEAD_DIM

MLA_HEADS = 8
MLA_NOPE_DIM = 64
MLA_ROPE_DIM = 32
MLA_V_DIM = 64
MLA_Q_LORA = 384
MLA_KV_LORA = 256
MLA_QK_DIM = MLA_NOPE_DIM + MLA_ROPE_DIM
MLA_WIDTH = MLA_HEADS * MLA_V_DIM
ROPE_THETA = 10000.0

LANES = 128
ATT_BLOCK = 512
SB_BLOCK = 256
SB_DEAD_LOG2 = 160.0
LOG2_E = float(np.log2(np.e))
HEADS_PER_STEP = 8
V_ROWS = 80
TOKEN_TILE = 512
OUT_TILE = 1024
WPREP_ROWS = 256
VMEM_LIMIT_BYTES = 56 * 1024 * 1024

F32 = jnp.float32
BF16 = jnp.bfloat16
NT_DIMS = (((1,), (1,)), ((), ()))


def _dot(a, b):
    return jnp.dot(a, b, preferred_element_type=F32)


def _rms(x, g):
    inv = lax.rsqrt(jnp.mean(x * x, axis=-1, keepdims=True) + EPS)
    return (x * inv) * g


def _wprep_kernel(wt_ref, wqk_ref, wvsbt_ref, wgsb_ref, wc_ref, wckv_ref, wg_ref):
    o_v, o_gsb, o_cq = 2 * SB_WIDTH, 3 * SB_WIDTH, 4 * SB_WIDTH
    o_ckv = o_cq + MLA_Q_LORA
    o_kr = o_ckv + MLA_KV_LORA
    o_gm = o_kr + MLA_ROPE_DIM
    rows = wt_ref.shape[1]
    piece = lambda lo, hi: wt_ref[lo:hi, :]
    wqk_ref[...] = piece(0, o_v).T.astype(BF16)
    wvsbt_ref[...] = piece(o_v, o_gsb).astype(BF16)
    wgsb_ref[...] = piece(o_gsb, o_cq).T.astype(BF16)
    wc_t = jnp.concatenate([piece(o_cq, o_ckv), jnp.zeros((MLA_NOPE_DIM, rows), F32), piece(o_kr, o_gm),
                            jnp.zeros((LANES - MLA_QK_DIM, rows), F32)], axis=0)
    wc_ref[...] = wc_t.T.astype(BF16)
    wckv_ref[...] = piece(o_ckv, o_kr).T.astype(BF16)
    wg_ref[...] = piece(o_gm, wt_ref.shape[0]).T.astype(BF16)


def _wprep_call(w_in_t, layer):
    _, cols, d = w_in_t.shape
    rows = WPREP_ROWS
    n_g = cols - (4 * SB_WIDTH + MLA_Q_LORA + MLA_KV_LORA + MLA_ROPE_DIM)
    row = lambda i: (i, 0)
    widths = (2 * SB_WIDTH, None, SB_WIDTH, MLA_Q_LORA + LANES, MLA_KV_LORA, n_g)
    out_specs = [pl.BlockSpec((SB_WIDTH, rows), lambda i: (0, i)) if wd is None else pl.BlockSpec((rows, wd), row)
                 for wd in widths]
    out_shape = [jax.ShapeDtypeStruct((SB_WIDTH, d) if wd is None else (d, wd), BF16) for wd in widths]
    return pl.pallas_call(
        _wprep_kernel,
        grid=(d // rows,),
        in_specs=[pl.BlockSpec((None, cols, rows), lambda i: (layer, 0, i))],
        out_specs=out_specs,
        out_shape=out_shape,
        compiler_params=pltpu.CompilerParams(
            dimension_semantics=("arbitrary",), vmem_limit_bytes=VMEM_LIMIT_BYTES),
        name="wprep",
    )(w_in_t)


def _proj_kernel(x_ref, gin_ref, wqk_ref, wvsb_ref, wc_ref, wckv_ref, qng_ref, kvng_ref, wq_ref, wkn_ref, wv_ref,
                 cos_ref, sin_ref, qsb_ref, ksb_ref, vsbt_ref, qf_ref, kf_ref, vt_ref):
    h = _rms(x_ref[...], gin_ref[...]).astype(BF16)

    c = _dot(h, wc_ref[...])
    ckv = _dot(h, wckv_ref[...])
    qsb_ref[...] = (_dot(h, wqk_ref[:, :SB_WIDTH]) * (SB_HEAD_DIM ** -0.5 * LOG2_E)).astype(BF16)
    cq = _rms(c[:, :MLA_Q_LORA], qng_ref[...]).astype(BF16)
    ckv = _rms(ckv, kvng_ref[...]).astype(BF16)

    def rope(v, cos, sin_lo, sin_hi):
        return (v * cos + pltpu.roll(v, LANES - MLA_ROPE_DIM // 2, 1) * sin_lo
                + pltpu.roll(v, MLA_ROPE_DIM // 2, 1) * sin_hi)

    lane = lax.broadcasted_iota(jnp.int32, cos_ref.shape, 1)
    ck_t, sin_t = cos_ref[...], sin_ref[...]
    sk_lo = jnp.where(lane < MLA_NOPE_DIM + MLA_ROPE_DIM // 2, sin_t, 0.0)
    sk_hi = sin_t - sk_lo
    scale = MLA_QK_DIM ** -0.5 * LOG2_E
    cq_t = jnp.where(lane < MLA_NOPE_DIM, scale, ck_t * scale)
    sq_lo, sq_hi = sk_lo * scale, sk_hi * scale

    k_rope = rope(c[:, MLA_Q_LORA:], ck_t, sk_lo, sk_hi)
    q = _dot(cq, wq_ref[...])
    kn = _dot(ckv, wkn_ref[...])
    v_t = lax.dot_general(wv_ref[...], ckv, NT_DIMS, preferred_element_type=F32)
    pad_rows = (lax.broadcasted_iota(jnp.int32, (V_ROWS - MLA_V_DIM, v_t.shape[1]), 0) == 0).astype(BF16)
    for hd in range(MLA_HEADS):
        vt_ref[hd, :MLA_V_DIM, :] = v_t[hd * MLA_V_DIM:(hd + 1) * MLA_V_DIM, :].astype(BF16)
        vt_ref[hd, MLA_V_DIM:, :] = pad_rows
    ksb_ref[...] = _dot(h, wqk_ref[:, SB_WIDTH:]).astype(BF16)
    v_sb_t = lax.dot_general(wvsb_ref[...], h, NT_DIMS, preferred_element_type=F32)
    for half in range(v_sb_t.shape[1] // SB_BLOCK):
        vsbt_ref[half] = v_sb_t[:, half * SB_BLOCK:(half + 1) * SB_BLOCK].astype(BF16)
    for hd in range(MLA_HEADS):
        sl = slice(hd * LANES, (hd + 1) * LANES)
        qf_ref[:, sl] = rope(q[:, sl], cq_t, sq_lo, sq_hi).astype(BF16)
        kf_ref[:, sl] = (kn[:, sl] + k_rope).astype(BF16)


def _proj_call(x2, gin, wts, tabs):
    n, d = x2.shape
    s = tabs[0].shape[0]
    tm = min(TOKEN_TILE, s)
    row = lambda i: (i, 0)
    fix = lambda i: (0, 0)
    w_specs = [pl.BlockSpec(w.shape, fix) for w in wts]
    t_specs = [pl.BlockSpec((tm, LANES), lambda i: (i % (s // tm), 0)) for _ in tabs]
    vsbt_spec = pl.BlockSpec((None, tm // SB_BLOCK, SB_WIDTH, SB_BLOCK), lambda i: (i, 0, 0, 0))
    vsbt_shape = jax.ShapeDtypeStruct((n // tm, tm // SB_BLOCK, SB_WIDTH, SB_BLOCK), BF16)
    vt_spec = pl.BlockSpec((None, MLA_HEADS, V_ROWS, tm), lambda i: (i, 0, 0, 0))
    vt_shape = jax.ShapeDtypeStruct((n // tm, MLA_HEADS, V_ROWS, tm), BF16)
    tile = lambda w: pl.BlockSpec((tm, w), row)
    sds = lambda w: jax.ShapeDtypeStruct((n, w), BF16)
    return pl.pallas_call(
        _proj_kernel,
        grid=(n // tm,),
        in_specs=[pl.BlockSpec((tm, d), row), pl.BlockSpec(gin.shape, fix)] + w_specs + t_specs,
        out_specs=[tile(SB_WIDTH), tile(SB_WIDTH), vsbt_spec, tile(MLA_HEADS * LANES), tile(MLA_HEADS * LANES),
                   vt_spec],
        out_shape=[sds(SB_WIDTH), sds(SB_WIDTH), vsbt_shape, sds(MLA_HEADS * LANES), sds(MLA_HEADS * LANES),
                   vt_shape],
        compiler_params=pltpu.CompilerParams(
            dimension_semantics=("arbitrary",), vmem_limit_bytes=VMEM_LIMIT_BYTES),
        name="proj",
    )(x2, gin, *wts, *tabs)


def _sb_kernel(q_ref, kin_ref, vtin_ref, u_ref, o_ref, k_ref, vt_ref, acc_ref):
    t = SB_BLOCK
    qi = pl.program_id(1)
    k_ref[qi] = kin_ref[...]
    vt_ref[qi] = vtin_ref[...]

    first = lax.broadcasted_iota(jnp.int32, (LANES, t), 0) < SB_HEAD_DIM
    q_heads = []
    for p in range(SB_HEADS // 2):
        q_t = q_ref[:, p * LANES:(p + 1) * LANES].astype(F32).T
        q_heads += [jnp.where(first, q_t, 0.0).astype(BF16), jnp.where(first, 0.0, q_t).astype(BF16)]

    half = t // 2
    key_pos = lax.broadcasted_iota(jnp.int32, (half, half), 0)
    qry_pos = lax.broadcasted_iota(jnp.int32, (half, half), 1)
    before = key_pos < qry_pos

    def visit(blocks, r):
        heads = range(SB_HEADS)
        keep = lambda mask, a: a if mask is None else jnp.where(mask, a, 0.0)

        def layout(diagonal):
            if not diagonal:
                return (lambda a: (a,)), (lambda a: a), (None,), (slice(0, t),)
            split = lambda a: (a[:half, :half], a[:half, half:], a[half:, half:])
            join = lambda tl, tr, br: jnp.concatenate(
                [jnp.concatenate([tl, tr], axis=1), jnp.concatenate([jnp.zeros_like(tl), br], axis=1)], axis=0)
            return split, join, (before, None, before), (slice(0, half), slice(half, t), slice(half, t))

        z, log_beta, later, r_in = [], [], [], [r]

        def scores(blk):
            k_blk = k_ref[blocks[blk][0]]
            z.append([_dot(k_blk[:, (hd // 2) * LANES:(hd // 2 + 1) * LANES], q_heads[hd]) for hd in heads])

        def sums(blk):
            _, diagonal, live = blocks[blk]
            split, join, masks, _ = layout(diagonal)
            log_beta.append([])
            later.append([])
            if live is not None:
                r_in[-1] = tuple(jnp.where(live, x, -jnp.inf) for x in r_in[-1])
            r_next = []
            for hd in heads:
                parts = []
                for z_part in split(z[blk][hd]):
                    neg_log_1m = jnp.maximum(z_part, 0.0) + jnp.log(1.0 + jnp.exp2(-jnp.abs(z_part))) * LOG2_E
                    parts.append((neg_log_1m, z_part - neg_log_1m))
                log_beta[-1].append([lb for _, lb in parts])
                neg_log_1m = join(*[keep(mask, sp) for mask, (sp, _) in zip(masks, parts)]).astype(BF16)
                later[-1].append(_dot(u_ref[...], neg_log_1m))
                r_next.append(r_in[-1][hd] - later[-1][hd][0:1, :] - neg_log_1m[0:1, :].astype(F32))
            r_in.append(tuple(r_next))

        def values(blk):
            j, diagonal, _ = blocks[blk]
            split, join, masks, queries = layout(diagonal)
            for hd in heads:
                w = [keep(mask, jnp.exp2(lb - lt + r_in[blk][hd][:, qs]))
                     for mask, qs, lb, lt in zip(masks, queries, log_beta[blk][hd], split(later[blk][hd]))]
                vt_blk = vt_ref[j, hd * SB_HEAD_DIM:(hd + 1) * SB_HEAD_DIM, :]
                acc_ref[hd] += _dot(vt_blk, join(*w).astype(BF16))

        scores(0)
        sums(0)
        for blk in range(1, len(blocks)):
            scores(blk)
            values(blk - 1)
            sums(blk)
        values(len(blocks) - 1)
        return r_in[-1]

    def r_max(r):
        return functools.reduce(jnp.maximum, [jnp.max(x) for x in r])

    acc_ref[...] = jnp.zeros_like(acc_ref)
    zero = jnp.zeros((1, t), F32)
    r = visit([(qi, True, None), (jnp.maximum(qi - 1, 0), False, qi > 0)], (zero,) * SB_HEADS)

    def cond(c):
        j, rmax, _ = c
        return jnp.logical_and(j >= 0, rmax > -SB_DEAD_LOG2)

    def body(c):
        j, _, r = c
        r = visit([(j, False, None)], r)
        return j - 1, r_max(r), r

    lax.while_loop(cond, body, (qi - 2, r_max(r), r))

    o_ref[...] = jnp.concatenate([acc_ref[hd] for hd in range(SB_HEADS)], axis=0).T


def _sb_call(q, k, vt, u):
    b, s, w = q.shape
    t = SB_BLOCK
    nblk = s // t
    assert vt.shape == (b, nblk, w, t)
    return pl.pallas_call(
        _sb_kernel,
        grid=(b, nblk),
        in_specs=[
            pl.BlockSpec((None, t, w), lambda bi, qi: (bi, qi, 0)),
            pl.BlockSpec((None, t, w), lambda bi, qi: (bi, qi, 0)),
            pl.BlockSpec((None, None, w, t), lambda bi, qi: (bi, qi, 0, 0)),
            pl.BlockSpec(u.shape, lambda bi, qi: (0, 0)),
        ],
        out_specs=pl.BlockSpec((None, t, w), lambda bi, qi: (bi, qi, 0)),
        out_shape=jax.ShapeDtypeStruct((b, s, w), F32),
        scratch_shapes=[
            pltpu.VMEM((nblk, t, w), BF16),
            pltpu.VMEM((nblk, w, t), BF16),
            pltpu.VMEM((SB_HEADS, SB_HEAD_DIM, t), F32),
        ],
        compiler_params=pltpu.CompilerParams(
            dimension_semantics=("arbitrary", "arbitrary"), vmem_limit_bytes=VMEM_LIMIT_BYTES),
        name="sb_attn",
    )(q, k, vt, u)


def _mla_kernel(q_ref, kin_ref, vtin_ref, o_ref, k_ref, vt_ref, acc_ref, m_ref, s0_ref, s1_ref, smax0_ref,
                smax1_ref):
    t = ATT_BLOCK
    qi = pl.program_id(1)
    k_ref[qi] = kin_ref[...]
    vt_ref[qi] = vtin_ref[...]

    q_heads = [q_ref[:, hd * LANES:(hd + 1) * LANES].astype(F32).T.astype(BF16) for hd in range(HEADS_PER_STEP)]

    halves = (slice(0, t // 2), slice(t // 2, t))
    key_chunk = lax.broadcasted_iota(jnp.int32, (t // 2, t // 2), 0) // CHUNK
    qry_chunk = lax.broadcasted_iota(jnp.int32, (t // 2, t // 2), 1) // CHUNK
    within = key_chunk <= qry_chunk

    def score_rows(hd, j, buf, rows):
        s = _dot(k_ref[j, rows, hd * LANES:(hd + 1) * LANES], q_heads[hd])
        buf[0][hd, rows, :] = s
        return jnp.max(s, axis=0, keepdims=True)

    def diagonal_scores(hd, buf):
        lo, hi = halves
        q_t = q_heads[hd]
        s_lo = _dot(k_ref[qi, lo, hd * LANES:(hd + 1) * LANES], q_t)
        s_hi = _dot(k_ref[qi, hi, hd * LANES:(hd + 1) * LANES], q_t[:, hi])
        s_lo_lo = jnp.where(within, s_lo[:, lo], -jnp.inf)
        s_hi_hi = jnp.where(within, s_hi, -jnp.inf)
        buf[0][hd, lo, lo] = s_lo_lo
        buf[0][hd, lo, hi] = s_lo[:, hi]
        buf[0][hd, hi, lo] = jnp.full_like(s_hi, -jnp.inf)
        buf[0][hd, hi, hi] = s_hi_hi
        col_max = lambda a: jnp.max(a, axis=0, keepdims=True)
        buf[1][hd] = jnp.concatenate(
            [col_max(s_lo_lo), jnp.maximum(col_max(s_lo[:, hi]), col_max(s_hi_hi))], axis=1)

    def rescale(hd, buf):
        m = m_ref[hd]
        m_new = jnp.maximum(m, buf[1][hd])
        m_ref[hd] = m_new
        acc_ref[hd] = jnp.exp2(m - m_new) * acc_ref[hd]
        return m_new

    def accumulate(hd, j, buf, rows, m_new):
        p = jnp.exp2(buf[0][hd, rows, :] - m_new).astype(BF16)
        acc_ref[hd] += _dot(vt_ref[j, hd, :, rows], p)

    def update(hd, j, buf):
        m_new = rescale(hd, buf)
        for rows in halves:
            accumulate(hd, j, buf, rows, m_new)

    def step(hd, j_next, buf_next, j_cur, buf_cur):
        m_new = rescale(hd, buf_cur)
        maxes = []
        for rows in halves:
            maxes.append(score_rows(hd, j_next, buf_next, rows))
            accumulate(hd, j_cur, buf_cur, rows, m_new)
        buf_next[1][hd] = jnp.maximum(*maxes)

    acc_ref[...] = jnp.zeros_like(acc_ref)
    m_ref[...] = jnp.full_like(m_ref, -jnp.inf)
    bufs = ((s0_ref, smax0_ref), (s1_ref, smax1_ref))

    block_of = lambda k: jnp.where(k == 0, qi, k - 1)
    for hd in range(HEADS_PER_STEP):
        diagonal_scores(hd, bufs[0])

    def consume(k, parity):
        for hd in range(HEADS_PER_STEP):
            step(hd, block_of(k + 1), bufs[1 - parity], block_of(k), bufs[parity])

    def body(pair, carry):
        consume(2 * pair, 0)
        consume(2 * pair + 1, 1)
        return carry

    lax.fori_loop(0, qi // 2, body, 0)

    @pl.when(qi % 2 == 1)
    def _():
        consume(qi - 1, 0)

    for parity in range(2):
        @pl.when(qi % 2 == parity)
        def _():
            for hd in range(HEADS_PER_STEP):
                update(hd, block_of(qi), bufs[parity])

    o_ref[...] = jnp.concatenate(
        [acc_ref[hd, :MLA_V_DIM, :] / acc_ref[hd, MLA_V_DIM:MLA_V_DIM + 1, :] for hd in range(HEADS_PER_STEP)],
        axis=0).T


def _mla_call(qf, kf, vt):
    b, s, w = qf.shape
    t = ATT_BLOCK
    nblk = s // t
    assert vt.shape == (b, nblk, MLA_HEADS, V_ROWS, t) and HEADS_PER_STEP == MLA_HEADS
    return pl.pallas_call(
        _mla_kernel,
        grid=(b, nblk),
        in_specs=[
            pl.BlockSpec((None, t, w), lambda bi, qi: (bi, qi, 0)),
            pl.BlockSpec((None, t, w), lambda bi, qi: (bi, qi, 0)),
            pl.BlockSpec((None, None, MLA_HEADS, V_ROWS, t), lambda bi, qi: (bi, qi, 0, 0, 0)),
        ],
        out_specs=pl.BlockSpec((None, t, MLA_WIDTH), lambda bi, qi: (bi, qi, 0)),
        out_shape=jax.ShapeDtypeStruct((b, s, MLA_WIDTH), F32),
        scratch_shapes=[
            pltpu.VMEM((nblk, t, w), BF16),
            pltpu.VMEM((nblk, MLA_HEADS, V_ROWS, t), BF16),
            pltpu.VMEM((HEADS_PER_STEP, V_ROWS, t), F32),
            pltpu.VMEM((HEADS_PER_STEP, 1, t), F32),
            pltpu.VMEM((HEADS_PER_STEP, t, t), F32),
            pltpu.VMEM((HEADS_PER_STEP, t, t), F32),
            pltpu.VMEM((HEADS_PER_STEP, 1, t), F32),
            pltpu.VMEM((HEADS_PER_STEP, 1, t), F32),
        ],
        compiler_params=pltpu.CompilerParams(
            dimension_semantics=("arbitrary", "arbitrary"), vmem_limit_bytes=VMEM_LIMIT_BYTES),
        name="mla_attn",
    )(qf, kf, vt)


def _out_kernel(x_ref, osb_ref, omla_ref, gin_ref, wgsb_ref, wg_ref, bg_ref, wosb_ref, womla_ref, wout_ref,
                gf_ref, o_ref, *, final_norm):
    x = x_ref[...]
    d = x.shape[-1]
    h = _rms(x, gin_ref[...]).astype(BF16)
    a_sb = (osb_ref[...] * jax.nn.silu(_dot(h, wgsb_ref[...]))).astype(BF16)
    a_mla = (omla_ref[...] * jax.nn.silu(_dot(h, wg_ref[:, :MLA_WIDTH]))).astype(BF16)
    g0 = MLA_WIDTH
    g_sb = jax.nn.sigmoid(_dot(h, wg_ref[:, g0:g0 + d]) + bg_ref[:, :d])
    merged = g_sb * _dot(a_sb, wosb_ref[...])
    g_mla = jax.nn.sigmoid(_dot(h, wg_ref[:, g0 + d:g0 + 2 * d]) + bg_ref[:, d:])
    merged = merged + g_mla * _dot(a_mla, womla_ref[...])
    y = x + _dot(merged.astype(BF16), wout_ref[...])
    o_ref[...] = _rms(y, gf_ref[...]) if final_norm else y


def _out_call(x2, osb, omla, gin, wgsb, wg, bg, wosb, womla, wout, gf, final_norm):
    n, d = x2.shape
    tm = min(OUT_TILE, n)
    row = lambda i: (i, 0)
    fix = lambda i: (0, 0)
    consts = (gin, wgsb, wg, bg, wosb, womla, wout, gf)
    return pl.pallas_call(
        functools.partial(_out_kernel, final_norm=final_norm),
        grid=(n // tm,),
        in_specs=[pl.BlockSpec((tm, d), row), pl.BlockSpec((tm, SB_WIDTH), row), pl.BlockSpec((tm, MLA_WIDTH), row)]
                 + [pl.BlockSpec(c.shape, fix, pipeline_mode=pl.Buffered(1)) for c in consts],
        out_specs=pl.BlockSpec((tm, d), row),
        out_shape=jax.ShapeDtypeStruct((n, d), F32),
        compiler_params=pltpu.CompilerParams(
            dimension_semantics=("arbitrary",), vmem_limit_bytes=VMEM_LIMIT_BYTES),
        name="out",
    )(x2, osb, omla, *consts)


def _rope_tables(s):
    half = MLA_ROPE_DIM // 2
    inv_freq = ROPE_THETA ** (-np.arange(half, dtype=np.float64) / half)
    ang = np.arange(s, dtype=np.float64)[:, None] * inv_freq[None, :]
    cos, sin = np.cos(ang), np.sin(ang)
    lanes = lambda lo, hi: np.concatenate(
        [np.zeros((s, MLA_NOPE_DIM)), lo, hi, np.zeros((s, LANES - MLA_QK_DIM))], axis=1).astype(np.float32)
    return jnp.asarray(lanes(cos, cos)), jnp.asarray(lanes(-sin, sin))


def _up_weights(w_q_up, w_kv_up):
    rq = w_q_up.shape[0]
    wq = w_q_up.astype(BF16).reshape(rq, MLA_HEADS, MLA_QK_DIM)
    wq = jnp.concatenate([wq, jnp.zeros((rq, MLA_HEADS, LANES - MLA_QK_DIM), BF16)], axis=-1)
    rkv = w_kv_up.shape[0]
    wkv = w_kv_up.astype(BF16).reshape(rkv, MLA_HEADS, MLA_NOPE_DIM + MLA_V_DIM)
    wkn = jnp.concatenate([wkv[..., :MLA_NOPE_DIM], jnp.zeros((rkv, MLA_HEADS, LANES - MLA_NOPE_DIM), BF16)], axis=-1)
    wv_t = wkv[..., MLA_NOPE_DIM:].reshape(rkv, MLA_WIDTH).T
    return wq.reshape(rq, MLA_HEADS * LANES), wkn.reshape(rkv, MLA_HEADS * LANES), wv_t


def kernel(x, norm_in_g, w_in, b_gate, q_norm_g, w_q_up, kv_norm_g, w_kv_up, w_o_sb, w_o_mla, w_out, norm_f_g):
    b, s, d = x.shape
    depth = w_in.shape[0]
    assert s % ATT_BLOCK == 0 and s % SB_BLOCK == 0 and TOKEN_TILE == ATT_BLOCK
    tabs = _rope_tables(s)
    idx = jnp.arange(SB_BLOCK)
    after_mat = (idx[None, :] > idx[:, None]).astype(BF16)

    w_in_t = jnp.swapaxes(w_in, 1, 2)
    x2 = x.reshape(b * s, d)
    for l in range(depth):
        wqk, wvsb_t, wgsb, wc, wckv, wg = _wprep_call(w_in_t, l)
        wq, wkn, wv_t = _up_weights(w_q_up[l], w_kv_up[l])
        gin = norm_in_g[l][None, :]
        qsb, ksb, vsbt, qf, kf, vt = _proj_call(
            x2, gin,
            (wqk, wvsb_t, wc, wckv, q_norm_g[l][None, :], kv_norm_g[l][None, :], wq, wkn, wv_t),
            tabs)
        r3 = lambda a: a.reshape(b, s, a.shape[-1])
        vsbt = vsbt.reshape(b, s // SB_BLOCK, SB_WIDTH, SB_BLOCK)
        vt = vt.reshape((b, s // ATT_BLOCK) + vt.shape[1:])
        o_sb = _sb_call(r3(qsb), r3(ksb), vsbt, after_mat).reshape(b * s, SB_WIDTH)
        o_mla = _mla_call(r3(qf), r3(kf), vt).reshape(b * s, MLA_WIDTH)
        x2 = _out_call(x2, o_sb, o_mla, gin, wgsb, wg, b_gate[l][None, :], w_o_sb[l].astype(BF16),
                       w_o_mla[l].astype(BF16), w_out[l].astype(BF16), norm_f_g[None, :],
                       final_norm=(l == depth - 1))
    return x2.reshape(b, s, d)
```

```python
import functools

import jax
import jax.numpy as jnp
import numpy as np
from jax import lax
from jax.experimental import pallas as pl
from jax.experimental.pallas import tpu as pltpu

EPS = 1e-6
CHUNK = 64

SB_HEADS = 8
SB_HEAD_DIM = 64
SB_WIDTH = SB_HEADS * SB_HEAD_DIM

MLA_HEADS = 8
MLA_NOPE_DIM = 64
MLA_ROPE_DIM = 32
MLA_V_DIM = 64
MLA_Q_LORA = 384
MLA_KV_LORA = 256
MLA_QK_DIM = MLA_NOPE_DIM + MLA_ROPE_DIM
MLA_WIDTH = MLA_HEADS * MLA_V_DIM
ROPE_THETA = 10000.0

LANES = 128
ATT_BLOCK = 512
SB_BLOCK = 256
SB_DEAD_LOG2 = 160.0
LOG2_E = float(np.log2(np.e))
HEADS_PER_STEP = 8
FINISH_LAG = 2
V_ROWS = 80
TOKEN_TILE = 512
OUT_TILE = 1024
WPREP_ROWS = 256
VMEM_LIMIT_BYTES = 56 * 1024 * 1024

F32 = jnp.float32
BF16 = jnp.bfloat16
NT_DIMS = (((1,), (1,)), ((), ()))


def _dot(a, b):
    return jnp.dot(a, b, preferred_element_type=F32)


def _rms(x, g):
    inv = lax.rsqrt(jnp.mean(x * x, axis=-1, keepdims=True) + EPS)
    return (x * inv) * g


def _wprep_kernel(wt_ref, wqk_ref, wvsbt_ref, wgsb_ref, wc_ref, wckv_ref, wg_ref):
    o_v, o_gsb, o_cq = 2 * SB_WIDTH, 3 * SB_WIDTH, 4 * SB_WIDTH
    o_ckv = o_cq + MLA_Q_LORA
    o_kr = o_ckv + MLA_KV_LORA
    o_gm = o_kr + MLA_ROPE_DIM
    rows = wt_ref.shape[1]
    piece = lambda lo, hi: wt_ref[lo:hi, :]
    wqk_ref[...] = piece(0, o_v).T.astype(BF16)
    wvsbt_ref[...] = piece(o_v, o_gsb).astype(BF16)
    wgsb_ref[...] = piece(o_gsb, o_cq).T.astype(BF16)
    wc_t = jnp.concatenate([piece(o_cq, o_ckv), jnp.zeros((MLA_NOPE_DIM, rows), F32), piece(o_kr, o_gm),
                            jnp.zeros((LANES - MLA_QK_DIM, rows), F32)], axis=0)
    wc_ref[...] = wc_t.T.astype(BF16)
    wckv_ref[...] = piece(o_ckv, o_kr).T.astype(BF16)
    wg_ref[...] = piece(o_gm, wt_ref.shape[0]).T.astype(BF16)


def _wprep_call(w_in_t, layer):
    _, cols, d = w_in_t.shape
    rows = WPREP_ROWS
    n_g = cols - (4 * SB_WIDTH + MLA_Q_LORA + MLA_KV_LORA + MLA_ROPE_DIM)
    row = lambda i: (i, 0)
    widths = (2 * SB_WIDTH, None, SB_WIDTH, MLA_Q_LORA + LANES, MLA_KV_LORA, n_g)
    out_specs = [pl.BlockSpec((SB_WIDTH, rows), lambda i: (0, i)) if wd is None else pl.BlockSpec((rows, wd), row)
                 for wd in widths]
    out_shape = [jax.ShapeDtypeStruct((SB_WIDTH, d) if wd is None else (d, wd), BF16) for wd in widths]
    return pl.pallas_call(
        _wprep_kernel,
        grid=(d // rows,),
        in_specs=[pl.BlockSpec((None, cols, rows), lambda i: (layer, 0, i))],
        out_specs=out_specs,
        out_shape=out_shape,
        compiler_params=pltpu.CompilerParams(
            dimension_semantics=("arbitrary",), vmem_limit_bytes=VMEM_LIMIT_BYTES),
        name="wprep",
    )(w_in_t)


def _proj_kernel(x_ref, gin_ref, wqk_ref, wvsb_ref, wc_ref, wckv_ref, qng_ref, kvng_ref, wq_ref, wkn_ref, wv_ref,
                 cos_ref, sin_ref, qsb_ref, ksb_ref, vsbt_ref, qf_ref, kf_ref, vt_ref):
    h = _rms(x_ref[...], gin_ref[...]).astype(BF16)

    c = _dot(h, wc_ref[...])
    ckv = _dot(h, wckv_ref[...])
    qsb_ref[...] = (_dot(h, wqk_ref[:, :SB_WIDTH]) * (SB_HEAD_DIM ** -0.5 * LOG2_E)).astype(BF16)
    cq = _rms(c[:, :MLA_Q_LORA], qng_ref[...]).astype(BF16)
    ckv = _rms(ckv, kvng_ref[...]).astype(BF16)

    def rope(v, cos, sin_lo, sin_hi):
        return (v * cos + pltpu.roll(v, LANES - MLA_ROPE_DIM // 2, 1) * sin_lo
                + pltpu.roll(v, MLA_ROPE_DIM // 2, 1) * sin_hi)

    lane = lax.broadcasted_iota(jnp.int32, cos_ref.shape, 1)
    ck_t, sin_t = cos_ref[...], sin_ref[...]
    sk_lo = jnp.where(lane < MLA_NOPE_DIM + MLA_ROPE_DIM // 2, sin_t, 0.0)
    sk_hi = sin_t - sk_lo
    scale = MLA_QK_DIM ** -0.5 * LOG2_E
    cq_t = jnp.where(lane < MLA_NOPE_DIM, scale, ck_t * scale)
    sq_lo, sq_hi = sk_lo * scale, sk_hi * scale

    k_rope = rope(c[:, MLA_Q_LORA:], ck_t, sk_lo, sk_hi)
    q = _dot(cq, wq_ref[...])
    kn = _dot(ckv, wkn_ref[...])
    v_t = lax.dot_general(wv_ref[...], ckv, NT_DIMS, preferred_element_type=F32)
    pad_rows = (lax.broadcasted_iota(jnp.int32, (V_ROWS - MLA_V_DIM, v_t.shape[1]), 0) == 0).astype(BF16)
    for hd in range(MLA_HEADS):
        vt_ref[hd, :MLA_V_DIM, :] = v_t[hd * MLA_V_DIM:(hd + 1) * MLA_V_DIM, :].astype(BF16)
        vt_ref[hd, MLA_V_DIM:, :] = pad_rows
    ksb_ref[...] = _dot(h, wqk_ref[:, SB_WIDTH:]).astype(BF16)
    v_sb_t = lax.dot_general(wvsb_ref[...], h, NT_DIMS, preferred_element_type=F32)
    for half in range(v_sb_t.shape[1] // SB_BLOCK):
        vsbt_ref[half] = v_sb_t[:, half * SB_BLOCK:(half + 1) * SB_BLOCK].astype(BF16)
    for hd in range(MLA_HEADS):
        sl = slice(hd * LANES, (hd + 1) * LANES)
        qf_ref[:, sl] = rope(q[:, sl], cq_t, sq_lo, sq_hi).astype(BF16)
        kf_ref[:, sl] = (kn[:, sl] + k_rope).astype(BF16)


def _proj_call(x2, gin, wts, tabs):
    n, d = x2.shape
    s = tabs[0].shape[0]
    tm = min(TOKEN_TILE, s)
    row = lambda i: (i, 0)
    fix = lambda i: (0, 0)
    w_specs = [pl.BlockSpec(w.shape, fix) for w in wts]
    t_specs = [pl.BlockSpec((tm, LANES), lambda i: (i % (s // tm), 0)) for _ in tabs]
    vsbt_spec = pl.BlockSpec((None, tm // SB_BLOCK, SB_WIDTH, SB_BLOCK), lambda i: (i, 0, 0, 0))
    vsbt_shape = jax.ShapeDtypeStruct((n // tm, tm // SB_BLOCK, SB_WIDTH, SB_BLOCK), BF16)
    vt_spec = pl.BlockSpec((None, MLA_HEADS, V_ROWS, tm), lambda i: (i, 0, 0, 0))
    vt_shape = jax.ShapeDtypeStruct((n // tm, MLA_HEADS, V_ROWS, tm), BF16)
    tile = lambda w: pl.BlockSpec((tm, w), row)
    sds = lambda w: jax.ShapeDtypeStruct((n, w), BF16)
    return pl.pallas_call(
        _proj_kernel,
        grid=(n // tm,),
        in_specs=[pl.BlockSpec((tm, d), row), pl.BlockSpec(gin.shape, fix)] + w_specs + t_specs,
        out_specs=[tile(SB_WIDTH), tile(SB_WIDTH), vsbt_spec, tile(MLA_HEADS * LANES), tile(MLA_HEADS * LANES),
                   vt_spec],
        out_shape=[sds(SB_WIDTH), sds(SB_WIDTH), vsbt_shape, sds(MLA_HEADS * LANES), sds(MLA_HEADS * LANES),
                   vt_shape],
        compiler_params=pltpu.CompilerParams(
            dimension_semantics=("arbitrary",), vmem_limit_bytes=VMEM_LIMIT_BYTES),
        name="proj",
    )(x2, gin, *wts, *tabs)


def _sb_kernel(q_ref, kin_ref, vtin_ref, u_ref, o_ref, k_ref, vt_ref, acc_ref):
    t = SB_BLOCK
    qi = pl.program_id(1)
    k_ref[qi] = kin_ref[...]
    vt_ref[qi] = vtin_ref[...]

    first = lax.broadcasted_iota(jnp.int32, (LANES, t), 0) < SB_HEAD_DIM
    q_heads = []
    for p in range(SB_HEADS // 2):
        q_t = q_ref[:, p * LANES:(p + 1) * LANES].astype(F32).T
        q_heads += [jnp.where(first, q_t, 0.0).astype(BF16), jnp.where(first, 0.0, q_t).astype(BF16)]

    half = t // 2
    key_pos = lax.broadcasted_iota(jnp.int32, (half, half), 0)
    qry_pos = lax.broadcasted_iota(jnp.int32, (half, half), 1)
    before = key_pos < qry_pos

    def visit(blocks, r):
        heads = range(SB_HEADS)
        keep = lambda mask, a: a if mask is None else jnp.where(mask, a, 0.0)

        def layout(diagonal):
            if not diagonal:
                return (lambda a: (a,)), (lambda a: a), (None,), (slice(0, t),)
            split = lambda a: (a[:half, :half], a[:half, half:], a[half:, half:])
            join = lambda tl, tr, br: jnp.concatenate(
                [jnp.concatenate([tl, tr], axis=1), jnp.concatenate([jnp.zeros_like(tl), br], axis=1)], axis=0)
            return split, join, (before, None, before), (slice(0, half), slice(half, t), slice(half, t))

        z, log_beta, later, r_in = [], [], [], [r]

        def scores(blk):
            k_blk = k_ref[blocks[blk][0]]
            z.append([_dot(k_blk[:, (hd // 2) * LANES:(hd // 2 + 1) * LANES], q_heads[hd]) for hd in heads])

        def sums(blk):
            _, diagonal, live = blocks[blk]
            split, join, masks, _ = layout(diagonal)
            log_beta.append([])
            later.append([])
            if live is not None:
                r_in[-1] = tuple(jnp.where(live, x, -jnp.inf) for x in r_in[-1])
            r_next = []
            for hd in heads:
                parts = []
                for z_part in split(z[blk][hd]):
                    neg_log_1m = jnp.maximum(z_part, 0.0) + jnp.log(1.0 + jnp.exp2(-jnp.abs(z_part))) * LOG2_E
                    parts.append((neg_log_1m, z_part - neg_log_1m))
                log_beta[-1].append([lb for _, lb in parts])
                neg_log_1m = join(*[keep(mask, sp) for mask, (sp, _) in zip(masks, parts)]).astype(BF16)
                later[-1].append(_dot(u_ref[...], neg_log_1m))
                r_next.append(r_in[-1][hd] - later[-1][hd][0:1, :] - neg_log_1m[0:1, :].astype(F32))
            r_in.append(tuple(r_next))

        def values(blk):
            j, diagonal, _ = blocks[blk]
            split, join, masks, queries = layout(diagonal)
            for hd in heads:
                w = [keep(mask, jnp.exp2(lb - lt + r_in[blk][hd][:, qs]))
                     for mask, qs, lb, lt in zip(masks, queries, log_beta[blk][hd], split(later[blk][hd]))]
                vt_blk = vt_ref[j, hd * SB_HEAD_DIM:(hd + 1) * SB_HEAD_DIM, :]
                acc_ref[hd] += _dot(vt_blk, join(*w).astype(BF16))

        scores(0)
        sums(0)
        for blk in range(1, len(blocks)):
            scores(blk)
            values(blk - 1)
            sums(blk)
        values(len(blocks) - 1)
        return r_in[-1]

    def r_max(r):
        return functools.reduce(jnp.maximum, [jnp.max(x) for x in r])

    acc_ref[...] = jnp.zeros_like(acc_ref)
    zero = jnp.zeros((1, t), F32)
    r = visit([(qi, True, None), (jnp.maximum(qi - 1, 0), False, qi > 0)], (zero,) * SB_HEADS)

    def cond(c):
        j, rmax, _ = c
        return jnp.logical_and(j >= 0, rmax > -SB_DEAD_LOG2)

    def body(c):
        j, _, r = c
        r = visit([(j, False, None)], r)
        return j - 1, r_max(r), r

    lax.while_loop(cond, body, (qi - 2, r_max(r), r))

    o_ref[...] = jnp.concatenate([acc_ref[hd] for hd in range(SB_HEADS)], axis=0).T


def _sb_call(q, k, vt, u):
    b, s, w = q.shape
    t = SB_BLOCK
    nblk = s // t
    assert vt.shape == (b, nblk, w, t)
    return pl.pallas_call(
        _sb_kernel,
        grid=(b, nblk),
        in_specs=[
            pl.BlockSpec((None, t, w), lambda bi, qi: (bi, qi, 0)),
            pl.BlockSpec((None, t, w), lambda bi, qi: (bi, qi, 0)),
            pl.BlockSpec((None, None, w, t), lambda bi, qi: (bi, qi, 0, 0)),
            pl.BlockSpec(u.shape, lambda bi, qi: (0, 0)),
        ],
        out_specs=pl.BlockSpec((None, t, w), lambda bi, qi: (bi, qi, 0)),
        out_shape=jax.ShapeDtypeStruct((b, s, w), F32),
        scratch_shapes=[
            pltpu.VMEM((nblk, t, w), BF16),
            pltpu.VMEM((nblk, w, t), BF16),
            pltpu.VMEM((SB_HEADS, SB_HEAD_DIM, t), F32),
        ],
        compiler_params=pltpu.CompilerParams(
            dimension_semantics=("arbitrary", "arbitrary"), vmem_limit_bytes=VMEM_LIMIT_BYTES),
        name="sb_attn",
    )(q, k, vt, u)


def _mla_kernel(q_ref, kin_ref, vtin_ref, o_ref, k_ref, vt_ref, acc_ref, m_ref, s0_ref, s1_ref, smax0_ref,
                smax1_ref):
    t = ATT_BLOCK
    qi = pl.program_id(1)
    k_ref[qi] = kin_ref[...]
    vt_ref[qi] = vtin_ref[...]

    q_heads = [q_ref[:, hd * LANES:(hd + 1) * LANES].astype(F32).T.astype(BF16) for hd in range(HEADS_PER_STEP)]

    halves = (slice(0, t // 2), slice(t // 2, t))
    key_chunk = lax.broadcasted_iota(jnp.int32, (t // 2, t // 2), 0) // CHUNK
    qry_chunk = lax.broadcasted_iota(jnp.int32, (t // 2, t // 2), 1) // CHUNK
    within = key_chunk <= qry_chunk

    def score_rows(hd, j, buf, rows):
        s = _dot(k_ref[j, rows, hd * LANES:(hd + 1) * LANES], q_heads[hd])
        buf[0][hd, rows, :] = s
        return jnp.max(s, axis=0, keepdims=True)

    def diagonal_scores(hd, buf):
        lo, hi = halves
        q_t = q_heads[hd]
        s_lo = _dot(k_ref[qi, lo, hd * LANES:(hd + 1) * LANES], q_t)
        s_hi = _dot(k_ref[qi, hi, hd * LANES:(hd + 1) * LANES], q_t[:, hi])
        s_lo_lo = jnp.where(within, s_lo[:, lo], -jnp.inf)
        s_hi_hi = jnp.where(within, s_hi, -jnp.inf)
        buf[0][hd, lo, lo] = s_lo_lo
        buf[0][hd, lo, hi] = s_lo[:, hi]
        buf[0][hd, hi, lo] = jnp.full_like(s_hi, -jnp.inf)
        buf[0][hd, hi, hi] = s_hi_hi
        col_max = lambda a: jnp.max(a, axis=0, keepdims=True)
        buf[1][hd] = jnp.concatenate(
            [col_max(s_lo_lo), jnp.maximum(col_max(s_lo[:, hi]), col_max(s_hi_hi))], axis=1)

    def rescale(hd, buf):
        m = m_ref[hd]
        m_new = jnp.maximum(m, buf[1][hd])
        m_ref[hd] = m_new
        acc_ref[hd] = jnp.exp2(m - m_new) * acc_ref[hd]
        return m_new

    def accumulate(hd, j, buf, rows, m_new):
        p = jnp.exp2(buf[0][hd, rows, :] - m_new).astype(BF16)
        acc_ref[hd] += _dot(vt_ref[j, hd, :, rows], p)

    def update(hd, j, buf):
        m_new = rescale(hd, buf)
        for rows in halves:
            accumulate(hd, j, buf, rows, m_new)

    def step(hd, j_next, buf_next, j_cur, buf_cur):
        m_new = rescale(hd, buf_cur)
        maxes = []
        for rows in halves:
            maxes.append(score_rows(hd, j_next, buf_next, rows))
            accumulate(hd, j_cur, buf_cur, rows, m_new)
        buf_next[1][hd] = jnp.maximum(*maxes)

    acc_ref[...] = jnp.zeros_like(acc_ref)
    m_ref[...] = jnp.full_like(m_ref, -jnp.inf)
    bufs = ((s0_ref, smax0_ref), (s1_ref, smax1_ref))

    block_of = lambda k: jnp.where(k == 0, qi, k - 1)
    for hd in range(HEADS_PER_STEP):
        diagonal_scores(hd, bufs[0])

    def consume(k, parity):
        for hd in range(HEADS_PER_STEP):
            step(hd, block_of(k + 1), bufs[1 - parity], block_of(k), bufs[parity])

    def body(pair, carry):
        consume(2 * pair, 0)
        consume(2 * pair + 1, 1)
        return carry

    def finish(parity):
        for hd in range(HEADS_PER_STEP):
            update(hd, block_of(qi), bufs[parity])

    odd = qi % 2
    lax.fori_loop(0, jnp.maximum(qi // 2 + odd - 1, 0), body, 0)

    def consume_and_finish(parity):
        for hd in range(HEADS_PER_STEP + FINISH_LAG):
            if hd < HEADS_PER_STEP:
                step(hd, block_of(qi), bufs[1 - parity], block_of(qi - 1), bufs[parity])
            if hd >= FINISH_LAG:
                update(hd - FINISH_LAG, block_of(qi), bufs[1 - parity])

    @pl.when(odd == 1)
    def _():
        consume_and_finish(0)

    @pl.when(jnp.logical_and(odd == 0, qi > 0))
    def _():
        consume(qi - 2, 0)
        consume_and_finish(1)

    @pl.when(qi == 0)
    def _():
        finish(0)

    o_ref[...] = jnp.concatenate(
        [acc_ref[hd, :MLA_V_DIM, :] / acc_ref[hd, MLA_V_DIM:MLA_V_DIM + 1, :] for hd in range(HEADS_PER_STEP)],
        axis=0).T


def _mla_call(qf, kf, vt):
    b, s, w = qf.shape
    t = ATT_BLOCK
    nblk = s // t
    assert vt.shape == (b, nblk, MLA_HEADS, V_ROWS, t) and HEADS_PER_STEP == MLA_HEADS
    return pl.pallas_call(
        _mla_kernel,
        grid=(b, nblk),
        in_specs=[
            pl.BlockSpec((None, t, w), lambda bi, qi: (bi, qi, 0)),
            pl.BlockSpec((None, t, w), lambda bi, qi: (bi, qi, 0)),
            pl.BlockSpec((None, None, MLA_HEADS, V_ROWS, t), lambda bi, qi: (bi, qi, 0, 0, 0)),
        ],
        out_specs=pl.BlockSpec((None, t, MLA_WIDTH), lambda bi, qi: (bi, qi, 0)),
        out_shape=jax.ShapeDtypeStruct((b, s, MLA_WIDTH), F32),
        scratch_shapes=[
            pltpu.VMEM((nblk, t, w), BF16),
            pltpu.VMEM((nblk, MLA_HEADS, V_ROWS, t), BF16),
            pltpu.VMEM((HEADS_PER_STEP, V_ROWS, t), F32),
            pltpu.VMEM((HEADS_PER_STEP, 1, t), F32),
            pltpu.VMEM((HEADS_PER_STEP, t, t), F32),
            pltpu.VMEM((HEADS_PER_STEP, t, t), F32),
            pltpu.VMEM((HEADS_PER_STEP, 1, t), F32),
            pltpu.VMEM((HEADS_PER_STEP, 1, t), F32),
        ],
        compiler_params=pltpu.CompilerParams(
            dimension_semantics=("arbitrary", "arbitrary"), vmem_limit_bytes=VMEM_LIMIT_BYTES),
        name="mla_attn",
    )(qf, kf, vt)


def _out_kernel(x_ref, osb_ref, omla_ref, gin_ref, wgsb_ref, wg_ref, bg_ref, wosb_ref, womla_ref, wout_ref,
                gf_ref, o_ref, *, final_norm):
    x = x_ref[...]
    d = x.shape[-1]
    h = _rms(x, gin_ref[...]).astype(BF16)
    a_sb = (osb_ref[...] * jax.nn.silu(_dot(h, wgsb_ref[...]))).astype(BF16)
    a_mla = (omla_ref[...] * jax.nn.silu(_dot(h, wg_ref[:, :MLA_WIDTH]))).astype(BF16)
    g0 = MLA_WIDTH
    g_sb = jax.nn.sigmoid(_dot(h, wg_ref[:, g0:g0 + d]) + bg_ref[:, :d])
    merged = g_sb * _dot(a_sb, wosb_ref[...])
    g_mla = jax.nn.sigmoid(_dot(h, wg_ref[:, g0 + d:g0 + 2 * d]) + bg_ref[:, d:])
    merged = merged + g_mla * _dot(a_mla, womla_ref[...])
    y = x + _dot(merged.astype(BF16), wout_ref[...])
    o_ref[...] = _rms(y, gf_ref[...]) if final_norm else y


def _out_call(x2, osb, omla, gin, wgsb, wg, bg, wosb, womla, wout, gf, final_norm):
    n, d = x2.shape
    tm = min(OUT_TILE, n)
    row = lambda i: (i, 0)
    fix = lambda i: (0, 0)
    consts = (gin, wgsb, wg, bg, wosb, womla, wout, gf)
    return pl.pallas_call(
        functools.partial(_out_kernel, final_norm=final_norm),
        grid=(n // tm,),
        in_specs=[pl.BlockSpec((tm, d), row), pl.BlockSpec((tm, SB_WIDTH), row), pl.BlockSpec((tm, MLA_WIDTH), row)]
                 + [pl.BlockSpec(c.shape, fix, pipeline_mode=pl.Buffered(1)) for c in consts],
        out_specs=pl.BlockSpec((tm, d), row),
        out_shape=jax.ShapeDtypeStruct((n, d), F32),
        compiler_params=pltpu.CompilerParams(
            dimension_semantics=("arbitrary",), vmem_limit_bytes=VMEM_LIMIT_BYTES),
        name="out",
    )(x2, osb, omla, *consts)


def _rope_tables(s):
    half = MLA_ROPE_DIM // 2
    inv_freq = ROPE_THETA ** (-np.arange(half, dtype=np.float64) / half)
    ang = np.arange(s, dtype=np.float64)[:, None] * inv_freq[None, :]
    cos, sin = np.cos(ang), np.sin(ang)
    lanes = lambda lo, hi: np.concatenate(
        [np.zeros((s, MLA_NOPE_DIM)), lo, hi, np.zeros((s, LANES - MLA_QK_DIM))], axis=1).astype(np.float32)
    return jnp.asarray(lanes(cos, cos)), jnp.asarray(lanes(-sin, sin))


def _up_weights(w_q_up, w_kv_up):
    rq = w_q_up.shape[0]
    wq = w_q_up.astype(BF16).reshape(rq, MLA_HEADS, MLA_QK_DIM)
    wq = jnp.concatenate([wq, jnp.zeros((rq, MLA_HEADS, LANES - MLA_QK_DIM), BF16)], axis=-1)
    rkv = w_kv_up.shape[0]
    wkv = w_kv_up.astype(BF16).reshape(rkv, MLA_HEADS, MLA_NOPE_DIM + MLA_V_DIM)
    wkn = jnp.concatenate([wkv[..., :MLA_NOPE_DIM], jnp.zeros((rkv, MLA_HEADS, LANES - MLA_NOPE_DIM), BF16)], axis=-1)
    wv_t = wkv[..., MLA_NOPE_DIM:].reshape(rkv, MLA_WIDTH).T
    return wq.reshape(rq, MLA_HEADS * LANES), wkn.reshape(rkv, MLA_HEADS * LANES), wv_t


def kernel(x, norm_in_g, w_in, b_gate, q_norm_g, w_q_up, kv_norm_g, w_kv_up, w_o_sb, w_o_mla, w_out, norm_f_g):
    b, s, d = x.shape
    depth = w_in.shape[0]
    assert s % ATT_BLOCK == 0 and s % SB_BLOCK == 0 and TOKEN_TILE == ATT_BLOCK
    tabs = _rope_tables(s)
    idx = jnp.arange(SB_BLOCK)
    after_mat = (idx[None, :] > idx[:, None]).astype(BF16)

    w_in_t = jnp.swapaxes(w_in, 1, 2)
    x2 = x.reshape(b * s, d)
    for l in range(depth):
        wqk, wvsb_t, wgsb, wc, wckv, wg = _wprep_call(w_in_t, l)
        wq, wkn, wv_t = _up_weights(w_q_up[l], w_kv_up[l])
        gin = norm_in_g[l][None, :]
        qsb, ksb, vsbt, qf, kf, vt = _proj_call(
            x2, gin,
            (wqk, wvsb_t, wc, wckv, q_norm_g[l][None, :], kv_norm_g[l][None, :], wq, wkn, wv_t),
            tabs)
        r3 = lambda a: a.reshape(b, s, a.shape[-1])
        vsbt = vsbt.reshape(b, s // SB_BLOCK, SB_WIDTH, SB_BLOCK)
        vt = vt.reshape((b, s // ATT_BLOCK) + vt.shape[1:])
        o_sb = _sb_call(r3(qsb), r3(ksb), vsbt, after_mat).reshape(b * s, SB_WIDTH)
        o_mla = _mla_call(r3(qf), r3(kf), vt).reshape(b * s, MLA_WIDTH)
        x2 = _out_call(x2, o_sb, o_mla, gin, wgsb, wg, b_gate[l][None, :], w_o_sb[l].astype(BF16),
                       w_o_mla[l].astype(BF16), w_out[l].astype(BF16), norm_f_g[None, :],
                       final_norm=(l == depth - 1))
    return x2.reshape(b, s, d)
```

```python
import functools

import jax
import jax.numpy as jnp
import numpy as np
from jax import lax
from jax.experimental import pallas as pl
from jax.experimental.pallas import tpu as pltpu

EPS = 1e-6
CHUNK = 64

SB_HEADS = 8
SB_HEAD_DIM = 64
SB_WIDTH = SB_HEADS * SB_HEAD_DIM

MLA_HEADS = 8
MLA_NOPE_DIM = 64
MLA_ROPE_DIM = 32
MLA_V_DIM = 64
MLA_Q_LORA = 384
MLA_KV_LORA = 256
MLA_QK_DIM = MLA_NOPE_DIM + MLA_ROPE_DIM
MLA_WIDTH = MLA_HEADS * MLA_V_DIM
ROPE_THETA = 10000.0

LANES = 128
ATT_BLOCK = 512
SB_BLOCK = 256
SB_DEAD_LOG2 = 160.0
LOG2_E = float(np.log2(np.e))
HEADS_PER_STEP = 8
FINISH_LAG = 2
V_ROWS = 80
TOKEN_TILE = 512
OUT_TILE = 1024
WPREP_ROWS = 256
VMEM_LIMIT_BYTES = 56 * 1024 * 1024

F32 = jnp.float32
BF16 = jnp.bfloat16
NT_DIMS = (((1,), (1,)), ((), ()))


def _dot(a, b):
    return jnp.dot(a, b, preferred_element_type=F32)


def _rms(x, g):
    inv = lax.rsqrt(jnp.mean(x * x, axis=-1, keepdims=True) + EPS)
    return (x * inv) * g


def _wprep_kernel(wt_ref, wqk_ref, wvsbt_ref, wgsb_ref, wc_ref, wckv_ref, wg_ref):
    o_v, o_gsb, o_cq = 2 * SB_WIDTH, 3 * SB_WIDTH, 4 * SB_WIDTH
    o_ckv = o_cq + MLA_Q_LORA
    o_kr = o_ckv + MLA_KV_LORA
    o_gm = o_kr + MLA_ROPE_DIM
    rows = wt_ref.shape[1]
    piece = lambda lo, hi: wt_ref[lo:hi, :]
    wqk_ref[...] = piece(0, o_v).T.astype(BF16)
    wvsbt_ref[...] = piece(o_v, o_gsb).astype(BF16)
    wgsb_ref[...] = piece(o_gsb, o_cq).T.astype(BF16)
    wc_t = jnp.concatenate([piece(o_cq, o_ckv), jnp.zeros((MLA_NOPE_DIM, rows), F32), piece(o_kr, o_gm),
                            jnp.zeros((LANES - MLA_QK_DIM, rows), F32)], axis=0)
    wc_ref[...] = wc_t.T.astype(BF16)
    wckv_ref[...] = piece(o_ckv, o_kr).T.astype(BF16)
    wg_ref[...] = piece(o_gm, wt_ref.shape[0]).T.astype(BF16)


def _wprep_call(w_in_t, layer):
    _, cols, d = w_in_t.shape
    rows = WPREP_ROWS
    n_g = cols - (4 * SB_WIDTH + MLA_Q_LORA + MLA_KV_LORA + MLA_ROPE_DIM)
    row = lambda i: (i, 0)
    widths = (2 * SB_WIDTH, None, SB_WIDTH, MLA_Q_LORA + LANES, MLA_KV_LORA, n_g)
    out_specs = [pl.BlockSpec((SB_WIDTH, rows), lambda i: (0, i)) if wd is None else pl.BlockSpec((rows, wd), row)
                 for wd in widths]
    out_shape = [jax.ShapeDtypeStruct((SB_WIDTH, d) if wd is None else (d, wd), BF16) for wd in widths]
    return pl.pallas_call(
        _wprep_kernel,
        grid=(d // rows,),
        in_specs=[pl.BlockSpec((None, cols, rows), lambda i: (layer, 0, i))],
        out_specs=out_specs,
        out_shape=out_shape,
        compiler_params=pltpu.CompilerParams(
            dimension_semantics=("arbitrary",), vmem_limit_bytes=VMEM_LIMIT_BYTES),
        name="wprep",
    )(w_in_t)


def _proj_kernel(x_ref, gin_ref, wqk_ref, wvsb_ref, wc_ref, wckv_ref, qng_ref, kvng_ref, wq_ref, wkn_ref, wv_ref,
                 cos_ref, sin_ref, qsb_ref, ksb_ref, vsbt_ref, qf_ref, kf_ref, vt_ref):
    h = _rms(x_ref[...], gin_ref[...]).astype(BF16)

    c = _dot(h, wc_ref[...])
    ckv = _dot(h, wckv_ref[...])
    qsb_ref[...] = (_dot(h, wqk_ref[:, :SB_WIDTH]) * (SB_HEAD_DIM ** -0.5 * LOG2_E)).astype(BF16)
    cq = _rms(c[:, :MLA_Q_LORA], qng_ref[...]).astype(BF16)
    ckv = _rms(ckv, kvng_ref[...]).astype(BF16)

    def rope(v, cos, sin_lo, sin_hi):
        return (v * cos + pltpu.roll(v, LANES - MLA_ROPE_DIM // 2, 1) * sin_lo
                + pltpu.roll(v, MLA_ROPE_DIM // 2, 1) * sin_hi)

    lane = lax.broadcasted_iota(jnp.int32, cos_ref.shape, 1)
    ck_t, sin_t = cos_ref[...], sin_ref[...]
    sk_lo = jnp.where(lane < MLA_NOPE_DIM + MLA_ROPE_DIM // 2, sin_t, 0.0)
    sk_hi = sin_t - sk_lo
    scale = MLA_QK_DIM ** -0.5 * LOG2_E
    cq_t = jnp.where(lane < MLA_NOPE_DIM, scale, ck_t * scale)
    sq_lo, sq_hi = sk_lo * scale, sk_hi * scale

    k_rope = rope(c[:, MLA_Q_LORA:], ck_t, sk_lo, sk_hi)
    q = _dot(cq, wq_ref[...])
    kn = _dot(ckv, wkn_ref[...])
    v_t = lax.dot_general(wv_ref[...], ckv, NT_DIMS, preferred_element_type=F32)
    pad_rows = (lax.broadcasted_iota(jnp.int32, (V_ROWS - MLA_V_DIM, v_t.shape[1]), 0) == 0).astype(BF16)
    for hd in range(MLA_HEADS):
        vt_ref[hd, :MLA_V_DIM, :] = v_t[hd * MLA_V_DIM:(hd + 1) * MLA_V_DIM, :].astype(BF16)
        vt_ref[hd, MLA_V_DIM:, :] = pad_rows
    ksb_ref[...] = _dot(h, wqk_ref[:, SB_WIDTH:]).astype(BF16)
    v_sb_t = lax.dot_general(wvsb_ref[...], h, NT_DIMS, preferred_element_type=F32)
    for half in range(v_sb_t.shape[1] // SB_BLOCK):
        vsbt_ref[half] = v_sb_t[:, half * SB_BLOCK:(half + 1) * SB_BLOCK].astype(BF16)
    for hd in range(MLA_HEADS):
        sl = slice(hd * LANES, (hd + 1) * LANES)
        qf_ref[:, sl] = rope(q[:, sl], cq_t, sq_lo, sq_hi).astype(BF16)
        kf_ref[:, sl] = (kn[:, sl] + k_rope).astype(BF16)


def _proj_call(x2, gin, wts, tabs):
    n, d = x2.shape
    s = tabs[0].shape[0]
    tm = min(TOKEN_TILE, s)
    row = lambda i: (i, 0)
    fix = lambda i: (0, 0)
    w_specs = [pl.BlockSpec(w.shape, fix) for w in wts]
    t_specs = [pl.BlockSpec((tm, LANES), lambda i: (i % (s // tm), 0)) for _ in tabs]
    vsbt_spec = pl.BlockSpec((None, tm // SB_BLOCK, SB_WIDTH, SB_BLOCK), lambda i: (i, 0, 0, 0))
    vsbt_shape = jax.ShapeDtypeStruct((n // tm, tm // SB_BLOCK, SB_WIDTH, SB_BLOCK), BF16)
    vt_spec = pl.BlockSpec((None, MLA_HEADS, V_ROWS, tm), lambda i: (i, 0, 0, 0))
    vt_shape = jax.ShapeDtypeStruct((n // tm, MLA_HEADS, V_ROWS, tm), BF16)
    tile = lambda w: pl.BlockSpec((tm, w), row)
    sds = lambda w: jax.ShapeDtypeStruct((n, w), BF16)
    return pl.pallas_call(
        _proj_kernel,
        grid=(n // tm,),
        in_specs=[pl.BlockSpec((tm, d), row), pl.BlockSpec(gin.shape, fix)] + w_specs + t_specs,
        out_specs=[tile(SB_WIDTH), tile(SB_WIDTH), vsbt_spec, tile(MLA_HEADS * LANES), tile(MLA_HEADS * LANES),
                   vt_spec],
        out_shape=[sds(SB_WIDTH), sds(SB_WIDTH), vsbt_shape, sds(MLA_HEADS * LANES), sds(MLA_HEADS * LANES),
                   vt_shape],
        compiler_params=pltpu.CompilerParams(
            dimension_semantics=("arbitrary",), vmem_limit_bytes=VMEM_LIMIT_BYTES),
        name="proj",
    )(x2, gin, *wts, *tabs)


def _sb_kernel(q_ref, kin_ref, vtin_ref, u_ref, o_ref, k_ref, vt_ref, acc_ref):
    t = SB_BLOCK
    qi = pl.program_id(1)
    k_ref[qi] = kin_ref[...]
    vt_ref[qi] = vtin_ref[...]

    first = lax.broadcasted_iota(jnp.int32, (LANES, t), 0) < SB_HEAD_DIM
    q_heads = []
    for p in range(SB_HEADS // 2):
        q_t = q_ref[:, p * LANES:(p + 1) * LANES].astype(F32).T
        q_heads += [jnp.where(first, q_t, 0.0).astype(BF16), jnp.where(first, 0.0, q_t).astype(BF16)]

    half = t // 2
    key_pos = lax.broadcasted_iota(jnp.int32, (half, half), 0)
    qry_pos = lax.broadcasted_iota(jnp.int32, (half, half), 1)
    before = key_pos < qry_pos

    def visit(blocks, r):
        heads = range(SB_HEADS)
        keep = lambda mask, a: a if mask is None else jnp.where(mask, a, 0.0)

        def layout(diagonal):
            if not diagonal:
                return (lambda a: (a,)), (lambda a: a), (None,), (slice(0, t),)
            split = lambda a: (a[:half, :half], a[:half, half:], a[half:, half:])
            join = lambda tl, tr, br: jnp.concatenate(
                [jnp.concatenate([tl, tr], axis=1), jnp.concatenate([jnp.zeros_like(tl), br], axis=1)], axis=0)
            return split, join, (before, None, before), (slice(0, half), slice(half, t), slice(half, t))

        z, log_beta, later, r_in = [], [], [], [list(r)]

        def scores(blk):
            k_blk = k_ref[blocks[blk][0]]
            z.append([_dot(k_blk[:, (hd // 2) * LANES:(hd // 2 + 1) * LANES], q_heads[hd]) for hd in heads])

        def begin_sums(blk):
            live = blocks[blk][2]
            log_beta.append([])
            later.append([])
            if live is not None:
                r_in[blk] = [jnp.where(live, x, -jnp.inf) for x in r_in[blk]]
            r_in.append([])

        def sums(blk, hd):
            split, join, masks, _ = layout(blocks[blk][1])
            parts = []
            for z_part in split(z[blk][hd]):
                neg_log_1m = jnp.maximum(z_part, 0.0) + jnp.log(1.0 + jnp.exp2(-jnp.abs(z_part))) * LOG2_E
                parts.append((neg_log_1m, z_part - neg_log_1m))
            log_beta[blk].append([lb for _, lb in parts])
            neg_log_1m = join(*[keep(mask, sp) for mask, (sp, _) in zip(masks, parts)]).astype(BF16)
            later[blk].append(_dot(u_ref[...], neg_log_1m))
            r_in[blk + 1].append(r_in[blk][hd] - later[blk][hd][0:1, :] - neg_log_1m[0:1, :].astype(F32))

        def values(blk, hd):
            j, diagonal, _ = blocks[blk]
            split, join, masks, queries = layout(diagonal)
            w = [keep(mask, jnp.exp2(lb - lt + r_in[blk][hd][:, qs]))
                 for mask, qs, lb, lt in zip(masks, queries, log_beta[blk][hd], split(later[blk][hd]))]
            vt_blk = vt_ref[j, hd * SB_HEAD_DIM:(hd + 1) * SB_HEAD_DIM, :]
            acc_ref[hd] += _dot(vt_blk, join(*w).astype(BF16))

        scores(0)
        begin_sums(0)
        for hd in heads:
            sums(0, hd)
        for blk in range(1, len(blocks)):
            scores(blk)
            begin_sums(blk)
            for hd in heads:
                values(blk - 1, hd)
                sums(blk, hd)
        for hd in heads:
            values(len(blocks) - 1, hd)
        return tuple(r_in[-1])

    def r_max(r):
        return functools.reduce(jnp.maximum, [jnp.max(x) for x in r])

    acc_ref[...] = jnp.zeros_like(acc_ref)
    zero = jnp.zeros((1, t), F32)
    r = visit([(qi, True, None), (jnp.maximum(qi - 1, 0), False, qi > 0)], (zero,) * SB_HEADS)

    def cond(c):
        j, rmax, _ = c
        return jnp.logical_and(j >= 0, rmax > -SB_DEAD_LOG2)

    def body(c):
        j, _, r = c
        r = visit([(j, False, None)], r)
        return j - 1, r_max(r), r

    lax.while_loop(cond, body, (qi - 2, r_max(r), r))

    o_ref[...] = jnp.concatenate([acc_ref[hd] for hd in range(SB_HEADS)], axis=0).T


def _sb_call(q, k, vt, u):
    b, s, w = q.shape
    t = SB_BLOCK
    nblk = s // t
    assert vt.shape == (b, nblk, w, t)
    return pl.pallas_call(
        _sb_kernel,
        grid=(b, nblk),
        in_specs=[
            pl.BlockSpec((None, t, w), lambda bi, qi: (bi, qi, 0)),
            pl.BlockSpec((None, t, w), lambda bi, qi: (bi, qi, 0)),
            pl.BlockSpec((None, None, w, t), lambda bi, qi: (bi, qi, 0, 0)),
            pl.BlockSpec(u.shape, lambda bi, qi: (0, 0)),
        ],
        out_specs=pl.BlockSpec((None, t, w), lambda bi, qi: (bi, qi, 0)),
        out_shape=jax.ShapeDtypeStruct((b, s, w), F32),
        scratch_shapes=[
            pltpu.VMEM((nblk, t, w), BF16),
            pltpu.VMEM((nblk, w, t), BF16),
            pltpu.VMEM((SB_HEADS, SB_HEAD_DIM, t), F32),
        ],
        compiler_params=pltpu.CompilerParams(
            dimension_semantics=("arbitrary", "arbitrary"), vmem_limit_bytes=VMEM_LIMIT_BYTES),
        name="sb_attn",
    )(q, k, vt, u)


def _mla_kernel(q_ref, kin_ref, vtin_ref, o_ref, k_ref, vt_ref, acc_ref, m_ref, s0_ref, s1_ref, smax0_ref,
                smax1_ref):
    t = ATT_BLOCK
    qi = pl.program_id(1)
    k_ref[qi] = kin_ref[...]
    vt_ref[qi] = vtin_ref[...]

    q_heads = [q_ref[:, hd * LANES:(hd + 1) * LANES].astype(F32).T.astype(BF16) for hd in range(HEADS_PER_STEP)]

    halves = (slice(0, t // 2), slice(t // 2, t))
    key_chunk = lax.broadcasted_iota(jnp.int32, (t // 2, t // 2), 0) // CHUNK
    qry_chunk = lax.broadcasted_iota(jnp.int32, (t // 2, t // 2), 1) // CHUNK
    within = key_chunk <= qry_chunk

    def score_rows(hd, j, buf, rows):
        s = _dot(k_ref[j, rows, hd * LANES:(hd + 1) * LANES], q_heads[hd])
        buf[0][hd, rows, :] = s
        return jnp.max(s, axis=0, keepdims=True)

    def diagonal_scores(hd, buf):
        lo, hi = halves
        q_t = q_heads[hd]
        s_lo = _dot(k_ref[qi, lo, hd * LANES:(hd + 1) * LANES], q_t)
        s_hi = _dot(k_ref[qi, hi, hd * LANES:(hd + 1) * LANES], q_t[:, hi])
        s_lo_lo = jnp.where(within, s_lo[:, lo], -jnp.inf)
        s_hi_hi = jnp.where(within, s_hi, -jnp.inf)
        buf[0][hd, lo, lo] = s_lo_lo
        buf[0][hd, lo, hi] = s_lo[:, hi]
        buf[0][hd, hi, lo] = jnp.full_like(s_hi, -jnp.inf)
        buf[0][hd, hi, hi] = s_hi_hi
        col_max = lambda a: jnp.max(a, axis=0, keepdims=True)
        buf[1][hd] = jnp.concatenate(
            [col_max(s_lo_lo), jnp.maximum(col_max(s_lo[:, hi]), col_max(s_hi_hi))], axis=1)

    def rescale(hd, buf):
        m = m_ref[hd]
        m_new = jnp.maximum(m, buf[1][hd])
        m_ref[hd] = m_new
        acc_ref[hd] = jnp.exp2(m - m_new) * acc_ref[hd]
        return m_new

    def accumulate(hd, j, buf, rows, m_new):
        p = jnp.exp2(buf[0][hd, rows, :] - m_new).astype(BF16)
        acc_ref[hd] += _dot(vt_ref[j, hd, :, rows], p)

    def update(hd, j, buf):
        m_new = rescale(hd, buf)
        for rows in halves:
            accumulate(hd, j, buf, rows, m_new)

    def step(hd, j_next, buf_next, j_cur, buf_cur):
        m_new = rescale(hd, buf_cur)
        maxes = []
        for rows in halves:
            maxes.append(score_rows(hd, j_next, buf_next, rows))
            accumulate(hd, j_cur, buf_cur, rows, m_new)
        buf_next[1][hd] = jnp.maximum(*maxes)

    acc_ref[...] = jnp.zeros_like(acc_ref)
    m_ref[...] = jnp.full_like(m_ref, -jnp.inf)
    bufs = ((s0_ref, smax0_ref), (s1_ref, smax1_ref))

    block_of = lambda k: jnp.where(k == 0, qi, k - 1)
    for hd in range(HEADS_PER_STEP):
        diagonal_scores(hd, bufs[0])

    def consume(k, parity):
        for hd in range(HEADS_PER_STEP):
            step(hd, block_of(k + 1), bufs[1 - parity], block_of(k), bufs[parity])

    def body(pair, carry):
        consume(2 * pair, 0)
        consume(2 * pair + 1, 1)
        return carry

    def finish(parity):
        for hd in range(HEADS_PER_STEP):
            update(hd, block_of(qi), bufs[parity])

    odd = qi % 2
    lax.fori_loop(0, jnp.maximum(qi // 2 + odd - 1, 0), body, 0)

    def consume_and_finish(parity):
        for hd in range(HEADS_PER_STEP + FINISH_LAG):
            if hd < HEADS_PER_STEP:
                step(hd, block_of(qi), bufs[1 - parity], block_of(qi - 1), bufs[parity])
            if hd >= FINISH_LAG:
                update(hd - FINISH_LAG, block_of(qi), bufs[1 - parity])

    @pl.when(odd == 1)
    def _():
        consume_and_finish(0)

    @pl.when(jnp.logical_and(odd == 0, qi > 0))
    def _():
        consume(qi - 2, 0)
        consume_and_finish(1)

    @pl.when(qi == 0)
    def _():
        finish(0)

    o_ref[...] = jnp.concatenate(
        [acc_ref[hd, :MLA_V_DIM, :] / acc_ref[hd, MLA_V_DIM:MLA_V_DIM + 1, :] for hd in range(HEADS_PER_STEP)],
        axis=0).T


def _mla_call(qf, kf, vt):
    b, s, w = qf.shape
    t = ATT_BLOCK
    nblk = s // t
    assert vt.shape == (b, nblk, MLA_HEADS, V_ROWS, t) and HEADS_PER_STEP == MLA_HEADS
    return pl.pallas_call(
        _mla_kernel,
        grid=(b, nblk),
        in_specs=[
            pl.BlockSpec((None, t, w), lambda bi, qi: (bi, qi, 0)),
            pl.BlockSpec((None, t, w), lambda bi, qi: (bi, qi, 0)),
            pl.BlockSpec((None, None, MLA_HEADS, V_ROWS, t), lambda bi, qi: (bi, qi, 0, 0, 0)),
        ],
        out_specs=pl.BlockSpec((None, t, MLA_WIDTH), lambda bi, qi: (bi, qi, 0)),
        out_shape=jax.ShapeDtypeStruct((b, s, MLA_WIDTH), F32),
        scratch_shapes=[
            pltpu.VMEM((nblk, t, w), BF16),
            pltpu.VMEM((nblk, MLA_HEADS, V_ROWS, t), BF16),
            pltpu.VMEM((HEADS_PER_STEP, V_ROWS, t), F32),
            pltpu.VMEM((HEADS_PER_STEP, 1, t), F32),
            pltpu.VMEM((HEADS_PER_STEP, t, t), F32),
            pltpu.VMEM((HEADS_PER_STEP, t, t), F32),
            pltpu.VMEM((HEADS_PER_STEP, 1, t), F32),
            pltpu.VMEM((HEADS_PER_STEP, 1, t), F32),
        ],
        compiler_params=pltpu.CompilerParams(
            dimension_semantics=("arbitrary", "arbitrary"), vmem_limit_bytes=VMEM_LIMIT_BYTES),
        name="mla_attn",
    )(qf, kf, vt)


def _out_kernel(x_ref, osb_ref, omla_ref, gin_ref, wgsb_ref, wg_ref, bg_ref, wosb_ref, womla_ref, wout_ref,
                gf_ref, o_ref, *, final_norm):
    x = x_ref[...]
    d = x.shape[-1]
    h = _rms(x, gin_ref[...]).astype(BF16)
    a_sb = (osb_ref[...] * jax.nn.silu(_dot(h, wgsb_ref[...]))).astype(BF16)
    a_mla = (omla_ref[...] * jax.nn.silu(_dot(h, wg_ref[:, :MLA_WIDTH]))).astype(BF16)
    g0 = MLA_WIDTH
    g_sb = jax.nn.sigmoid(_dot(h, wg_ref[:, g0:g0 + d]) + bg_ref[:, :d])
    merged = g_sb * _dot(a_sb, wosb_ref[...])
    g_mla = jax.nn.sigmoid(_dot(h, wg_ref[:, g0 + d:g0 + 2 * d]) + bg_ref[:, d:])
    merged = merged + g_mla * _dot(a_mla, womla_ref[...])
    y = x + _dot(merged.astype(BF16), wout_ref[...])
    o_ref[...] = _rms(y, gf_ref[...]) if final_norm else y


def _out_call(x2, osb, omla, gin, wgsb, wg, bg, wosb, womla, wout, gf, final_norm):
    n, d = x2.shape
    tm = min(OUT_TILE, n)
    row = lambda i: (i, 0)
    fix = lambda i: (0, 0)
    consts = (gin, wgsb, wg, bg, wosb, womla, wout, gf)
    return pl.pallas_call(
        functools.partial(_out_kernel, final_norm=final_norm),
        grid=(n // tm,),
        in_specs=[pl.BlockSpec((tm, d), row), pl.BlockSpec((tm, SB_WIDTH), row), pl.BlockSpec((tm, MLA_WIDTH), row)]
                 + [pl.BlockSpec(c.shape, fix, pipeline_mode=pl.Buffered(1)) for c in consts],
        out_specs=pl.BlockSpec((tm, d), row),
        out_shape=jax.ShapeDtypeStruct((n, d), F32),
        compiler_params=pltpu.CompilerParams(
            dimension_semantics=("arbitrary",), vmem_limit_bytes=VMEM_LIMIT_BYTES),
        name="out",
    )(x2, osb, omla, *consts)


def _rope_tables(s):
    half = MLA_ROPE_DIM // 2
    inv_freq = ROPE_THETA ** (-np.arange(half, dtype=np.float64) / half)
    ang = np.arange(s, dtype=np.float64)[:, None] * inv_freq[None, :]
    cos, sin = np.cos(ang), np.sin(ang)
    lanes = lambda lo, hi: np.concatenate(
        [np.zeros((s, MLA_NOPE_DIM)), lo, hi, np.zeros((s, LANES - MLA_QK_DIM))], axis=1).astype(np.float32)
    return jnp.asarray(lanes(cos, cos)), jnp.asarray(lanes(-sin, sin))


def _up_weights(w_q_up, w_kv_up):
    rq = w_q_up.shape[0]
    wq = w_q_up.astype(BF16).reshape(rq, MLA_HEADS, MLA_QK_DIM)
    wq = jnp.concatenate([wq, jnp.zeros((rq, MLA_HEADS, LANES - MLA_QK_DIM), BF16)], axis=-1)
    rkv = w_kv_up.shape[0]
    wkv = w_kv_up.astype(BF16).reshape(rkv, MLA_HEADS, MLA_NOPE_DIM + MLA_V_DIM)
    wkn = jnp.concatenate([wkv[..., :MLA_NOPE_DIM], jnp.zeros((rkv, MLA_HEADS, LANES - MLA_NOPE_DIM), BF16)], axis=-1)
    wv_t = wkv[..., MLA_NOPE_DIM:].reshape(rkv, MLA_WIDTH).T
    return wq.reshape(rq, MLA_HEADS * LANES), wkn.reshape(rkv, MLA_HEADS * LANES), wv_t


def kernel(x, norm_in_g, w_in, b_gate, q_norm_g, w_q_up, kv_norm_g, w_kv_up, w_o_sb, w_o_mla, w_out, norm_f_g):
    b, s, d = x.shape
    depth = w_in.shape[0]
    assert s % ATT_BLOCK == 0 and s % SB_BLOCK == 0 and TOKEN_TILE == ATT_BLOCK
    tabs = _rope_tables(s)
    idx = jnp.arange(SB_BLOCK)
    after_mat = (idx[None, :] > idx[:, None]).astype(BF16)

    w_in_t = jnp.swapaxes(w_in, 1, 2)
    x2 = x.reshape(b * s, d)
    for l in range(depth):
        wqk, wvsb_t, wgsb, wc, wckv, wg = _wprep_call(w_in_t, l)
        wq, wkn, wv_t = _up_weights(w_q_up[l], w_kv_up[l])
        gin = norm_in_g[l][None, :]
        qsb, ksb, vsbt, qf, kf, vt = _proj_call(
            x2, gin,
            (wqk, wvsb_t, wc, wckv, q_norm_g[l][None, :], kv_norm_g[l][None, :], wq, wkn, wv_t),
            tabs)
        r3 = lambda a: a.reshape(b, s, a.shape[-1])
        vsbt = vsbt.reshape(b, s // SB_BLOCK, SB_WIDTH, SB_BLOCK)
        vt = vt.reshape((b, s // ATT_BLOCK) + vt.shape[1:])
        o_sb = _sb_call(r3(qsb), r3(ksb), vsbt, after_mat).reshape(b * s, SB_WIDTH)
        o_mla = _mla_call(r3(qf), r3(kf), vt).reshape(b * s, MLA_WIDTH)
        x2 = _out_call(x2, o_sb, o_mla, gin, wgsb, wg, b_gate[l][None, :], w_o_sb[l].astype(BF16),
                       w_o_mla[l].astype(BF16), w_out[l].astype(BF16), norm_f_g[None, :],
                       final_norm=(l == depth - 1))
    return x2.reshape(b, s, d)
```

```python
import functools

import jax
import jax.numpy as jnp
import numpy as np
from jax import lax
from jax.experimental import pallas as pl
from jax.experimental.pallas import tpu as pltpu

EPS = 1e-6
CHUNK = 64

SB_HEADS = 8
SB_HEAD_DIM = 64
SB_WIDTH = SB_HEADS * SB_HEAD_DIM

MLA_HEADS = 8
MLA_NOPE_DIM = 64
MLA_ROPE_DIM = 32
MLA_V_DIM = 64
MLA_Q_LORA = 384
MLA_KV_LORA = 256
MLA_QK_DIM = MLA_NOPE_DIM + MLA_ROPE_DIM
MLA_WIDTH = MLA_HEADS * MLA_V_DIM
ROPE_THETA = 10000.0

LANES = 128
ATT_BLOCK = 512
SB_BLOCK = 256
SB_DEAD_LOG2 = 160.0
LOG2_E = float(np.log2(np.e))
HEADS_PER_STEP = 8
FINISH_LAG = 2
V_ROWS = 80
TOKEN_TILE = 512
OUT_TILE = 1024
WPREP_ROWS = 256
VMEM_LIMIT_BYTES = 56 * 1024 * 1024

F32 = jnp.float32
BF16 = jnp.bfloat16
NT_DIMS = (((1,), (1,)), ((), ()))


def _dot(a, b):
    return jnp.dot(a, b, preferred_element_type=F32)


def _rms(x, g):
    inv = lax.rsqrt(jnp.mean(x * x, axis=-1, keepdims=True) + EPS)
    return (x * inv) * g


def _wprep_kernel(wt_ref, wqk_ref, wvsbt_ref, wgsb_ref, wc_ref, wckv_ref, wg_ref):
    o_v, o_gsb, o_cq = 2 * SB_WIDTH, 3 * SB_WIDTH, 4 * SB_WIDTH
    o_ckv = o_cq + MLA_Q_LORA
    o_kr = o_ckv + MLA_KV_LORA
    o_gm = o_kr + MLA_ROPE_DIM
    rows = wt_ref.shape[1]
    piece = lambda lo, hi: wt_ref[lo:hi, :]
    wqk_ref[...] = piece(0, o_v).T.astype(BF16)
    wvsbt_ref[...] = piece(o_v, o_gsb).astype(BF16)
    wgsb_ref[...] = piece(o_gsb, o_cq).T.astype(BF16)
    wc_t = jnp.concatenate([piece(o_cq, o_ckv), jnp.zeros((MLA_NOPE_DIM, rows), F32), piece(o_kr, o_gm),
                            jnp.zeros((LANES - MLA_QK_DIM, rows), F32)], axis=0)
    wc_ref[...] = wc_t.T.astype(BF16)
    wckv_ref[...] = piece(o_ckv, o_kr).T.astype(BF16)
    wg_ref[...] = piece(o_gm, wt_ref.shape[0]).T.astype(BF16)


def _wprep_call(w_in_t, layer):
    _, cols, d = w_in_t.shape
    rows = WPREP_ROWS
    assert d % rows == 0
    n_g = cols - (4 * SB_WIDTH + MLA_Q_LORA + MLA_KV_LORA + MLA_ROPE_DIM)
    row = lambda i: (i, 0)
    widths = (2 * SB_WIDTH, None, SB_WIDTH, MLA_Q_LORA + LANES, MLA_KV_LORA, n_g)
    out_specs = [pl.BlockSpec((SB_WIDTH, rows), lambda i: (0, i)) if wd is None else pl.BlockSpec((rows, wd), row)
                 for wd in widths]
    out_shape = [jax.ShapeDtypeStruct((SB_WIDTH, d) if wd is None else (d, wd), BF16) for wd in widths]
    return pl.pallas_call(
        _wprep_kernel,
        grid=(d // rows,),
        in_specs=[pl.BlockSpec((None, cols, rows), lambda i: (layer, 0, i))],
        out_specs=out_specs,
        out_shape=out_shape,
        compiler_params=pltpu.CompilerParams(
            dimension_semantics=("arbitrary",), vmem_limit_bytes=VMEM_LIMIT_BYTES),
        name="wprep",
    )(w_in_t)


def _proj_kernel(x_ref, gin_ref, wqk_ref, wvsb_ref, wc_ref, wckv_ref, qng_ref, kvng_ref, wq_ref, wkn_ref, wv_ref,
                 cos_ref, sin_ref, qsb_ref, ksb_ref, vsbt_ref, qf_ref, kf_ref, vt_ref):
    h = _rms(x_ref[...], gin_ref[...]).astype(BF16)

    c = _dot(h, wc_ref[...])
    ckv = _dot(h, wckv_ref[...])
    qsb_ref[...] = (_dot(h, wqk_ref[:, :SB_WIDTH]) * (SB_HEAD_DIM ** -0.5 * LOG2_E)).astype(BF16)
    cq = _rms(c[:, :MLA_Q_LORA], qng_ref[...]).astype(BF16)
    ckv = _rms(ckv, kvng_ref[...]).astype(BF16)

    def rope(v, cos, sin_lo, sin_hi):
        return (v * cos + pltpu.roll(v, LANES - MLA_ROPE_DIM // 2, 1) * sin_lo
                + pltpu.roll(v, MLA_ROPE_DIM // 2, 1) * sin_hi)

    lane = lax.broadcasted_iota(jnp.int32, cos_ref.shape, 1)
    ck_t, sin_t = cos_ref[...], sin_ref[...]
    sk_lo = jnp.where(lane < MLA_NOPE_DIM + MLA_ROPE_DIM // 2, sin_t, 0.0)
    sk_hi = sin_t - sk_lo
    scale = MLA_QK_DIM ** -0.5 * LOG2_E
    cq_t = jnp.where(lane < MLA_NOPE_DIM, scale, ck_t * scale)
    sq_lo, sq_hi = sk_lo * scale, sk_hi * scale

    k_rope = rope(c[:, MLA_Q_LORA:], ck_t, sk_lo, sk_hi)
    q = _dot(cq, wq_ref[...])
    kn = _dot(ckv, wkn_ref[...])
    v_t = lax.dot_general(wv_ref[...], ckv, NT_DIMS, preferred_element_type=F32)
    pad_rows = (lax.broadcasted_iota(jnp.int32, (V_ROWS - MLA_V_DIM, v_t.shape[1]), 0) == 0).astype(BF16)
    for hd in range(MLA_HEADS):
        vt_ref[hd, :MLA_V_DIM, :] = v_t[hd * MLA_V_DIM:(hd + 1) * MLA_V_DIM, :].astype(BF16)
        vt_ref[hd, MLA_V_DIM:, :] = pad_rows
    ksb_ref[...] = _dot(h, wqk_ref[:, SB_WIDTH:]).astype(BF16)
    v_sb_t = lax.dot_general(wvsb_ref[...], h, NT_DIMS, preferred_element_type=F32)
    for half in range(v_sb_t.shape[1] // SB_BLOCK):
        vsbt_ref[half] = v_sb_t[:, half * SB_BLOCK:(half + 1) * SB_BLOCK].astype(BF16)
    for hd in range(MLA_HEADS):
        sl = slice(hd * LANES, (hd + 1) * LANES)
        qf_ref[:, sl] = rope(q[:, sl], cq_t, sq_lo, sq_hi).astype(BF16)
        kf_ref[:, sl] = (kn[:, sl] + k_rope).astype(BF16)


def _proj_call(x2, gin, wts, tabs):
    n, d = x2.shape
    s = tabs[0].shape[0]
    tm = min(TOKEN_TILE, s)
    row = lambda i: (i, 0)
    fix = lambda i: (0, 0)
    w_specs = [pl.BlockSpec(w.shape, fix) for w in wts]
    t_specs = [pl.BlockSpec((tm, LANES), lambda i: (i % (s // tm), 0)) for _ in tabs]
    vsbt_spec = pl.BlockSpec((None, tm // SB_BLOCK, SB_WIDTH, SB_BLOCK), lambda i: (i, 0, 0, 0))
    vsbt_shape = jax.ShapeDtypeStruct((n // tm, tm // SB_BLOCK, SB_WIDTH, SB_BLOCK), BF16)
    vt_spec = pl.BlockSpec((None, MLA_HEADS, V_ROWS, tm), lambda i: (i, 0, 0, 0))
    vt_shape = jax.ShapeDtypeStruct((n // tm, MLA_HEADS, V_ROWS, tm), BF16)
    tile = lambda w: pl.BlockSpec((tm, w), row)
    sds = lambda w: jax.ShapeDtypeStruct((n, w), BF16)
    return pl.pallas_call(
        _proj_kernel,
        grid=(n // tm,),
        in_specs=[pl.BlockSpec((tm, d), row), pl.BlockSpec(gin.shape, fix)] + w_specs + t_specs,
        out_specs=[tile(SB_WIDTH), tile(SB_WIDTH), vsbt_spec, tile(MLA_HEADS * LANES), tile(MLA_HEADS * LANES),
                   vt_spec],
        out_shape=[sds(SB_WIDTH), sds(SB_WIDTH), vsbt_shape, sds(MLA_HEADS * LANES), sds(MLA_HEADS * LANES),
                   vt_shape],
        compiler_params=pltpu.CompilerParams(
            dimension_semantics=("arbitrary",), vmem_limit_bytes=VMEM_LIMIT_BYTES),
        name="proj",
    )(x2, gin, *wts, *tabs)


def _sb_kernel(q_ref, kin_ref, vtin_ref, u_ref, o_ref, k_ref, vt_ref, acc_ref):
    t = SB_BLOCK
    qi = pl.program_id(1)
    k_ref[qi] = kin_ref[...]
    vt_ref[qi] = vtin_ref[...]

    first = lax.broadcasted_iota(jnp.int32, (LANES, t), 0) < SB_HEAD_DIM
    q_heads = []
    for p in range(SB_HEADS // 2):
        q_t = q_ref[:, p * LANES:(p + 1) * LANES].astype(F32).T
        q_heads += [jnp.where(first, q_t, 0.0).astype(BF16), jnp.where(first, 0.0, q_t).astype(BF16)]

    half = t // 2
    key_pos = lax.broadcasted_iota(jnp.int32, (half, half), 0)
    qry_pos = lax.broadcasted_iota(jnp.int32, (half, half), 1)
    before = key_pos < qry_pos

    def visit(blocks, r):
        heads = range(SB_HEADS)
        keep = lambda mask, a: a if mask is None else jnp.where(mask, a, 0.0)

        def layout(diagonal):
            if not diagonal:
                return (lambda a: (a,)), (lambda a: a), (None,), (slice(0, t),)
            split = lambda a: (a[:half, :half], a[:half, half:], a[half:, half:])
            join = lambda tl, tr, br: jnp.concatenate(
                [jnp.concatenate([tl, tr], axis=1), jnp.concatenate([jnp.zeros_like(tl), br], axis=1)], axis=0)
            return split, join, (before, None, before), (slice(0, half), slice(half, t), slice(half, t))

        z, log_beta, later, r_in = [], [], [], [list(r)]

        def scores(blk):
            k_blk = k_ref[blocks[blk][0]]
            z.append([_dot(k_blk[:, (hd // 2) * LANES:(hd // 2 + 1) * LANES], q_heads[hd]) for hd in heads])

        def begin_sums(blk):
            live = blocks[blk][2]
            log_beta.append([])
            later.append([])
            if live is not None:
                r_in[blk] = [jnp.where(live, x, -jnp.inf) for x in r_in[blk]]
            r_in.append([])

        def sums(blk, hd):
            split, join, masks, _ = layout(blocks[blk][1])
            parts = []
            for z_part in split(z[blk][hd]):
                neg_log_1m = jnp.maximum(z_part, 0.0) + jnp.log(1.0 + jnp.exp2(-jnp.abs(z_part))) * LOG2_E
                parts.append((neg_log_1m, z_part - neg_log_1m))
            log_beta[blk].append([lb for _, lb in parts])
            neg_log_1m = join(*[keep(mask, sp) for mask, (sp, _) in zip(masks, parts)]).astype(BF16)
            later[blk].append(_dot(u_ref[...], neg_log_1m))
            r_in[blk + 1].append(r_in[blk][hd] - later[blk][hd][0:1, :] - neg_log_1m[0:1, :].astype(F32))

        def values(blk, hd):
            j, diagonal, _ = blocks[blk]
            split, join, masks, queries = layout(diagonal)
            w = [keep(mask, jnp.exp2(lb - lt + r_in[blk][hd][:, qs]))
                 for mask, qs, lb, lt in zip(masks, queries, log_beta[blk][hd], split(later[blk][hd]))]
            vt_blk = vt_ref[j, hd * SB_HEAD_DIM:(hd + 1) * SB_HEAD_DIM, :]
            acc_ref[hd] += _dot(vt_blk, join(*w).astype(BF16))

        scores(0)
        begin_sums(0)
        for hd in heads:
            sums(0, hd)
        for blk in range(1, len(blocks)):
            scores(blk)
            begin_sums(blk)
            for hd in heads:
                values(blk - 1, hd)
                sums(blk, hd)
        for hd in heads:
            values(len(blocks) - 1, hd)
        return tuple(r_in[-1])

    def r_max(r):
        return functools.reduce(jnp.maximum, [jnp.max(x) for x in r])

    acc_ref[...] = jnp.zeros_like(acc_ref)
    zero = jnp.zeros((1, t), F32)
    r = visit([(qi, True, None), (jnp.maximum(qi - 1, 0), False, qi > 0)], (zero,) * SB_HEADS)

    def cond(c):
        j, rmax, _ = c
        return jnp.logical_and(j >= 0, rmax > -SB_DEAD_LOG2)

    def body(c):
        j, _, r = c
        r = visit([(j, False, None)], r)
        return j - 1, r_max(r), r

    lax.while_loop(cond, body, (qi - 2, r_max(r), r))

    o_ref[...] = jnp.concatenate([acc_ref[hd] for hd in range(SB_HEADS)], axis=0).T


def _sb_call(q, k, vt, u):
    b, s, w = q.shape
    t = SB_BLOCK
    nblk = s // t
    assert vt.shape == (b, nblk, w, t)
    return pl.pallas_call(
        _sb_kernel,
        grid=(b, nblk),
        in_specs=[
            pl.BlockSpec((None, t, w), lambda bi, qi: (bi, qi, 0)),
            pl.BlockSpec((None, t, w), lambda bi, qi: (bi, qi, 0)),
            pl.BlockSpec((None, None, w, t), lambda bi, qi: (bi, qi, 0, 0)),
            pl.BlockSpec(u.shape, lambda bi, qi: (0, 0)),
        ],
        out_specs=pl.BlockSpec((None, t, w), lambda bi, qi: (bi, qi, 0)),
        out_shape=jax.ShapeDtypeStruct((b, s, w), F32),
        scratch_shapes=[
            pltpu.VMEM((nblk, t, w), BF16),
            pltpu.VMEM((nblk, w, t), BF16),
            pltpu.VMEM((SB_HEADS, SB_HEAD_DIM, t), F32),
        ],
        compiler_params=pltpu.CompilerParams(
            dimension_semantics=("arbitrary", "arbitrary"), vmem_limit_bytes=VMEM_LIMIT_BYTES),
        name="sb_attn",
    )(q, k, vt, u)


def _mla_kernel(q_ref, kin_ref, vtin_ref, o_ref, k_ref, vt_ref, acc_ref, m_ref, s0_ref, s1_ref, smax0_ref,
                smax1_ref):
    t = ATT_BLOCK
    qi = pl.program_id(1)
    k_ref[qi] = kin_ref[...]
    vt_ref[qi] = vtin_ref[...]

    q_heads = [q_ref[:, hd * LANES:(hd + 1) * LANES].astype(F32).T.astype(BF16) for hd in range(HEADS_PER_STEP)]

    halves = (slice(0, t // 2), slice(t // 2, t))
    key_chunk = lax.broadcasted_iota(jnp.int32, (t // 2, t // 2), 0) // CHUNK
    qry_chunk = lax.broadcasted_iota(jnp.int32, (t // 2, t // 2), 1) // CHUNK
    within = key_chunk <= qry_chunk

    def score_rows(hd, j, buf, rows):
        s = _dot(k_ref[j, rows, hd * LANES:(hd + 1) * LANES], q_heads[hd])
        buf[0][hd, rows, :] = s
        return jnp.max(s, axis=0, keepdims=True)

    def diagonal_scores(hd, buf):
        lo, hi = halves
        q_t = q_heads[hd]
        s_lo = _dot(k_ref[qi, lo, hd * LANES:(hd + 1) * LANES], q_t)
        s_hi = _dot(k_ref[qi, hi, hd * LANES:(hd + 1) * LANES], q_t[:, hi])
        s_lo_lo = jnp.where(within, s_lo[:, lo], -jnp.inf)
        s_hi_hi = jnp.where(within, s_hi, -jnp.inf)
        buf[0][hd, lo, lo] = s_lo_lo
        buf[0][hd, lo, hi] = s_lo[:, hi]
        buf[0][hd, hi, lo] = jnp.full_like(s_hi, -jnp.inf)
        buf[0][hd, hi, hi] = s_hi_hi
        col_max = lambda a: jnp.max(a, axis=0, keepdims=True)
        buf[1][hd] = jnp.concatenate(
            [col_max(s_lo_lo), jnp.maximum(col_max(s_lo[:, hi]), col_max(s_hi_hi))], axis=1)

    def rescale(hd, buf):
        m = m_ref[hd]
        m_new = jnp.maximum(m, buf[1][hd])
        m_ref[hd] = m_new
        acc_ref[hd] = jnp.exp2(m - m_new) * acc_ref[hd]
        return m_new

    def accumulate(hd, j, buf, rows, m_new):
        p = jnp.exp2(buf[0][hd, rows, :] - m_new).astype(BF16)
        acc_ref[hd] += _dot(vt_ref[j, hd, :, rows], p)

    def update(hd, j, buf):
        m_new = rescale(hd, buf)
        for rows in halves:
            accumulate(hd, j, buf, rows, m_new)

    def step(hd, j_next, buf_next, j_cur, buf_cur):
        m_new = rescale(hd, buf_cur)
        maxes = []
        for rows in halves:
            maxes.append(score_rows(hd, j_next, buf_next, rows))
            accumulate(hd, j_cur, buf_cur, rows, m_new)
        buf_next[1][hd] = jnp.maximum(*maxes)

    acc_ref[...] = jnp.zeros_like(acc_ref)
    m_ref[...] = jnp.full_like(m_ref, -jnp.inf)
    bufs = ((s0_ref, smax0_ref), (s1_ref, smax1_ref))

    block_of = lambda k: jnp.where(k == 0, qi, k - 1)
    for hd in range(HEADS_PER_STEP):
        diagonal_scores(hd, bufs[0])

    def consume(k, parity):
        for hd in range(HEADS_PER_STEP):
            step(hd, block_of(k + 1), bufs[1 - parity], block_of(k), bufs[parity])

    def body(pair, carry):
        consume(2 * pair, 0)
        consume(2 * pair + 1, 1)
        return carry

    def finish(parity):
        for hd in range(HEADS_PER_STEP):
            update(hd, block_of(qi), bufs[parity])

    odd = qi % 2
    lax.fori_loop(0, jnp.maximum(qi // 2 + odd - 1, 0), body, 0)

    def consume_and_finish(parity):
        for hd in range(HEADS_PER_STEP + FINISH_LAG):
            if hd < HEADS_PER_STEP:
                step(hd, block_of(qi), bufs[1 - parity], block_of(qi - 1), bufs[parity])
            if hd >= FINISH_LAG:
                update(hd - FINISH_LAG, block_of(qi), bufs[1 - parity])

    @pl.when(odd == 1)
    def _():
        consume_and_finish(0)

    @pl.when(jnp.logical_and(odd == 0, qi > 0))
    def _():
        consume(qi - 2, 0)
        consume_and_finish(1)

    @pl.when(qi == 0)
    def _():
        finish(0)

    o_ref[...] = jnp.concatenate(
        [acc_ref[hd, :MLA_V_DIM, :] / acc_ref[hd, MLA_V_DIM:MLA_V_DIM + 1, :] for hd in range(HEADS_PER_STEP)],
        axis=0).T


def _mla_call(qf, kf, vt):
    b, s, w = qf.shape
    t = ATT_BLOCK
    nblk = s // t
    assert vt.shape == (b, nblk, MLA_HEADS, V_ROWS, t) and HEADS_PER_STEP == MLA_HEADS
    return pl.pallas_call(
        _mla_kernel,
        grid=(b, nblk),
        in_specs=[
            pl.BlockSpec((None, t, w), lambda bi, qi: (bi, qi, 0)),
            pl.BlockSpec((None, t, w), lambda bi, qi: (bi, qi, 0)),
            pl.BlockSpec((None, None, MLA_HEADS, V_ROWS, t), lambda bi, qi: (bi, qi, 0, 0, 0)),
        ],
        out_specs=pl.BlockSpec((None, t, MLA_WIDTH), lambda bi, qi: (bi, qi, 0)),
        out_shape=jax.ShapeDtypeStruct((b, s, MLA_WIDTH), F32),
        scratch_shapes=[
            pltpu.VMEM((nblk, t, w), BF16),
            pltpu.VMEM((nblk, MLA_HEADS, V_ROWS, t), BF16),
            pltpu.VMEM((HEADS_PER_STEP, V_ROWS, t), F32),
            pltpu.VMEM((HEADS_PER_STEP, 1, t), F32),
            pltpu.VMEM((HEADS_PER_STEP, t, t), F32),
            pltpu.VMEM((HEADS_PER_STEP, t, t), F32),
            pltpu.VMEM((HEADS_PER_STEP, 1, t), F32),
            pltpu.VMEM((HEADS_PER_STEP, 1, t), F32),
        ],
        compiler_params=pltpu.CompilerParams(
            dimension_semantics=("arbitrary", "arbitrary"), vmem_limit_bytes=VMEM_LIMIT_BYTES),
        name="mla_attn",
    )(qf, kf, vt)


def _out_kernel(x_ref, osb_ref, omla_ref, gin_ref, wgsb_ref, wg_ref, bg_ref, wosb_ref, womla_ref, wout_ref,
                gf_ref, o_ref, *, final_norm):
    x = x_ref[...]
    d = x.shape[-1]
    h = _rms(x, gin_ref[...]).astype(BF16)
    a_sb = (osb_ref[...] * jax.nn.silu(_dot(h, wgsb_ref[...]))).astype(BF16)
    a_mla = (omla_ref[...] * jax.nn.silu(_dot(h, wg_ref[:, :MLA_WIDTH]))).astype(BF16)
    g0 = MLA_WIDTH
    g_sb = jax.nn.sigmoid(_dot(h, wg_ref[:, g0:g0 + d]) + bg_ref[:, :d])
    merged = g_sb * _dot(a_sb, wosb_ref[...])
    g_mla = jax.nn.sigmoid(_dot(h, wg_ref[:, g0 + d:g0 + 2 * d]) + bg_ref[:, d:])
    merged = merged + g_mla * _dot(a_mla, womla_ref[...])
    y = x + _dot(merged.astype(BF16), wout_ref[...])
    o_ref[...] = _rms(y, gf_ref[...]) if final_norm else y


def _out_call(x2, osb, omla, gin, wgsb, wg, bg, wosb, womla, wout, gf, final_norm):
    n, d = x2.shape
    tm = min(OUT_TILE, n)
    assert n % tm == 0
    row = lambda i: (i, 0)
    fix = lambda i: (0, 0)
    consts = (gin, wgsb, wg, bg, wosb, womla, wout, gf)
    return pl.pallas_call(
        functools.partial(_out_kernel, final_norm=final_norm),
        grid=(n // tm,),
        in_specs=[pl.BlockSpec((tm, d), row), pl.BlockSpec((tm, SB_WIDTH), row), pl.BlockSpec((tm, MLA_WIDTH), row)]
                 + [pl.BlockSpec(c.shape, fix, pipeline_mode=pl.Buffered(1)) for c in consts],
        out_specs=pl.BlockSpec((tm, d), row),
        out_shape=jax.ShapeDtypeStruct((n, d), F32),
        compiler_params=pltpu.CompilerParams(
            dimension_semantics=("arbitrary",), vmem_limit_bytes=VMEM_LIMIT_BYTES),
        name="out",
    )(x2, osb, omla, *consts)


def _rope_tables(s):
    half = MLA_ROPE_DIM // 2
    inv_freq = ROPE_THETA ** (-np.arange(half, dtype=np.float64) / half)
    ang = np.arange(s, dtype=np.float64)[:, None] * inv_freq[None, :]
    cos, sin = np.cos(ang), np.sin(ang)
    lanes = lambda lo, hi: np.concatenate(
        [np.zeros((s, MLA_NOPE_DIM)), lo, hi, np.zeros((s, LANES - MLA_QK_DIM))], axis=1).astype(np.float32)
    return jnp.asarray(lanes(cos, cos)), jnp.asarray(lanes(-sin, sin))


def _up_weights(w_q_up, w_kv_up):
    rq = w_q_up.shape[0]
    wq = w_q_up.astype(BF16).reshape(rq, MLA_HEADS, MLA_QK_DIM)
    wq = jnp.concatenate([wq, jnp.zeros((rq, MLA_HEADS, LANES - MLA_QK_DIM), BF16)], axis=-1)
    rkv = w_kv_up.shape[0]
    wkv = w_kv_up.astype(BF16).reshape(rkv, MLA_HEADS, MLA_NOPE_DIM + MLA_V_DIM)
    wkn = jnp.concatenate([wkv[..., :MLA_NOPE_DIM], jnp.zeros((rkv, MLA_HEADS, LANES - MLA_NOPE_DIM), BF16)], axis=-1)
    wv_t = wkv[..., MLA_NOPE_DIM:].reshape(rkv, MLA_WIDTH).T
    return wq.reshape(rq, MLA_HEADS * LANES), wkn.reshape(rkv, MLA_HEADS * LANES), wv_t


def kernel(x, norm_in_g, w_in, b_gate, q_norm_g, w_q_up, kv_norm_g, w_kv_up, w_o_sb, w_o_mla, w_out, norm_f_g):
    b, s, d = x.shape
    depth = w_in.shape[0]
    assert s % ATT_BLOCK == 0 and s % SB_BLOCK == 0 and TOKEN_TILE == ATT_BLOCK
    tabs = _rope_tables(s)
    idx = jnp.arange(SB_BLOCK)
    after_mat = (idx[None, :] > idx[:, None]).astype(BF16)

    w_in_t = jnp.swapaxes(w_in, 1, 2)
    x2 = x.reshape(b * s, d)
    for l in range(depth):
        wqk, wvsb_t, wgsb, wc, wckv, wg = _wprep_call(w_in_t, l)
        wq, wkn, wv_t = _up_weights(w_q_up[l], w_kv_up[l])
        gin = norm_in_g[l][None, :]
        qsb, ksb, vsbt, qf, kf, vt = _proj_call(
            x2, gin,
            (wqk, wvsb_t, wc, wckv, q_norm_g[l][None, :], kv_norm_g[l][None, :], wq, wkn, wv_t),
            tabs)
        r3 = lambda a: a.reshape(b, s, a.shape[-1])
        vsbt = vsbt.reshape(b, s // SB_BLOCK, SB_WIDTH, SB_BLOCK)
        vt = vt.reshape((b, s // ATT_BLOCK) + vt.shape[1:])
        o_sb = _sb_call(r3(qsb), r3(ksb), vsbt, after_mat).reshape(b * s, SB_WIDTH)
        o_mla = _mla_call(r3(qf), r3(kf), vt).reshape(b * s, MLA_WIDTH)
        x2 = _out_call(x2, o_sb, o_mla, gin, wgsb, wg, b_gate[l][None, :], w_o_sb[l].astype(BF16),
                       w_o_mla[l].astype(BF16), w_out[l].astype(BF16), norm_f_g[None, :],
                       final_norm=(l == depth - 1))
    return x2.reshape(b, s, d)
```
